```python
import math
import jax, jax.numpy as jnp
from jax import lax
import numpy as np

D_MODEL = 1024
BATCH = 4
SEQ = 4096
DEPTH = 2
DEC_BATCH = 128
DEC_SEQ = 4
PAST_LEN = 2048
PAGE_SIZE = 128

ATT_HEADS = 8
ATT_HEAD_DIM = 64
ATT_WIDTH = ATT_HEADS * ATT_HEAD_DIM
MOBA_BLOCK = 256
MOBA_TOPK = 3
MOBA_Q_CHUNK = 32
CONV_CH = 512
CONV_GROUPS = 8
CONV_LEN = 31
HGRN_HEADS = 8
HGRN_DK = 128
HGRN_DV = 128
HGRN_KEY_WIDTH = HGRN_HEADS * HGRN_DK
HGRN_VAL_WIDTH = HGRN_HEADS * HGRN_DV
HGRN_CHUNK = 64
EPS = 1e-6

AB_SPLITS = [ATT_WIDTH] * 4 + [CONV_CH] * 3
C_SPLITS = [HGRN_KEY_WIDTH, HGRN_KEY_WIDTH, HGRN_VAL_WIDTH, HGRN_VAL_WIDTH]

kernel_name = 'moba_conformer_hgrn2_hybrid_step'


def _split(z, sizes):
    return jnp.split(z, [int(c) for c in np.cumsum(sizes)[:-1]], axis=-1)


def _rms(x, g):
    xf = x.astype(jnp.float32)
    y = xf * lax.rsqrt(jnp.mean(xf * xf, axis=-1, keepdims=True) + EPS)
    return (y * g.astype(jnp.float32)).astype(x.dtype)


def _layernorm(x, g, b):
    xf = x.astype(jnp.float32)
    mu = jnp.mean(xf, axis=-1, keepdims=True)
    var = jnp.mean(jnp.square(xf - mu), axis=-1, keepdims=True)
    y = (xf - mu) * lax.rsqrt(var + EPS) * g.astype(jnp.float32) + b.astype(jnp.float32)
    return y.astype(x.dtype)


def _moba_combine(s_own, v_own, s_sel=None, v_sel=None):
    if s_sel is None:
        p = jax.nn.softmax(s_own, axis=-1)
        return jnp.einsum('bhqk,bhkd->bhqd', p.astype(v_own.dtype), v_own)
    shp = s_sel.shape
    n_sel = shp[-2] * shp[-1]
    s = jnp.concatenate([s_sel.reshape(shp[:-2] + (n_sel,)), s_own], axis=-1)
    p = jax.nn.softmax(s, axis=-1)
    p_sel = p[..., :n_sel].reshape(shp).astype(v_sel.dtype)
    p_own = p[..., n_sel:].astype(v_own.dtype)
    return (jnp.einsum('bhqnk,bhqnkd->bhqd', p_sel, v_sel)
            + jnp.einsum('bhqk,bhkd->bhqd', p_own, v_own))


def _moba_prompt(q, k, v):
    B, S, H, Dh = q.shape
    nb = -(-S // MOBA_BLOCK)
    pad = nb * MOBA_BLOCK - S
    scale = Dh ** -0.5

    def blocks(a):
        a = jnp.pad(a, ((0, 0), (0, pad), (0, 0), (0, 0)))
        return a.reshape(B, nb, MOBA_BLOCK, H, Dh).transpose(0, 3, 1, 2, 4)

    kb, vb = blocks(k), blocks(v)
    k_mean = jnp.mean(kb.astype(jnp.float32), axis=3)
    topk = min(MOBA_TOPK, nb)
    qh = q.transpose(0, 2, 1, 3)
    qlen = math.gcd(S, MOBA_Q_CHUNK)
    b_idx = jnp.arange(B)[:, None, None, None]
    h_idx = jnp.arange(H)[None, :, None, None]
    blk_ids = jnp.arange(nb)

    def one_chunk(c):
        t0 = c * qlen
        qc = lax.dynamic_slice_in_dim(qh, t0, qlen, axis=2)
        j = t0 // MOBA_BLOCK
        gate = jnp.einsum('bhqd,bhnd->bhqn', qc.astype(jnp.float32), k_mean)
        gate = jnp.where(blk_ids < j, gate, -jnp.inf)
        _, sel = lax.top_k(gate, topk)
        valid = sel < j
        k_sel = kb[b_idx, h_idx, sel]
        v_sel = vb[b_idx, h_idx, sel]
        s_sel = jnp.einsum('bhqd,bhqnkd->bhqnk', qc, k_sel).astype(jnp.float32) * scale
        s_sel = jnp.where(valid[..., None], s_sel, -jnp.inf)
        k_own = lax.dynamic_index_in_dim(kb, j, axis=2, keepdims=False)
        v_own = lax.dynamic_index_in_dim(vb, j, axis=2, keepdims=False)
        s_own = jnp.einsum('bhqd,bhkd->bhqk', qc, k_own).astype(jnp.float32) * scale
        q_pos = t0 + jnp.arange(qlen)
        k_pos = j * MOBA_BLOCK + jnp.arange(MOBA_BLOCK)
        s_own = jnp.where(k_pos[None, :] <= q_pos[:, None], s_own, -jnp.inf)
        return _moba_combine(s_own, v_own, s_sel, v_sel)

    out = lax.map(one_chunk, jnp.arange(S // qlen))
    return out.transpose(1, 0, 3, 2, 4).reshape(B, S, H * Dh)


def _moba_sample(q, k_new, v_new, cache_k, cache_v, page_table):
    N, T, H, Dh = q.shape
    n_pages = page_table.shape[1]
    past = n_pages * PAGE_SIZE
    ppb = MOBA_BLOCK // PAGE_SIZE
    j = past // MOBA_BLOCK
    scale = Dh ** -0.5
    qh = q.transpose(0, 2, 1, 3)
    own_pages = page_table[:, j * ppb:]
    n_own = own_pages.shape[1] * PAGE_SIZE
    k_own = jnp.concatenate([cache_k[own_pages].reshape(N, n_own, H, Dh), k_new], axis=1)
    v_own = jnp.concatenate([cache_v[own_pages].reshape(N, n_own, H, Dh), v_new], axis=1)
    k_own = k_own.transpose(0, 2, 1, 3)
    v_own = v_own.transpose(0, 2, 1, 3)
    L = k_own.shape[2]
    s_own = jnp.einsum('bhqd,bhkd->bhqk', qh, k_own).astype(jnp.float32) * scale
    q_pos = past + jnp.arange(T)
    k_pos = j * MOBA_BLOCK + jnp.arange(L)
    s_own = jnp.where(k_pos[None, :] <= q_pos[:, None], s_own, -jnp.inf)
    if j == 0:
        out = _moba_combine(s_own, v_own)
    else:
        past_pages = page_table[:, :j * ppb]
        k_mean = jnp.mean(cache_k[past_pages].astype(jnp.float32)
                          .reshape(N, j, ppb * PAGE_SIZE, H, Dh), axis=2)
        gate = jnp.einsum('bhqd,bnhd->bhqn', qh.astype(jnp.float32), k_mean)
        topk = min(MOBA_TOPK, j)
        _, sel = lax.top_k(gate, topk)
        phys = page_table[jnp.arange(N)[:, None, None, None, None],
                          sel[..., None] * ppb + jnp.arange(ppb)]
        h_idx = jnp.arange(H)[None, :, None, None, None]
        k_sel = cache_k[phys, :, h_idx].reshape(N, H, T, topk, MOBA_BLOCK, Dh)
        v_sel = cache_v[phys, :, h_idx].reshape(N, H, T, topk, MOBA_BLOCK, Dh)
        s_sel = jnp.einsum('bhqd,bhqnkd->bhqnk', qh, k_sel).astype(jnp.float32) * scale
        out = _moba_combine(s_own, v_own, s_sel, v_sel)
    return out.transpose(0, 2, 1, 3).reshape(N, T, H * Dh)


def _ab_in(x, norm_g, w_in, q_g, k_g):
    B, T, _ = x.shape
    h = _rms(x, norm_g)
    q, k, v, g_att, u_a, u_b, g_conv = _split(h @ w_in, AB_SPLITS)
    q = _rms(q.reshape(B, T, ATT_HEADS, ATT_HEAD_DIM), q_g)
    k = _rms(k.reshape(B, T, ATT_HEADS, ATT_HEAD_DIM), k_g)
    v = v.reshape(B, T, ATT_HEADS, ATT_HEAD_DIM)
    u = u_a * jax.nn.sigmoid(u_b)
    return q, k, v, g_att, u, g_conv


def _conv_branch(u, buf, conv_w, conv_b, ln_g, ln_b):
    up = jnp.concatenate([buf, u], axis=1)
    y = lax.conv_general_dilated(up, conv_w[:, None, :].astype(up.dtype), (1,), 'VALID',
                                 dimension_numbers=('NWC', 'WIO', 'NWC'),
                                 feature_group_count=CONV_CH) + conv_b.astype(up.dtype)
    y = jax.nn.silu(_layernorm(y, ln_g, ln_b))
    return y, up[:, -(CONV_LEN - 1):]


def _ab_out(x, o_att, g_att, o_conv, g_conv, w_out):
    m = jnp.concatenate([o_att * jax.nn.silu(g_att), o_conv * jax.nn.silu(g_conv)], axis=-1)
    return x + m @ w_out


def _hgrn2_scan(q, k, logf, v, S0):
    B, T, H, DK = q.shape
    DV = v.shape[-1]
    C = math.gcd(T, HGRN_CHUNK)
    n = T // C

    def to_chunks(a):
        return a.reshape(B, n, C, H, a.shape[-1]).transpose(1, 0, 3, 2, 4)

    causal = jnp.tril(jnp.ones((C, C), dtype=bool))[:, :, None]

    def step(S, inp):
        qc, kc, lc, vc = inp
        b = jnp.cumsum(lc, axis=2)
        o_inter = jnp.einsum('bhtk,bhkv->bhtv', qc * jnp.exp(b), S)
        diff = b[:, :, :, None, :] - b[:, :, None, :, :]
        decay = jnp.exp(jnp.where(causal, diff, -jnp.inf))
        a = jnp.einsum('bhtk,bhsk,bhtsk->bhts', qc, kc, decay)
        o = o_inter + jnp.einsum('bhts,bhsv->bhtv', a, vc)
        b_last = b[:, :, -1, :]
        S = (jnp.exp(b_last)[..., None] * S
             + jnp.einsum('bhsk,bhsv->bhkv', kc * jnp.exp(b_last[:, :, None, :] - b), vc))
        return S, o

    S, o = lax.scan(step, S0, (to_chunks(q), to_chunks(k), to_chunks(logf), to_chunks(v)))
    return o.transpose(1, 0, 3, 2, 4).reshape(B, T, H, DV), S


def _c_layer(x, S0, layer, norm_g, w_in, lb_logits, o_g, w_out):
    B, T, _ = x.shape
    h = _rms(x, norm_g)
    q, fz, i, g = _split(h @ w_in, C_SPLITS)
    p = jax.nn.softmax(lb_logits.astype(jnp.float32), axis=0)
    lb = (jnp.cumsum(p, axis=0) - p[0:1])[layer]
    f = lb + (1.0 - lb) * jax.nn.sigmoid(fz.astype(jnp.float32))
    logf = jnp.log(f).reshape(B, T, HGRN_HEADS, HGRN_DK)
    kk = (1.0 - f).reshape(B, T, HGRN_HEADS, HGRN_DK)
    qq = jax.nn.silu(q.astype(jnp.float32)).reshape(B, T, HGRN_HEADS, HGRN_DK)
    vv = i.astype(jnp.float32).reshape(B, T, HGRN_HEADS, HGRN_DV)
    o, S = _hgrn2_scan(qq, kk, logf, vv, S0.astype(jnp.float32))
    o = _rms(o, o_g).reshape(B, T, HGRN_VAL_WIDTH).astype(x.dtype)
    return x + (o * jax.nn.silu(g)) @ w_out, S.astype(x.dtype)


def setup_inputs(seed: int = 0) -> dict:
    key = jax.random.key(seed)
    ks = jax.random.split(key, 24)
    n_pages = PAST_LEN // PAGE_SIZE
    n_used = DEC_BATCH * n_pages
    n_pool = n_used + n_used // 4

    def nrm(k, shape, s):
        return s * jax.random.normal(k, shape, jnp.float32)

    page_table = jax.random.permutation(ks[6], n_pool)[:n_used].reshape(DEC_BATCH, n_pages).astype(jnp.int32)
    return {
        'x_prompt': nrm(ks[0], (BATCH, SEQ, D_MODEL), 1.0),
        'x_sample': nrm(ks[1], (DEC_BATCH, DEC_SEQ, D_MODEL), 1.0),
        'cache_k': nrm(ks[2], (n_pool, PAGE_SIZE, ATT_HEADS, ATT_HEAD_DIM), 1.0),
        'cache_v': nrm(ks[3], (n_pool, PAGE_SIZE, ATT_HEADS, ATT_HEAD_DIM), 1.0),
        'state_conv': nrm(ks[4], (DEC_BATCH, CONV_LEN - 1, CONV_CH), 0.5),
        'state_hgrn': nrm(ks[5], (DEC_BATCH, HGRN_HEADS, HGRN_DK, HGRN_DV), 0.3),
        'page_table': page_table,
        'norm_0': 1.0 + nrm(ks[7], (D_MODEL,), 0.02),
        'w_in_0': nrm(ks[8], (D_MODEL, sum(AB_SPLITS)), D_MODEL ** -0.5),
        'q_norm_0': 1.0 + nrm(ks[9], (ATT_HEAD_DIM,), 0.02),
        'k_norm_0': 1.0 + nrm(ks[10], (ATT_HEAD_DIM,), 0.02),
        'conv_w_0': nrm(ks[11], (CONV_LEN, CONV_CH), CONV_LEN ** -0.5),
        'conv_b_0': nrm(ks[12], (CONV_CH,), 0.01),
        'conv_ln_g_0': 1.0 + nrm(ks[13], (CONV_CH,), 0.02),
        'conv_ln_b_0': nrm(ks[14], (CONV_CH,), 0.01),
        'w_out_0': nrm(ks[15], (ATT_WIDTH + CONV_CH, D_MODEL), (ATT_WIDTH + CONV_CH) ** -0.5),
        'norm_1': 1.0 + nrm(ks[16], (D_MODEL,), 0.02),
        'w_in_1': nrm(ks[17], (D_MODEL, sum(C_SPLITS)), D_MODEL ** -0.5),
        'lb_logits': nrm(ks[18], (DEPTH, HGRN_KEY_WIDTH), 0.5),
        'o_norm_1': 1.0 + nrm(ks[19], (HGRN_DV,), 0.02),
        'w_out_1': nrm(ks[20], (HGRN_VAL_WIDTH, D_MODEL), HGRN_VAL_WIDTH ** -0.5),
    }


def reference(x_prompt, x_sample, cache_k, cache_v, state_conv, state_hgrn, page_table,
              norm_0, w_in_0, q_norm_0, k_norm_0, conv_w_0, conv_b_0, conv_ln_g_0, conv_ln_b_0, w_out_0,
              norm_1, w_in_1, lb_logits, o_norm_1, w_out_1):
    xp, xs = x_prompt, x_sample
    for layer in range(DEPTH):
        if layer % 2 == 0:
            q, k_prompt, v_prompt, ga, u, gc = _ab_in(xp, norm_0, w_in_0, q_norm_0, k_norm_0)
            oa = _moba_prompt(q, k_prompt, v_prompt)
            buf0 = jnp.zeros((xp.shape[0], CONV_LEN - 1, CONV_CH), u.dtype)
            oc, conv_prompt = _conv_branch(u, buf0, conv_w_0, conv_b_0, conv_ln_g_0, conv_ln_b_0)
            xp = _ab_out(xp, oa, ga, oc, gc, w_out_0)
            q, k_sample, v_sample, ga, u, gc = _ab_in(xs, norm_0, w_in_0, q_norm_0, k_norm_0)
            oa = _moba_sample(q, k_sample, v_sample, cache_k, cache_v, page_table)
            oc, conv_sample = _conv_branch(u, state_conv.astype(u.dtype), conv_w_0, conv_b_0,
                                           conv_ln_g_0, conv_ln_b_0)
            xs = _ab_out(xs, oa, ga, oc, gc, w_out_0)
        else:
            s0 = jnp.zeros((xp.shape[0], HGRN_HEADS, HGRN_DK, HGRN_DV), jnp.float32)
            xp, hgrn_prompt = _c_layer(xp, s0, layer, norm_1, w_in_1, lb_logits, o_norm_1, w_out_1)
            xs, hgrn_sample = _c_layer(xs, state_hgrn, layer, norm_1, w_in_1, lb_logits, o_norm_1, w_out_1)
    return (xp, xs, k_prompt, v_prompt, k_sample, v_sample, conv_prompt, conv_sample, hgrn_prompt, hgrn_sample)
```

```python
import functools

import jax
import jax.numpy as jnp
from jax import lax
from jax.experimental import pallas as pl
from jax.experimental.pallas import tpu as pltpu

EPS = 1e-6
ATT_HEADS = 8
ATT_HEAD_DIM = 64
ATT_WIDTH = ATT_HEADS * ATT_HEAD_DIM
MOBA_BLOCK = 256
MOBA_TOPK = 3
PAGE_SIZE = 128
CONV_CH = 512
CONV_LEN = 31
HGRN_HEADS = 8
HGRN_DK = 128
HGRN_DV = 128
LAYER_C = 1

LANES = 128
HEADS_PER_VREG = LANES // ATT_HEAD_DIM
NEG = -1e30
VMEM_LIMIT = 56 * 1024 * 1024
ROW_TILE = 512
SCAN_TILE = 256

F32 = jnp.float32
BF16 = jnp.bfloat16
NT_DIMS = (((1,), (1,)), ((), ()))
TN_DIMS = (((0,), (0,)), ((), ()))


def _silu(x):
    return x * jax.nn.sigmoid(x)


def _cparams(n_axes):
    return pltpu.CompilerParams(dimension_semantics=("arbitrary",) * n_axes,
                                vmem_limit_bytes=VMEM_LIMIT)


def _ab_in_kernel(x_ref, ng_ref, w_ref, qg_ref, kg_ref, gsum_ref,
                  qs_ref, k_ref, v_ref, kb_ref, vb_ref, sga_ref, u_ref, sgc_ref, kmean_ref):
    x = x_ref[...]
    ms = jnp.mean(x * x, axis=-1, keepdims=True)
    h = (x * lax.rsqrt(ms + EPS) * ng_ref[...]).astype(BF16)

    def proj(c):
        return jnp.dot(h, w_ref[:, c * ATT_WIDTH:(c + 1) * ATT_WIDTH], preferred_element_type=F32)

    def head_rms(z, g):
        zz = (z * z).astype(BF16)
        half = ATT_WIDTH // 2
        ms_h = jnp.concatenate(
            [jnp.dot(zz[:, :half], gsum_ref[...], preferred_element_type=F32),
             jnp.dot(zz[:, half:], gsum_ref[...], preferred_element_type=F32)], axis=1)
        return z * lax.rsqrt(ms_h + EPS) * g

    q = head_rms(proj(0), qg_ref[...])
    qs_ref[...] = (q * (ATT_HEAD_DIM ** -0.5)).astype(BF16)
    k = head_rms(proj(1), kg_ref[...])
    k_ref[...] = k
    kb_ref[...] = k.astype(BF16)
    tm = k.shape[0]
    kmean_ref[0] = jnp.mean(k.reshape(tm // MOBA_BLOCK, MOBA_BLOCK, ATT_WIDTH), axis=1)
    v = proj(2)
    v_ref[...] = v
    vb_ref[...] = v.astype(BF16)
    sga_ref[...] = _silu(proj(3)).astype(BF16)
    u_ref[...] = proj(4) * jax.nn.sigmoid(proj(5))
    sgc_ref[...] = _silu(proj(6)).astype(BF16)


def _ab_in(x2d, ng, w_bf, qg, kg, gsum, tm):
    t, d = x2d.shape
    wn = w_bf.shape[1]
    row = lambda i: (i, 0)
    const = lambda i: (0, 0)
    tile = lambda dt: jax.ShapeDtypeStruct((t, ATT_WIDTH), dt)
    out_shape = (tile(BF16), tile(F32), tile(F32), tile(BF16), tile(BF16), tile(BF16), tile(F32), tile(BF16),
                 jax.ShapeDtypeStruct((t // tm, tm // MOBA_BLOCK, ATT_WIDTH), F32))
    tspec = pl.BlockSpec((tm, ATT_WIDTH), row)
    return pl.pallas_call(
        _ab_in_kernel,
        grid=(t // tm,),
        in_specs=[pl.BlockSpec((tm, d), row), pl.BlockSpec((1, d), const), pl.BlockSpec((d, wn), const),
                  pl.BlockSpec((1, ATT_WIDTH), const), pl.BlockSpec((1, ATT_WIDTH), const),
                  pl.BlockSpec(gsum.shape, const)],
        out_specs=(tspec,) * 8 + (pl.BlockSpec((1, tm // MOBA_BLOCK, ATT_WIDTH), lambda i: (i, 0, 0)),),
        out_shape=out_shape,
        compiler_params=_cparams(1),
        name="ab_in",
    )(x2d, ng, w_bf, qg, kg, gsum)


def _moba_prompt_kernel(qs_ref, kb_ref, vb_ref, kmean_ref, sga_ref, o_ref, *, nb):
    j = pl.program_id(2)
    blk = MOBA_BLOCK
    q2 = qs_ref[0]
    km = kmean_ref[0]
    lane = lax.broadcasted_iota(jnp.int32, (blk, LANES), 1)
    r_i = lax.broadcasted_iota(jnp.int32, (blk, blk), 0)
    c_i = lax.broadcasted_iota(jnp.int32, (blk, blk), 1)
    causal_bias = jnp.where(c_i <= r_i, 0.0, NEG).astype(F32)
    b_i = lax.broadcasted_iota(jnp.int32, (nb, blk), 0)
    valid = b_i < j
    kd = kb_ref[0, pl.ds(pl.multiple_of(j * blk, blk), blk), :]
    vd = vb_ref[0, pl.ds(pl.multiple_of(j * blk, blk), blk), :]

    outs = []
    for hh in range(HEADS_PER_VREG):
        head_mask = jnp.where((lane // ATT_HEAD_DIM) == hh, 1.0, 0.0).astype(BF16)
        qh = q2 * head_mask
        g = lax.dot_general(km, qh.astype(F32), NT_DIMS, precision=lax.Precision.HIGHEST,
                            preferred_element_type=F32)
        g = jnp.where(valid, g, -jnp.inf)
        rank = jnp.zeros((nb, blk), F32)
        for i in range(nb):
            gi = g[i:i + 1, :]
            beats = jnp.where(gi > g, 1.0, jnp.where((gi == g) & (b_i > i), 1.0, 0.0))
            rank = rank + beats
        sel = jnp.where(valid & (rank < MOBA_TOPK), 1.0, 0.0)
        sel_t = jnp.concatenate([sel, jnp.zeros((LANES - nb, blk), F32)], axis=0).T
        bias_q = jnp.where(sel_t > 0.5, 0.0, NEG).astype(BF16)
        qcat = jnp.concatenate([qh, bias_q], axis=1)

        s = lax.dot_general(qh, kd, NT_DIMS, preferred_element_type=F32) + causal_bias
        m = jnp.max(s, axis=1, keepdims=True)
        p = jnp.exp(s - m)
        l = jnp.sum(p, axis=1, keepdims=True)
        acc = jnp.dot(p.astype(BF16), vd, preferred_element_type=F32)

        def body(jp, carry):
            m, l, acc = carry
            off = pl.multiple_of(jp * blk, blk)
            kj = kb_ref[0, pl.ds(off, blk), :]
            vj = vb_ref[0, pl.ds(off, blk), :]
            onehot = jnp.where(lane == jp, 1.0, 0.0).astype(BF16)
            kcat = jnp.concatenate([kj, onehot], axis=1)
            s = lax.dot_general(qcat, kcat, NT_DIMS, preferred_element_type=F32)
            m_new = jnp.maximum(m, jnp.max(s, axis=1, keepdims=True))
            alpha = jnp.exp(m - m_new)
            p = jnp.exp(s - m_new)
            l = alpha * l + jnp.sum(p, axis=1, keepdims=True)
            acc = alpha * acc + jnp.dot(p.astype(BF16), vj, preferred_element_type=F32)
            return m_new, l, acc

        m, l, acc = lax.fori_loop(0, j, body, (m, l, acc))
        outs.append(acc / l)
    o = jnp.where(lane < ATT_HEAD_DIM, outs[0], outs[1])
    o_ref[0] = (o * sga_ref[0].astype(F32)).astype(BF16)


def _moba_prompt(qs, kb, vb, kmean, sga):
    b, s, _ = qs.shape
    nb = s // MOBA_BLOCK
    npair = ATT_WIDTH // LANES
    qmap = lambda bi, hp, j: (bi, j, hp)
    kvmap = lambda bi, hp, j: (bi, 0, hp)
    return pl.pallas_call(
        functools.partial(_moba_prompt_kernel, nb=nb),
        grid=(b, npair, nb),
        in_specs=[pl.BlockSpec((1, MOBA_BLOCK, LANES), qmap),
                  pl.BlockSpec((1, s, LANES), kvmap), pl.BlockSpec((1, s, LANES), kvmap),
                  pl.BlockSpec((1, nb, LANES), kvmap),
                  pl.BlockSpec((1, MOBA_BLOCK, LANES), qmap)],
        out_specs=pl.BlockSpec((1, MOBA_BLOCK, LANES), qmap),
        out_shape=jax.ShapeDtypeStruct((b, s, ATT_WIDTH), BF16),
        compiler_params=_cparams(3),
        name="moba_prompt",
    )(qs, kb, vb, kmean, sga)


CONV_HALO = 32
CONV_ROWS = 64


SUBLANES = 8
CONV_LEAD = CONV_HALO - (CONV_LEN - 1)
CONV_SPAN = -(-(CONV_LEAD + SUBLANES - 1 + CONV_LEN) // SUBLANES)


def _conv_tap_table(conv_w):
    m = jnp.arange(CONV_SPAN)[:, None, None]
    s = jnp.arange(SUBLANES)[None, :, None]
    r = jnp.arange(SUBLANES)[None, None, :]
    idx = SUBLANES * m + s - r - CONV_LEAD
    ok = (idx >= 0) & (idx < CONV_LEN)
    tab = jnp.where(ok[..., None], conv_w[jnp.clip(idx, 0, CONV_LEN - 1)], 0.0)
    return tab.reshape(CONV_SPAN * SUBLANES, SUBLANES, conv_w.shape[-1])


def _tap_used(ms):
    lo = ms - (SUBLANES - 1) - CONV_LEAD
    return lo + SUBLANES - 1 >= 0 and lo < CONV_LEN


def _conv_rows(xpad_ref, ybuf_ref, tab_ref, cb_ref, n_rows):
    groups = CONV_ROWS // SUBLANES
    for lt in range(CONV_CH // LANES):
        ls = slice(lt * LANES, (lt + 1) * LANES)

        def chunk(c, carry, ls=ls):
            r0 = pl.multiple_of(c * CONV_ROWS, CONV_ROWS)
            win_ref = xpad_ref.at[pl.ds(r0, CONV_ROWS + CONV_SPAN * SUBLANES)]
            taps = {ms: tab_ref[ms, :, ls] for ms in range(CONV_SPAN * SUBLANES) if _tap_used(ms)}
            acc = [jnp.broadcast_to(cb_ref[:, ls], (SUBLANES, LANES))] * groups
            for rho in range(CONV_ROWS + CONV_SPAN * SUBLANES):
                users = [(rho // SUBLANES - m, SUBLANES * m + rho % SUBLANES) for m in range(CONV_SPAN)]
                users = [(g, ms) for g, ms in users if 0 <= g < groups and ms in taps]
                if not users:
                    continue
                xb = win_ref[pl.ds(rho, 1), ls]
                for g, ms in users:
                    acc[g] = acc[g] + taps[ms] * xb
            ybuf_ref[pl.ds(r0, CONV_ROWS), ls] = jnp.concatenate(acc, axis=0)
            return carry

        lax.fori_loop(0, n_rows // CONV_ROWS, chunk, 0)


def _ln_silu_gate(y, lg_ref, lb_ref, sgc):
    mu = jnp.mean(y, axis=-1, keepdims=True)
    cen = y - mu
    var = jnp.mean(cen * cen, axis=-1, keepdims=True)
    yn = cen * lax.rsqrt(var + EPS) * lg_ref[...] + lb_ref[...]
    return (_silu(yn) * sgc.astype(F32)).astype(BF16)


def _ab_out_prompt_kernel(x_ref, matt_ref, u_ref, halo_ref, sgc_ref, tab_ref, cb_ref, lg_ref, lb_ref, w_ref,
                          y_ref, xpad_ref, ybuf_ref, m_ref):
    t = pl.program_id(1)
    tm = u_ref.shape[1]
    halo = halo_ref[0]
    xpad_ref[0:CONV_HALO, :] = jnp.where(t == 0, jnp.zeros_like(halo), halo)
    xpad_ref[CONV_HALO:CONV_HALO + tm, :] = u_ref[0]
    m_ref[:, 0:ATT_WIDTH] = matt_ref[0]
    _conv_rows(xpad_ref, ybuf_ref, tab_ref, cb_ref, tm)

    def chunk(c, carry):
        r0 = pl.multiple_of(c * CONV_ROWS, CONV_ROWS)
        m_ref[pl.ds(r0, CONV_ROWS), ATT_WIDTH:] = _ln_silu_gate(
            ybuf_ref[pl.ds(r0, CONV_ROWS), :], lg_ref, lb_ref, sgc_ref[0, pl.ds(r0, CONV_ROWS), :])
        return carry

    lax.fori_loop(0, tm // CONV_ROWS, chunk, 0, unroll=2)
    y_ref[0] = x_ref[0] + jnp.dot(m_ref[...], w_ref[...], preferred_element_type=F32)


def _ab_out_prompt(x, matt, u, sgc, tab, cb, lg, lb, w_bf, tm):
    b, s, d = x.shape
    hpt = tm // CONV_HALO
    tmap = lambda bi, t: (bi, t, 0)
    hmap = lambda bi, t: (bi, jnp.maximum(t * hpt - 1, 0), 0)
    const = lambda bi, t: (0, 0)
    return pl.pallas_call(
        _ab_out_prompt_kernel,
        grid=(b, s // tm),
        in_specs=[pl.BlockSpec((1, tm, d), tmap), pl.BlockSpec((1, tm, ATT_WIDTH), tmap),
                  pl.BlockSpec((1, tm, CONV_CH), tmap), pl.BlockSpec((1, CONV_HALO, CONV_CH), hmap),
                  pl.BlockSpec((1, tm, CONV_CH), tmap),
                  pl.BlockSpec(tab.shape, lambda bi, t: (0, 0, 0)), pl.BlockSpec((1, CONV_CH), const),
                  pl.BlockSpec((1, CONV_CH), const), pl.BlockSpec((1, CONV_CH), const),
                  pl.BlockSpec(w_bf.shape, const)],
        out_specs=pl.BlockSpec((1, tm, d), tmap),
        out_shape=jax.ShapeDtypeStruct((b, s, d), F32),
        scratch_shapes=[pltpu.VMEM((tm + CONV_SPAN * SUBLANES, CONV_CH), F32),
                        pltpu.VMEM((tm, CONV_CH), F32),
                        pltpu.VMEM((tm, ATT_WIDTH + CONV_CH), BF16)],
        compiler_params=_cparams(2),
        name="ab_out_prompt",
    )(x, matt, u, u, sgc, tab, cb, lg, lb, w_bf)


C_COLS = 512


def _c_in_kernel(x_ref, ng_ref, w_ref, lbl_ref, qq_ref, kk_ref, lf_ref, vv_ref, sg_ref):
    x = x_ref[...]
    ms = jnp.mean(x * x, axis=-1, keepdims=True)
    h = (x * lax.rsqrt(ms + EPS) * ng_ref[...]).astype(BF16)
    p = jax.nn.softmax(lbl_ref[...], axis=0)
    lb = jnp.sum(p[0:LAYER_C + 1], axis=0, keepdims=True) - p[0:1]
    key_w = qq_ref.shape[1]
    per = key_w // C_COLS
    for c in range(per):
        cs = slice(c * C_COLS, (c + 1) * C_COLS)

        def proj(g, cs=cs):
            return jnp.dot(h, w_ref[:, g * key_w + cs.start:g * key_w + cs.stop], preferred_element_type=F32)

        qq_ref[:, cs] = _silu(proj(0)).astype(BF16)
        lbc = lb[:, cs]
        f = lbc + (1.0 - lbc) * jax.nn.sigmoid(proj(1))
        lf_ref[:, cs] = jnp.log(f)
        kk_ref[:, cs] = (1.0 - f).astype(BF16)
        vv_ref[:, cs] = proj(2).astype(BF16)
        sg_ref[:, cs] = _silu(proj(3)).astype(BF16)


def _c_in(x2d, ng, w_bf, lbl, tm):
    t, d = x2d.shape
    key_w = lbl.shape[1]
    row = lambda i: (i, 0)
    const = lambda i: (0, 0)
    tile = lambda dt: jax.ShapeDtypeStruct((t, key_w), dt)
    tspec = pl.BlockSpec((tm, key_w), row)
    return pl.pallas_call(
        _c_in_kernel,
        grid=(t // tm,),
        in_specs=[pl.BlockSpec((tm, d), row), pl.BlockSpec((1, d), const), pl.BlockSpec(w_bf.shape, const),
                  pl.BlockSpec(lbl.shape, const)],
        out_specs=(tspec,) * 5,
        out_shape=(tile(BF16), tile(BF16), tile(F32), tile(BF16), tile(BF16)),
        compiler_params=_cparams(1),
        name="c_in",
    )(x2d, ng, w_bf, lbl)


HGRN_SUB = SUBLANES


def _split3(x):
    hi = x.astype(BF16)
    r1 = x - hi.astype(F32)
    mid = r1.astype(BF16)
    lo = (r1 - mid.astype(F32)).astype(BF16)
    return hi, mid, lo


def _chunk_masks(c):
    t_i = lax.broadcasted_iota(jnp.int32, (c, c), 0)
    s_i = lax.broadcasted_iota(jnp.int32, (c, c), 1)
    same = (t_i // HGRN_SUB) == (s_i // HGRN_SUB)
    diag = [same & (t_i - s_i == d) for d in range(HGRN_SUB)]
    col_group = [(s_i // HGRN_SUB) == a for a in range(c // HGRN_SUB)]
    return diag, col_group


def _hgrn_chunk(q, k, v_b, b, masks, o_inter_fn):
    c = q.shape[0]
    n_sub = c // HGRN_SUB
    diag, col_group = masks
    b_last = b[c - 1:c, :]
    o = o_inter_fn((q * jnp.exp(b)).astype(BF16))
    kt = (k * jnp.exp(b_last - b)).astype(BF16)

    k_b = k.astype(BF16)
    stack = [q.astype(BF16)]
    for d in range(1, HGRN_SUB):
        stack.append((q * jnp.exp(jnp.minimum(b - pltpu.roll(b, d, 0), 0.0))).astype(BF16))
    zd = lax.dot_general(jnp.concatenate(stack, axis=0), k_b, NT_DIMS, preferred_element_type=F32)
    a = jnp.zeros((c, c), F32)
    for d in range(HGRN_SUB):
        a = a + jnp.where(diag[d], zd[d * c:(d + 1) * c, :], 0.0)

    if n_sub > 1:
        b3 = b.reshape(n_sub, HGRN_SUB, b.shape[1])
        b_end = jnp.broadcast_to(b3[:, HGRN_SUB - 1:HGRN_SUB, :], b3.shape).reshape(b.shape)
        ksc = (k * jnp.exp(b_end - b)).astype(BF16)
        stack = []
        for g in range(1, n_sub):
            r = g * HGRN_SUB
            stack.append((q[r:, :] * jnp.exp(b[r:, :] - b[r - 1:r, :])).astype(BF16))
        zo = lax.dot_general(jnp.concatenate(stack, axis=0), ksc, NT_DIMS, preferred_element_type=F32)
        row = 0
        for g in range(1, n_sub):
            r = g * HGRN_SUB
            part = jnp.concatenate([jnp.zeros((r, c), F32), zo[row:row + c - r, :]], axis=0)
            a = a + jnp.where(col_group[g - 1], part, 0.0)
            row += c - r

    o = o + jnp.dot(a.astype(BF16), v_b, preferred_element_type=F32)
    return o, kt, jnp.exp(b_last)


def _head_norm_gate(o, og, sg):
    ms = jnp.mean(o * o, axis=-1, keepdims=True)
    return ((o * lax.rsqrt(ms + EPS) * og) * sg.astype(F32)).astype(BF16)


HGRN_CHUNK = 64


def _c_scan_prompt_kernel(qq_ref, kk_ref, lf_ref, vv_ref, sg_ref, y0_ref, og_ref, w_ref,
                          y_ref, s_ref, st_ref, m_ref):
    t = pl.program_id(1)
    ts = qq_ref.shape[1]
    c = HGRN_CHUNK
    dk, dv = HGRN_DK, HGRN_DV

    @pl.when(t == 0)
    def _():
        st_ref[...] = jnp.zeros_like(st_ref)

    def chunk(ci, carry):
        masks = _chunk_masks(c)
        r_i = lax.broadcasted_iota(jnp.int32, (c, c), 0)
        c_i = lax.broadcasted_iota(jnp.int32, (c, c), 1)
        tri = jnp.where(c_i <= r_i, 1.0, 0.0).astype(BF16)
        r0 = pl.multiple_of(ci * c, c)
        rows = pl.ds(r0, c)
        b_all = sum(jnp.dot(tri, part, preferred_element_type=F32) for part in _split3(lf_ref[0, rows, :]))
        for h in range(HGRN_HEADS):
            ks = slice(h * dk, (h + 1) * dk)
            vs = slice(h * dv, (h + 1) * dv)
            st = st_ref[h]
            st_b = st.astype(BF16)
            o, kt, decay = _hgrn_chunk(
                qq_ref[0, rows, ks].astype(F32), kk_ref[0, rows, ks].astype(F32), vv_ref[0, rows, vs],
                b_all[:, ks], masks,
                lambda qt, st_b=st_b: lax.dot_general(qt, st_b, NT_DIMS, preferred_element_type=F32))
            st_ref[h] = st * decay + lax.dot_general(vv_ref[0, rows, vs], kt, TN_DIMS, preferred_element_type=F32)
            m_ref[rows, vs] = _head_norm_gate(o, og_ref[...], sg_ref[0, rows, vs])
        return carry

    lax.fori_loop(0, ts // c, chunk, 0)
    y_ref[0] = y0_ref[0] + jnp.dot(m_ref[...], w_ref[...], preferred_element_type=F32)

    @pl.when(t == pl.num_programs(1) - 1)
    def _():
        for h in range(HGRN_HEADS):
            s_ref[0, h] = st_ref[h].T


def _c_scan_prompt(qq, kk, lf, vv, sg, y0, og, w_bf, ts):
    b, s, d = y0.shape
    key_w, val_w = qq.shape[2], vv.shape[2]
    tmap = lambda bi, t: (bi, t, 0)
    const = lambda bi, t: (0, 0)
    return pl.pallas_call(
        _c_scan_prompt_kernel,
        grid=(b, s // ts),
        in_specs=[pl.BlockSpec((1, ts, key_w), tmap), pl.BlockSpec((1, ts, key_w), tmap),
                  pl.BlockSpec((1, ts, key_w), tmap), pl.BlockSpec((1, ts, val_w), tmap),
                  pl.BlockSpec((1, ts, val_w), tmap), pl.BlockSpec((1, ts, d), tmap),
                  pl.BlockSpec((1, HGRN_DV), const), pl.BlockSpec(w_bf.shape, const)],
        out_specs=(pl.BlockSpec((1, ts, d), tmap),
                   pl.BlockSpec((1, HGRN_HEADS, HGRN_DK, HGRN_DV), lambda bi, t: (bi, 0, 0, 0))),
        out_shape=(jax.ShapeDtypeStruct((b, s, d), F32),
                   jax.ShapeDtypeStruct((b, HGRN_HEADS, HGRN_DK, HGRN_DV), F32)),
        scratch_shapes=[pltpu.VMEM((HGRN_HEADS, HGRN_DV, HGRN_DK), F32),
                        pltpu.VMEM((ts, val_w), BF16)],
        compiler_params=_cparams(2),
        name="c_scan_prompt",
    )(qq, kk, lf, vv, sg, y0, og, w_bf)


def _moba_sample_kernel(pt_ref, qs_ref, kn_ref, vn_ref, sga_ref, ck_hbm, cv_hbm, o_ref,
                        kbuf, vbuf, sem, *, n_pages):
    n = pl.program_id(0)
    n_seq = pl.num_programs(0)
    past = n_pages * PAGE_SIZE
    nb = past // MOBA_BLOCK
    slot = lax.rem(n, 2)

    def page_copies(seq, sl):
        out = []
        for p in range(n_pages):
            page = pt_ref[seq, p]
            rows = pl.ds(p * PAGE_SIZE, PAGE_SIZE)
            out.append(pltpu.make_async_copy(ck_hbm.at[page], kbuf.at[sl, rows], sem.at[0, sl]))
            out.append(pltpu.make_async_copy(cv_hbm.at[page], vbuf.at[sl, rows], sem.at[1, sl]))
        return out

    @pl.when(n == 0)
    def _():
        for cp in page_copies(0, 0):
            cp.start()

    @pl.when(n + 1 < n_seq)
    def _():
        for cp in page_copies(n + 1, 1 - slot):
            cp.start()

    for cp in page_copies(n, slot):
        cp.wait()

    hd = ATT_HEADS
    rows_q = qs_ref.shape[1]
    k_all = kbuf[slot]
    kmean = jnp.mean(k_all.reshape(nb, MOBA_BLOCK, hd, ATT_HEAD_DIM), axis=1).reshape(nb * hd, ATT_HEAD_DIM)
    kf = k_all.reshape(past * hd, ATT_HEAD_DIM).astype(BF16)
    vf = vbuf[slot].reshape(past * hd, ATT_HEAD_DIM).astype(BF16)
    q_b = qs_ref[0]

    gx = lax.dot_general(q_b.astype(F32), kmean, NT_DIMS, precision=lax.Precision.HIGHEST,
                         preferred_element_type=F32)
    g_row_h = lax.broadcasted_iota(jnp.int32, gx.shape, 0) % hd
    g_lane = lax.broadcasted_iota(jnp.int32, gx.shape, 1)
    gates = [jnp.sum(jnp.where(g_lane == i * hd + g_row_h, gx, 0.0), axis=1, keepdims=True) for i in range(nb)]
    b_i = lax.broadcasted_iota(jnp.int32, (rows_q, LANES), 1)
    gate = jnp.full((rows_q, LANES), -jnp.inf, F32)
    for i in range(nb):
        gate = jnp.where(b_i == i, gates[i], gate)
    rank = jnp.zeros((rows_q, LANES), F32)
    for i in range(nb):
        gi = gates[i]
        rank = rank + jnp.where(gi > gate, 1.0, jnp.where((gi == gate) & (b_i > i), 1.0, 0.0))
    sel_bias = jnp.where(rank < min(MOBA_TOPK, nb), 0.0, NEG)

    row_h = lax.broadcasted_iota(jnp.int32, (rows_q, LANES), 0) % hd
    head_bias = jnp.where(b_i % hd == row_h, 0.0, NEG)
    blk_lanes = MOBA_BLOCK * hd
    s = lax.dot_general(q_b, kf, NT_DIMS, preferred_element_type=F32)
    s = jnp.concatenate(
        [s[:, j * blk_lanes:(j + 1) * blk_lanes]
         + jnp.concatenate([head_bias + sel_bias[:, j:j + 1]] * (blk_lanes // LANES), axis=1)
         for j in range(nb)], axis=1)

    kn = kn_ref[0].astype(BF16)
    vn = vn_ref[0].astype(BF16)
    s_own = lax.dot_general(q_b, kn, NT_DIMS, preferred_element_type=F32)
    o_row = lax.broadcasted_iota(jnp.int32, s_own.shape, 0)
    o_col = lax.broadcasted_iota(jnp.int32, s_own.shape, 1)
    s_own = jnp.where((o_col % hd == o_row % hd) & (o_col // hd <= o_row // hd), s_own, NEG)

    m = jnp.maximum(jnp.max(s, axis=1, keepdims=True), jnp.max(s_own, axis=1, keepdims=True))
    p = jnp.exp(s - m)
    p_own = jnp.exp(s_own - m)
    l = jnp.sum(p, axis=1, keepdims=True) + jnp.sum(p_own, axis=1, keepdims=True)
    o = (jnp.dot(p.astype(BF16), vf, preferred_element_type=F32)
         + jnp.dot(p_own.astype(BF16), vn, preferred_element_type=F32)) / l
    o_ref[0] = o * sga_ref[0]


def _moba_sample(page_table, qs, k_new, v_new, sga, cache_k, cache_v):
    n_seq, rows_q, _ = qs.shape
    n_pages = page_table.shape[1]
    past = n_pages * PAGE_SIZE
    assert past % MOBA_BLOCK == 0, "the sample group's past must end on a MoBA block boundary"
    seq = lambda n, pt: (n, 0, 0)
    sspec = pl.BlockSpec((1, rows_q, ATT_HEAD_DIM), seq)
    page_buf = pltpu.VMEM((2, past, ATT_HEADS, ATT_HEAD_DIM), F32)
    grid_spec = pltpu.PrefetchScalarGridSpec(
        num_scalar_prefetch=1,
        grid=(n_seq,),
        in_specs=[sspec, sspec, sspec, sspec,
                  pl.BlockSpec(memory_space=pl.ANY), pl.BlockSpec(memory_space=pl.ANY)],
        out_specs=sspec,
        scratch_shapes=[page_buf, page_buf, pltpu.SemaphoreType.DMA((2, 2))],
    )
    return pl.pallas_call(
        functools.partial(_moba_sample_kernel, n_pages=n_pages),
        grid_spec=grid_spec,
        out_shape=jax.ShapeDtypeStruct((n_seq, rows_q, ATT_HEAD_DIM), F32),
        compiler_params=_cparams(1),
        name="moba_sample",
    )(page_table, qs, k_new, v_new, sga, cache_k, cache_v)


def _ab_out_sample_kernel(x_ref, matt_ref, u_ref, buf_ref, sgc_ref, tab_ref, cb_ref, lg_ref, lb_ref, w_ref,
                          y_ref, xpad_ref, m_ref):
    n_seq, t_new, _ = u_ref.shape
    hist = buf_ref.shape[1]
    per = SUBLANES // t_new
    assert per * t_new == SUBLANES and n_seq % per == 0 and hist == CONV_LEN - 1
    span = CONV_SPAN * SUBLANES
    xpad_ref[...] = jnp.zeros_like(xpad_ref)
    row8 = lax.broadcasted_iota(jnp.int32, (SUBLANES, CONV_CH), 0)

    def group(gi, carry):
        y8 = jnp.zeros((SUBLANES, CONV_CH), F32)
        for j in range(per):
            sq = gi * per + j
            off = j * t_new
            xpad_ref[j, CONV_HALO - hist + off:CONV_HALO + off, :] = buf_ref[sq]
            xpad_ref[j, CONV_HALO + off:CONV_HALO + off + t_new, :] = u_ref[sq]
            cols = []
            for lt in range(CONV_CH // LANES):
                ls = slice(lt * LANES, (lt + 1) * LANES)
                acc = jnp.broadcast_to(cb_ref[:, ls], (SUBLANES, LANES))
                for ms in range(span):
                    if _tap_used(ms):
                        acc = acc + tab_ref[ms, :, ls] * xpad_ref[j, pl.ds(ms, 1), ls]
                cols.append(acc)
            yj = jnp.concatenate(cols, axis=1)
            y8 = jnp.where((row8 >= off) & (row8 < off + t_new), yj, y8)
        r0 = pl.multiple_of(gi * SUBLANES, SUBLANES)
        rows = pl.ds(r0, SUBLANES)
        m_ref[rows, 0:ATT_WIDTH] = matt_ref[rows, :]
        m_ref[rows, ATT_WIDTH:] = _ln_silu_gate(y8, lg_ref, lb_ref, sgc_ref[rows, :]).astype(F32)
        return carry

    lax.fori_loop(0, n_seq // per, group, 0)
    y_ref[...] = x_ref[...] + jnp.dot(m_ref[...].astype(BF16), w_ref[...], preferred_element_type=F32)


def _ab_out_sample(x2d, matt2d, u3, buf, sgc2d, tab, cb, lg, lb, w_bf):
    t, d = x2d.shape
    n_seq, t_new, _ = u3.shape
    per = SUBLANES // t_new
    vm = lambda: pl.BlockSpec(memory_space=pltpu.VMEM)
    return pl.pallas_call(
        _ab_out_sample_kernel,
        in_specs=[vm() for _ in range(10)],
        out_specs=vm(),
        out_shape=jax.ShapeDtypeStruct((t, d), F32),
        scratch_shapes=[pltpu.VMEM((per, CONV_HALO + 2 * SUBLANES, CONV_CH), F32),
                        pltpu.VMEM((t, ATT_WIDTH + CONV_CH), F32)],
        compiler_params=pltpu.CompilerParams(vmem_limit_bytes=VMEM_LIMIT),
        name="ab_out_sample",
    )(x2d, matt2d, u3, buf, sgc2d, tab, cb, lg, lb, w_bf)


def _c_scan_sample_kernel(qq_ref, kk_ref, lf_ref, vv_ref, sg_ref, y0_ref, s0_ref, og_ref, w_ref,
                          y_ref, s_ref, *, t_new):
    per = SUBLANES // t_new
    c = SUBLANES
    dk, dv = HGRN_DK, HGRN_DV
    masks = _chunk_masks(c)
    r_i = lax.broadcasted_iota(jnp.int32, (c, c), 0)
    c_i = lax.broadcasted_iota(jnp.int32, (c, c), 1)
    tri = jnp.where(c_i <= r_i, 1.0, 0.0).astype(BF16)
    row = lax.broadcasted_iota(jnp.int32, (c, 1), 0)
    lf = lf_ref[0]
    m_heads = [jnp.zeros((c, dv), BF16)] * HGRN_HEADS
    for j in range(per):
        mine = (row >= j * t_new) & (row < (j + 1) * t_new)
        keep = jnp.where(mine, 1.0, 0.0)
        b_all = sum(jnp.dot(tri, part, preferred_element_type=F32) for part in _split3(lf * keep))
        for h in range(HGRN_HEADS):
            ks = slice(h * dk, (h + 1) * dk)
            vs = slice(h * dv, (h + 1) * dv)
            st = s0_ref[j, h]
            st_b = st.astype(BF16)
            v_b = (vv_ref[0, :, vs].astype(F32) * keep).astype(BF16)
            o, kt, decay = _hgrn_chunk(
                qq_ref[0, :, ks].astype(F32) * keep, kk_ref[0, :, ks].astype(F32) * keep, v_b, b_all[:, ks], masks,
                lambda qt, st_b=st_b: jnp.dot(qt, st_b, preferred_element_type=F32))
            decay_col = jnp.broadcast_to(decay, (dv, dk)).T
            s_ref[j, h] = st * decay_col + lax.dot_general(kt, v_b, TN_DIMS, preferred_element_type=F32)
            m_heads[h] = m_heads[h] + _head_norm_gate(o, og_ref[...], sg_ref[0, :, vs]).astype(F32)
    m1 = jnp.concatenate(m_heads, axis=1).astype(BF16)
    y_ref[0] = y0_ref[0] + jnp.dot(m1, w_ref[...], preferred_element_type=F32)


def _c_scan_sample(qq, kk, lf, vv, sg, y0, s0, og, w_bf, t_new):
    groups, rows, d = y0.shape
    per = SUBLANES // t_new
    key_w, val_w = qq.shape[2], vv.shape[2]
    gmap = lambda g: (g, 0, 0)
    const = lambda g: (0, 0)
    smap = lambda g: (g, 0, 0, 0)
    sspec = pl.BlockSpec((per, HGRN_HEADS, HGRN_DK, HGRN_DV), smap)
    return pl.pallas_call(
        functools.partial(_c_scan_sample_kernel, t_new=t_new),
        grid=(groups,),
        in_specs=[pl.BlockSpec((1, rows, key_w), gmap), pl.BlockSpec((1, rows, key_w), gmap),
                  pl.BlockSpec((1, rows, key_w), gmap), pl.BlockSpec((1, rows, val_w), gmap),
                  pl.BlockSpec((1, rows, val_w), gmap), pl.BlockSpec((1, rows, d), gmap), sspec,
                  pl.BlockSpec((1, HGRN_DV), const), pl.BlockSpec(w_bf.shape, const)],
        out_specs=(pl.BlockSpec((1, rows, d), gmap), sspec),
        out_shape=(jax.ShapeDtypeStruct((groups, rows, d), F32),
                   jax.ShapeDtypeStruct(s0.shape, F32)),
        compiler_params=_cparams(1),
        name="c_scan_sample",
    )(qq, kk, lf, vv, sg, y0, s0, og, w_bf)


def kernel(x_prompt, x_sample, cache_k, cache_v, state_conv, state_hgrn, page_table, norm_0, w_in_0, q_norm_0, k_norm_0, conv_w_0, conv_b_0, conv_ln_g_0, conv_ln_b_0, w_out_0, norm_1, w_in_1, lb_logits, o_norm_1, w_out_1):
    b, s, d = x_prompt.shape
    n_seq, t_new, _ = x_sample.shape
    n_tok = n_seq * t_new
    hist = CONV_LEN - 1
    gsum = jnp.kron(jnp.eye(ATT_WIDTH // 2 // ATT_HEAD_DIM, dtype=F32),
                    jnp.full((ATT_HEAD_DIM, ATT_HEAD_DIM), 1.0 / ATT_HEAD_DIM, F32)).astype(BF16)
    qg = jnp.tile(q_norm_0, ATT_HEADS)[None]
    kg = jnp.tile(k_norm_0, ATT_HEADS)[None]
    w_in_0b = w_in_0.astype(BF16)
    w_out_0b = w_out_0.astype(BF16)
    w_in_1b = w_in_1.astype(BF16)
    w_out_1b = w_out_1.astype(BF16)
    tab = _conv_tap_table(conv_w_0)
    conv_args = (tab, conv_b_0[None], conv_ln_g_0[None], conv_ln_b_0[None], w_out_0b)
    heads = lambda a, lead: a.reshape(lead + (ATT_HEADS, ATT_HEAD_DIM))

    qs, k_p, v_p, kb, vb, sga, u_p, sgc, kmean = _ab_in(
        x_prompt.reshape(b * s, d), norm_0[None], w_in_0b, qg, kg, gsum, ROW_TILE)
    r3 = lambda a: a.reshape(b, s, a.shape[-1])
    matt = _moba_prompt(r3(qs), r3(kb), r3(vb), kmean.reshape(b, s // MOBA_BLOCK, ATT_WIDTH), r3(sga))
    y0 = _ab_out_prompt(x_prompt, matt, r3(u_p), r3(sgc), *conv_args, ROW_TILE)
    qq, kk, lf, vv, sg = _c_in(y0.reshape(b * s, d), norm_1[None], w_in_1b, lb_logits, ROW_TILE)
    y_prompt, hgrn_prompt = _c_scan_prompt(r3(qq), r3(kk), r3(lf), r3(vv), r3(sg), y0, o_norm_1[None], w_out_1b,
                                           SCAN_TILE)

    qs_s, k_s, v_s, _, _, sga_s, u_s, sgc_s, _ = _ab_in(
        x_sample.reshape(n_tok, d), norm_0[None], w_in_0b, qg, kg, gsum, n_tok)
    q3 = lambda a: a.reshape(n_seq, t_new, a.shape[-1])
    th = lambda a: a.reshape(n_seq, t_new * ATT_HEADS, ATT_HEAD_DIM)
    matt_s = _moba_sample(page_table, th(qs_s), th(k_s), th(v_s), th(sga_s.astype(F32)), cache_k, cache_v)
    y0_s = _ab_out_sample(x_sample.reshape(n_tok, d), matt_s.reshape(n_tok, ATT_WIDTH), q3(u_s), state_conv,
                          sgc_s.astype(F32), *conv_args)
    qq_s, kk_s, lf_s, vv_s, sg_s = _c_in(y0_s, norm_1[None], w_in_1b, lb_logits, n_tok)
    g8 = lambda a: a.reshape(n_tok // SUBLANES, SUBLANES, a.shape[-1])
    y_s, hgrn_sample = _c_scan_sample(g8(qq_s), g8(kk_s), g8(lf_s), g8(vv_s), g8(sg_s), g8(y0_s), state_hgrn,
                                      o_norm_1[None], w_out_1b, t_new)

    conv_prompt = r3(u_p)[:, s - hist:, :]
    conv_sample = jnp.concatenate([state_conv[:, t_new:, :], q3(u_s)], axis=1)
    return (y_prompt, y_s.reshape(n_seq, t_new, d),
            heads(k_p, (b, s)), heads(v_p, (b, s)), heads(k_s, (n_seq, t_new)), heads(v_s, (n_seq, t_new)),
            conv_prompt, conv_sample, hgrn_prompt, hgrn_sample)
```

```python
import functools

import jax
import jax.numpy as jnp
from jax import lax
from jax.experimental import pallas as pl
from jax.experimental.pallas import tpu as pltpu

EPS = 1e-6
ATT_HEADS = 8
ATT_HEAD_DIM = 64
ATT_WIDTH = ATT_HEADS * ATT_HEAD_DIM
MOBA_BLOCK = 256
MOBA_TOPK = 3
MOBA_STEP_BLOCKS = 2
MOBA_STEP_UNROLL = 2
PAGE_SIZE = 128
CONV_CH = 512
CONV_LEN = 31
HGRN_HEADS = 8
HGRN_DK = 128
HGRN_DV = 128
LAYER_C = 1

LANES = 128
HEADS_PER_VREG = LANES // ATT_HEAD_DIM
NEG = -1e30
VMEM_LIMIT = 56 * 1024 * 1024
ROW_TILE = 512
SCAN_TILE = 256

F32 = jnp.float32
BF16 = jnp.bfloat16
NT_DIMS = (((1,), (1,)), ((), ()))
TN_DIMS = (((0,), (0,)), ((), ()))


def _silu(x):
    return x * jax.nn.sigmoid(x)


def _cparams(n_axes):
    return pltpu.CompilerParams(dimension_semantics=("arbitrary",) * n_axes,
                                vmem_limit_bytes=VMEM_LIMIT)


def _ab_in_kernel(x_ref, ng_ref, w_ref, qg_ref, kg_ref, gsum_ref,
                  qs_ref, k_ref, v_ref, kb_ref, vb_ref, sga_ref, u_ref, sgc_ref, kmean_ref, *, kv_transposed):
    def put_kv(ref, val):
        if kv_transposed:
            ref[0] = val.T
        else:
            ref[...] = val

    x = x_ref[...]
    ms = jnp.mean(x * x, axis=-1, keepdims=True)
    h = (x * lax.rsqrt(ms + EPS) * ng_ref[...]).astype(BF16)

    def proj(c):
        return jnp.dot(h, w_ref[:, c * ATT_WIDTH:(c + 1) * ATT_WIDTH], preferred_element_type=F32)

    def head_rms(z, g):
        zz = (z * z).astype(BF16)
        half = ATT_WIDTH // 2
        ms_h = jnp.concatenate(
            [jnp.dot(zz[:, :half], gsum_ref[...], preferred_element_type=F32),
             jnp.dot(zz[:, half:], gsum_ref[...], preferred_element_type=F32)], axis=1)
        return z * lax.rsqrt(ms_h + EPS) * g

    q = head_rms(proj(0), qg_ref[...])
    qs_ref[...] = (q * (ATT_HEAD_DIM ** -0.5)).astype(BF16)
    k = head_rms(proj(1), kg_ref[...])
    put_kv(k_ref, k)
    kb_ref[...] = k.astype(BF16)
    tm = k.shape[0]
    kmean_ref[0] = jnp.mean(k.reshape(tm // MOBA_BLOCK, MOBA_BLOCK, ATT_WIDTH), axis=1)
    v = proj(2)
    put_kv(v_ref, v)
    vb_ref[...] = v.astype(BF16)
    sga_ref[...] = _silu(proj(3)).astype(BF16)
    u_ref[...] = proj(4) * jax.nn.sigmoid(proj(5))
    sgc_ref[...] = _silu(proj(6)).astype(BF16)


def _ab_in(x2d, ng, w_bf, qg, kg, gsum, tm, seq_len=None):
    t, d = x2d.shape
    wn = w_bf.shape[1]
    row = lambda i: (i, 0)
    const = lambda i: (0, 0)
    tile = lambda dt: jax.ShapeDtypeStruct((t, ATT_WIDTH), dt)
    tspec = pl.BlockSpec((tm, ATT_WIDTH), row)
    if seq_len is None:
        kv_shape, kv_spec = tile(F32), tspec
    else:
        per_seq = seq_len // tm
        kv_shape = jax.ShapeDtypeStruct((t // seq_len, ATT_WIDTH, seq_len), F32)
        kv_spec = pl.BlockSpec((1, ATT_WIDTH, tm), lambda i: (i // per_seq, 0, i % per_seq))
    out_shape = (tile(BF16), kv_shape, kv_shape, tile(BF16), tile(BF16), tile(BF16), tile(F32), tile(BF16),
                 jax.ShapeDtypeStruct((t // tm, tm // MOBA_BLOCK, ATT_WIDTH), F32))
    return pl.pallas_call(
        functools.partial(_ab_in_kernel, kv_transposed=seq_len is not None),
        grid=(t // tm,),
        in_specs=[pl.BlockSpec((tm, d), row), pl.BlockSpec((1, d), const), pl.BlockSpec((d, wn), const),
                  pl.BlockSpec((1, ATT_WIDTH), const), pl.BlockSpec((1, ATT_WIDTH), const),
                  pl.BlockSpec(gsum.shape, const)],
        out_specs=(tspec, kv_spec, kv_spec) + (tspec,) * 5
        + (pl.BlockSpec((1, tm // MOBA_BLOCK, ATT_WIDTH), lambda i: (i, 0, 0)),),
        out_shape=out_shape,
        compiler_params=_cparams(1),
        name="ab_in",
    )(x2d, ng, w_bf, qg, kg, gsum)


def _moba_prompt_kernel(qs_ref, kb_ref, vb_ref, kmean_ref, sga_ref, o_ref, *, nb):
    j = pl.program_id(2)
    blk = MOBA_BLOCK
    q2 = qs_ref[0]
    km = kmean_ref[0]
    lane = lax.broadcasted_iota(jnp.int32, (blk, LANES), 1)
    r_i = lax.broadcasted_iota(jnp.int32, (blk, blk), 0)
    c_i = lax.broadcasted_iota(jnp.int32, (blk, blk), 1)
    causal_bias = jnp.where(c_i <= r_i, 0.0, NEG).astype(F32)
    b_i = lax.broadcasted_iota(jnp.int32, (nb, blk), 0)
    valid = b_i < j
    kd = kb_ref[0, pl.ds(pl.multiple_of(j * blk, blk), blk), :]
    vd = vb_ref[0, pl.ds(pl.multiple_of(j * blk, blk), blk), :]

    qhs, qcats = [], []
    for hh in range(HEADS_PER_VREG):
        head_mask = jnp.where((lane // ATT_HEAD_DIM) == hh, 1.0, 0.0).astype(BF16)
        qh = q2 * head_mask
        g = lax.dot_general(km, qh.astype(F32), NT_DIMS, precision=lax.Precision.HIGHEST,
                            preferred_element_type=F32)
        g = jnp.where(valid, g, -jnp.inf)
        rank = jnp.zeros((nb, blk), F32)
        for i in range(nb):
            gi = g[i:i + 1, :]
            beats = jnp.where(gi > g, 1.0, jnp.where((gi == g) & (b_i > i), 1.0, 0.0))
            rank = rank + beats
        sel = jnp.where(valid & (rank < MOBA_TOPK), 1.0, 0.0)
        sel_t = jnp.concatenate([sel, jnp.zeros((LANES - nb, blk), F32)], axis=0).T
        bias_q = jnp.where(sel_t > 0.5, 0.0, NEG).astype(BF16)
        qhs.append(qh)
        qcats.append(jnp.concatenate([qh, bias_q], axis=1))

    q_rows = jnp.concatenate(qhs, axis=0)
    qcat_rows = jnp.concatenate(qcats, axis=0)
    s = (lax.dot_general(q_rows, kd, NT_DIMS, preferred_element_type=F32)
         + jnp.concatenate([causal_bias] * HEADS_PER_VREG, axis=0))
    m = jnp.max(s, axis=1, keepdims=True)
    p = jnp.exp(s - m)
    l = jnp.sum(p, axis=1, keepdims=True)
    acc = jnp.dot(p.astype(BF16), vd, preferred_element_type=F32)

    span = MOBA_STEP_BLOCKS * blk
    lane_s = lax.broadcasted_iota(jnp.int32, (span, LANES), 1)
    sub_s = lax.broadcasted_iota(jnp.int32, (span, LANES), 0) // blk

    def step(i, carry):
        m, l, acc = carry
        off = pl.multiple_of(i * span, span)
        kj = kb_ref[0, pl.ds(off, span), :]
        vj = vb_ref[0, pl.ds(off, span), :]
        onehot = jnp.where(lane_s == i * MOBA_STEP_BLOCKS + sub_s, 1.0, 0.0).astype(BF16)
        kcat = jnp.concatenate([kj, onehot], axis=1)
        s = lax.dot_general(qcat_rows, kcat, NT_DIMS, preferred_element_type=F32)
        m_new = jnp.maximum(m, jnp.max(s, axis=1, keepdims=True))
        alpha = jnp.exp(m - m_new)
        p = jnp.exp(s - m_new)
        l = alpha * l + jnp.sum(p, axis=1, keepdims=True)
        acc = alpha * acc + jnp.dot(p.astype(BF16), vj, preferred_element_type=F32)
        return m_new, l, acc

    def body(i, carry):
        for u in range(MOBA_STEP_UNROLL):
            carry = step(i * MOBA_STEP_UNROLL + u, carry)
        return carry

    per_iter = MOBA_STEP_BLOCKS * MOBA_STEP_UNROLL
    m, l, acc = lax.fori_loop(0, (j + per_iter - 1) // per_iter, body, (m, l, acc))
    o = acc / l
    o = jnp.where(lane < ATT_HEAD_DIM, o[:blk], o[blk:])
    o_ref[0] = (o * sga_ref[0].astype(F32)).astype(BF16)


def _moba_prompt(qs, kb, vb, kmean, sga):
    b, s, _ = qs.shape
    nb = s // MOBA_BLOCK
    assert nb % (MOBA_STEP_BLOCKS * MOBA_STEP_UNROLL) == 0
    npair = ATT_WIDTH // LANES
    qmap = lambda bi, hp, j: (bi, j, hp)
    kvmap = lambda bi, hp, j: (bi, 0, hp)
    return pl.pallas_call(
        functools.partial(_moba_prompt_kernel, nb=nb),
        grid=(b, npair, nb),
        in_specs=[pl.BlockSpec((1, MOBA_BLOCK, LANES), qmap),
                  pl.BlockSpec((1, s, LANES), kvmap), pl.BlockSpec((1, s, LANES), kvmap),
                  pl.BlockSpec((1, nb, LANES), kvmap),
                  pl.BlockSpec((1, MOBA_BLOCK, LANES), qmap)],
        out_specs=pl.BlockSpec((1, MOBA_BLOCK, LANES), qmap),
        out_shape=jax.ShapeDtypeStruct((b, s, ATT_WIDTH), BF16),
        compiler_params=_cparams(3),
        name="moba_prompt",
    )(qs, kb, vb, kmean, sga)


CONV_HALO = 32
CONV_ROWS = 64


SUBLANES = 8
CONV_LEAD = CONV_HALO - (CONV_LEN - 1)
CONV_SPAN = -(-(CONV_LEAD + SUBLANES - 1 + CONV_LEN) // SUBLANES)


def _conv_tap_table(conv_w):
    m = jnp.arange(CONV_SPAN)[:, None, None]
    s = jnp.arange(SUBLANES)[None, :, None]
    r = jnp.arange(SUBLANES)[None, None, :]
    idx = SUBLANES * m + s - r - CONV_LEAD
    ok = (idx >= 0) & (idx < CONV_LEN)
    tab = jnp.where(ok[..., None], conv_w[jnp.clip(idx, 0, CONV_LEN - 1)], 0.0)
    return tab.reshape(CONV_SPAN * SUBLANES, SUBLANES, conv_w.shape[-1])


def _tap_used(ms):
    lo = ms - (SUBLANES - 1) - CONV_LEAD
    return lo + SUBLANES - 1 >= 0 and lo < CONV_LEN


def _conv_rows(xpad_ref, ybuf_ref, tab_ref, cb_ref, n_rows):
    groups = CONV_ROWS // SUBLANES
    for lt in range(CONV_CH // LANES):
        ls = slice(lt * LANES, (lt + 1) * LANES)

        def chunk(c, carry, ls=ls):
            r0 = pl.multiple_of(c * CONV_ROWS, CONV_ROWS)
            win_ref = xpad_ref.at[pl.ds(r0, CONV_ROWS + CONV_SPAN * SUBLANES)]
            taps = {ms: tab_ref[ms, :, ls] for ms in range(CONV_SPAN * SUBLANES) if _tap_used(ms)}
            acc = [jnp.broadcast_to(cb_ref[:, ls], (SUBLANES, LANES))] * groups
            for rho in range(CONV_ROWS + CONV_SPAN * SUBLANES):
                users = [(rho // SUBLANES - m, SUBLANES * m + rho % SUBLANES) for m in range(CONV_SPAN)]
                users = [(g, ms) for g, ms in users if 0 <= g < groups and ms in taps]
                if not users:
                    continue
                xb = win_ref[pl.ds(rho, 1), ls]
                for g, ms in users:
                    acc[g] = acc[g] + taps[ms] * xb
            ybuf_ref[pl.ds(r0, CONV_ROWS), ls] = jnp.concatenate(acc, axis=0)
            return carry

        lax.fori_loop(0, n_rows // CONV_ROWS, chunk, 0)


def _ln_silu_gate(y, lg_ref, lb_ref, sgc):
    mu = jnp.mean(y, axis=-1, keepdims=True)
    cen = y - mu
    var = jnp.mean(cen * cen, axis=-1, keepdims=True)
    yn = cen * lax.rsqrt(var + EPS) * lg_ref[...] + lb_ref[...]
    return (_silu(yn) * sgc.astype(F32)).astype(BF16)


def _ab_out_prompt_kernel(x_ref, matt_ref, u_ref, halo_ref, sgc_ref, tab_ref, cb_ref, lg_ref, lb_ref, w_ref,
                          y_ref, xpad_ref, ybuf_ref, m_ref):
    t = pl.program_id(1)
    tm = u_ref.shape[1]
    halo = halo_ref[0]
    xpad_ref[0:CONV_HALO, :] = jnp.where(t == 0, jnp.zeros_like(halo), halo)
    xpad_ref[CONV_HALO:CONV_HALO + tm, :] = u_ref[0]
    m_ref[:, 0:ATT_WIDTH] = matt_ref[0]
    _conv_rows(xpad_ref, ybuf_ref, tab_ref, cb_ref, tm)

    def chunk(c, carry):
        r0 = pl.multiple_of(c * CONV_ROWS, CONV_ROWS)
        m_ref[pl.ds(r0, CONV_ROWS), ATT_WIDTH:] = _ln_silu_gate(
            ybuf_ref[pl.ds(r0, CONV_ROWS), :], lg_ref, lb_ref, sgc_ref[0, pl.ds(r0, CONV_ROWS), :])
        return carry

    lax.fori_loop(0, tm // CONV_ROWS, chunk, 0, unroll=2)
    y_ref[0] = x_ref[0] + jnp.dot(m_ref[...], w_ref[...], preferred_element_type=F32)


def _ab_out_prompt(x, matt, u, sgc, tab, cb, lg, lb, w_bf, tm):
    b, s, d = x.shape
    hpt = tm // CONV_HALO
    tmap = lambda bi, t: (bi, t, 0)
    hmap = lambda bi, t: (bi, jnp.maximum(t * hpt - 1, 0), 0)
    const = lambda bi, t: (0, 0)
    return pl.pallas_call(
        _ab_out_prompt_kernel,
        grid=(b, s // tm),
        in_specs=[pl.BlockSpec((1, tm, d), tmap), pl.BlockSpec((1, tm, ATT_WIDTH), tmap),
                  pl.BlockSpec((1, tm, CONV_CH), tmap), pl.BlockSpec((1, CONV_HALO, CONV_CH), hmap),
                  pl.BlockSpec((1, tm, CONV_CH), tmap),
                  pl.BlockSpec(tab.shape, lambda bi, t: (0, 0, 0)), pl.BlockSpec((1, CONV_CH), const),
                  pl.BlockSpec((1, CONV_CH), const), pl.BlockSpec((1, CONV_CH), const),
                  pl.BlockSpec(w_bf.shape, const)],
        out_specs=pl.BlockSpec((1, tm, d), tmap),
        out_shape=jax.ShapeDtypeStruct((b, s, d), F32),
        scratch_shapes=[pltpu.VMEM((tm + CONV_SPAN * SUBLANES, CONV_CH), F32),
                        pltpu.VMEM((tm, CONV_CH), F32),
                        pltpu.VMEM((tm, ATT_WIDTH + CONV_CH), BF16)],
        compiler_params=_cparams(2),
        name="ab_out_prompt",
    )(x, matt, u, u, sgc, tab, cb, lg, lb, w_bf)


C_COLS = 512


def _c_in_kernel(x_ref, ng_ref, w_ref, lbl_ref, qq_ref, kk_ref, lf_ref, vv_ref, sg_ref):
    x = x_ref[...]
    ms = jnp.mean(x * x, axis=-1, keepdims=True)
    h = (x * lax.rsqrt(ms + EPS) * ng_ref[...]).astype(BF16)
    p = jax.nn.softmax(lbl_ref[...], axis=0)
    lb = jnp.sum(p[0:LAYER_C + 1], axis=0, keepdims=True) - p[0:1]
    key_w = qq_ref.shape[1]
    per = key_w // C_COLS
    for c in range(per):
        cs = slice(c * C_COLS, (c + 1) * C_COLS)

        def proj(g, cs=cs):
            return jnp.dot(h, w_ref[:, g * key_w + cs.start:g * key_w + cs.stop], preferred_element_type=F32)

        qq_ref[:, cs] = _silu(proj(0)).astype(BF16)
        lbc = lb[:, cs]
        f = lbc + (1.0 - lbc) * jax.nn.sigmoid(proj(1))
        lf_ref[:, cs] = jnp.log(f)
        kk_ref[:, cs] = (1.0 - f).astype(BF16)
        vv_ref[:, cs] = proj(2).astype(BF16)
        sg_ref[:, cs] = _silu(proj(3)).astype(BF16)


def _c_in(x2d, ng, w_bf, lbl, tm):
    t, d = x2d.shape
    key_w = lbl.shape[1]
    row = lambda i: (i, 0)
    const = lambda i: (0, 0)
    tile = lambda dt: jax.ShapeDtypeStruct((t, key_w), dt)
    tspec = pl.BlockSpec((tm, key_w), row)
    return pl.pallas_call(
        _c_in_kernel,
        grid=(t // tm,),
        in_specs=[pl.BlockSpec((tm, d), row), pl.BlockSpec((1, d), const), pl.BlockSpec(w_bf.shape, const),
                  pl.BlockSpec(lbl.shape, const)],
        out_specs=(tspec,) * 5,
        out_shape=(tile(BF16), tile(BF16), tile(F32), tile(BF16), tile(BF16)),
        compiler_params=_cparams(1),
        name="c_in",
    )(x2d, ng, w_bf, lbl)


HGRN_SUB = SUBLANES


def _split3(x):
    hi = x.astype(BF16)
    r1 = x - hi.astype(F32)
    mid = r1.astype(BF16)
    lo = (r1 - mid.astype(F32)).astype(BF16)
    return hi, mid, lo


def _chunk_masks(c):
    t_i = lax.broadcasted_iota(jnp.int32, (c, c), 0)
    s_i = lax.broadcasted_iota(jnp.int32, (c, c), 1)
    same = (t_i // HGRN_SUB) == (s_i // HGRN_SUB)
    diag = [same & (t_i - s_i == d) for d in range(HGRN_SUB)]
    col_group = [(s_i // HGRN_SUB) == a for a in range(c // HGRN_SUB)]
    return diag, col_group


def _hgrn_chunk(q, k, v_b, b, masks, o_inter_fn):
    c = q.shape[0]
    n_sub = c // HGRN_SUB
    diag, col_group = masks
    b_last = b[c - 1:c, :]
    o = o_inter_fn((q * jnp.exp(b)).astype(BF16))
    kt = (k * jnp.exp(b_last - b)).astype(BF16)

    k_b = k.astype(BF16)
    stack = [q.astype(BF16)]
    for d in range(1, HGRN_SUB):
        stack.append((q * jnp.exp(jnp.minimum(b - pltpu.roll(b, d, 0), 0.0))).astype(BF16))
    zd = lax.dot_general(jnp.concatenate(stack, axis=0), k_b, NT_DIMS, preferred_element_type=F32)
    a = jnp.zeros((c, c), F32)
    for d in range(HGRN_SUB):
        a = a + jnp.where(diag[d], zd[d * c:(d + 1) * c, :], 0.0)

    if n_sub > 1:
        b3 = b.reshape(n_sub, HGRN_SUB, b.shape[1])
        b_end = jnp.broadcast_to(b3[:, HGRN_SUB - 1:HGRN_SUB, :], b3.shape).reshape(b.shape)
        ksc = (k * jnp.exp(b_end - b)).astype(BF16)
        stack = []
        for g in range(1, n_sub):
            r = g * HGRN_SUB
            stack.append((q[r:, :] * jnp.exp(b[r:, :] - b[r - 1:r, :])).astype(BF16))
        zo = lax.dot_general(jnp.concatenate(stack, axis=0), ksc, NT_DIMS, preferred_element_type=F32)
        row = 0
        for g in range(1, n_sub):
            r = g * HGRN_SUB
            part = jnp.concatenate([jnp.zeros((r, c), F32), zo[row:row + c - r, :]], axis=0)
            a = a + jnp.where(col_group[g - 1], part, 0.0)
            row += c - r

    o = o + jnp.dot(a.astype(BF16), v_b, preferred_element_type=F32)
    return o, kt, jnp.exp(b_last)


def _head_norm_gate(o, og, sg):
    ms = jnp.mean(o * o, axis=-1, keepdims=True)
    return ((o * lax.rsqrt(ms + EPS) * og) * sg.astype(F32)).astype(BF16)


HGRN_CHUNK = 64


def _c_scan_prompt_kernel(qq_ref, kk_ref, lf_ref, vv_ref, sg_ref, y0_ref, og_ref, w_ref,
                          y_ref, s_ref, st_ref, m_ref):
    t = pl.program_id(1)
    ts = qq_ref.shape[1]
    c = HGRN_CHUNK
    dk, dv = HGRN_DK, HGRN_DV

    @pl.when(t == 0)
    def _():
        st_ref[...] = jnp.zeros_like(st_ref)

    def chunk(ci, carry):
        masks = _chunk_masks(c)
        r_i = lax.broadcasted_iota(jnp.int32, (c, c), 0)
        c_i = lax.broadcasted_iota(jnp.int32, (c, c), 1)
        tri = jnp.where(c_i <= r_i, 1.0, 0.0).astype(BF16)
        r0 = pl.multiple_of(ci * c, c)
        rows = pl.ds(r0, c)
        b_all = sum(jnp.dot(tri, part, preferred_element_type=F32) for part in _split3(lf_ref[0, rows, :]))
        for h in range(HGRN_HEADS):
            ks = slice(h * dk, (h + 1) * dk)
            vs = slice(h * dv, (h + 1) * dv)
            st = st_ref[h]
            st_b = st.astype(BF16)
            o, kt, decay = _hgrn_chunk(
                qq_ref[0, rows, ks].astype(F32), kk_ref[0, rows, ks].astype(F32), vv_ref[0, rows, vs],
                b_all[:, ks], masks,
                lambda qt, st_b=st_b: lax.dot_general(qt, st_b, NT_DIMS, preferred_element_type=F32))
            st_ref[h] = st * decay + lax.dot_general(vv_ref[0, rows, vs], kt, TN_DIMS, preferred_element_type=F32)
            m_ref[rows, vs] = _head_norm_gate(o, og_ref[...], sg_ref[0, rows, vs])
        return carry

    lax.fori_loop(0, ts // c, chunk, 0)
    y_ref[0] = y0_ref[0] + jnp.dot(m_ref[...], w_ref[...], preferred_element_type=F32)

    @pl.when(t == pl.num_programs(1) - 1)
    def _():
        for h in range(HGRN_HEADS):
            s_ref[0, h] = st_ref[h].T


def _c_scan_prompt(qq, kk, lf, vv, sg, y0, og, w_bf, ts):
    b, s, d = y0.shape
    key_w, val_w = qq.shape[2], vv.shape[2]
    tmap = lambda bi, t: (bi, t, 0)
    const = lambda bi, t: (0, 0)
    return pl.pallas_call(
        _c_scan_prompt_kernel,
        grid=(b, s // ts),
        in_specs=[pl.BlockSpec((1, ts, key_w), tmap), pl.BlockSpec((1, ts, key_w), tmap),
                  pl.BlockSpec((1, ts, key_w), tmap), pl.BlockSpec((1, ts, val_w), tmap),
                  pl.BlockSpec((1, ts, val_w), tmap), pl.BlockSpec((1, ts, d), tmap),
                  pl.BlockSpec((1, HGRN_DV), const), pl.BlockSpec(w_bf.shape, const)],
        out_specs=(pl.BlockSpec((1, ts, d), tmap),
                   pl.BlockSpec((1, HGRN_HEADS, HGRN_DK, HGRN_DV), lambda bi, t: (bi, 0, 0, 0))),
        out_shape=(jax.ShapeDtypeStruct((b, s, d), F32),
                   jax.ShapeDtypeStruct((b, HGRN_HEADS, HGRN_DK, HGRN_DV), F32)),
        scratch_shapes=[pltpu.VMEM((HGRN_HEADS, HGRN_DV, HGRN_DK), F32),
                        pltpu.VMEM((ts, val_w), BF16)],
        compiler_params=_cparams(2),
        name="c_scan_prompt",
    )(qq, kk, lf, vv, sg, y0, og, w_bf)


def _moba_sample_kernel(pt_ref, qs_ref, kn_ref, vn_ref, sga_ref, ck_hbm, cv_hbm, o_ref,
                        kbuf, vbuf, sem, *, n_pages, t_new):
    n = pl.program_id(0)
    n_seq = pl.num_programs(0)
    past = n_pages * PAGE_SIZE
    nb = past // MOBA_BLOCK
    slot = lax.rem(n, 2)

    def page_copies(seq, sl):
        out = []
        for p in range(n_pages):
            page = pt_ref[seq, p]
            toks = pl.ds(p * PAGE_SIZE, PAGE_SIZE)
            out.append(pltpu.make_async_copy(ck_hbm.at[page], kbuf.at[sl, :, :, toks], sem.at[0, sl]))
            out.append(pltpu.make_async_copy(cv_hbm.at[page], vbuf.at[sl, :, :, toks], sem.at[1, sl]))
        return out

    @pl.when(n == 0)
    def _():
        for cp in page_copies(0, 0):
            cp.start()

    @pl.when(n + 1 < n_seq)
    def _():
        for cp in page_copies(n + 1, 1 - slot):
            cp.start()

    for cp in page_copies(n, slot):
        cp.wait()

    rows = qs_ref.shape[2]
    b_i = lax.broadcasted_iota(jnp.int32, (rows, LANES), 1)
    o_row = lax.broadcasted_iota(jnp.int32, (rows, rows), 0)
    o_col = lax.broadcasted_iota(jnp.int32, (rows, rows), 1)
    for h in range(ATT_HEADS):
        q_h = qs_ref[0, h]
        s = jnp.dot(q_h, kbuf[slot, h].astype(BF16), preferred_element_type=F32)
        blocks = [s[:, j * MOBA_BLOCK:(j + 1) * MOBA_BLOCK] for j in range(nb)]
        gates = [jnp.sum(blk, axis=1, keepdims=True) for blk in blocks]
        gate = jnp.full((rows, LANES), -jnp.inf, F32)
        for i in range(nb):
            gate = jnp.where(b_i == i, gates[i], gate)
        rank = jnp.zeros((rows, LANES), F32)
        for i in range(nb):
            gi = gates[i]
            rank = rank + jnp.where(gi > gate, 1.0, jnp.where((gi == gate) & (b_i > i), 1.0, 0.0))
        sel_bias = jnp.where(rank < min(MOBA_TOPK, nb), 0.0, NEG)
        s = jnp.concatenate([blocks[j] + sel_bias[:, j:j + 1] for j in range(nb)], axis=1)

        s_own = lax.dot_general(q_h, kn_ref[0, h], NT_DIMS, preferred_element_type=F32)
        s_own = jnp.where((o_col <= o_row) & (o_col < t_new), s_own, NEG)

        m = jnp.maximum(jnp.max(s, axis=1, keepdims=True), jnp.max(s_own, axis=1, keepdims=True))
        p = jnp.exp(s - m)
        p_own = jnp.exp(s_own - m)
        l = jnp.sum(p, axis=1, keepdims=True) + jnp.sum(p_own, axis=1, keepdims=True)
        o = (lax.dot_general(p.astype(BF16), vbuf[slot, h].astype(BF16), NT_DIMS, preferred_element_type=F32)
             + jnp.dot(p_own.astype(BF16), vn_ref[0, h], preferred_element_type=F32)) / l
        o_ref[0, h] = o * sga_ref[0, h]


def _moba_sample(page_table, qs, k_new, v_new, sga, cache_kt, cache_vt, t_new):
    n_seq, _, rows, _ = qs.shape
    n_pages = page_table.shape[1]
    past = n_pages * PAGE_SIZE
    assert past % MOBA_BLOCK == 0, "the sample group's past must end on a MoBA block boundary"
    seq = lambda n, pt: (n, 0, 0, 0)
    sspec = pl.BlockSpec((1, ATT_HEADS, rows, ATT_HEAD_DIM), seq)
    page_buf = pltpu.VMEM((2, ATT_HEADS, ATT_HEAD_DIM, past), F32)
    grid_spec = pltpu.PrefetchScalarGridSpec(
        num_scalar_prefetch=1,
        grid=(n_seq,),
        in_specs=[sspec, sspec, sspec, sspec,
                  pl.BlockSpec(memory_space=pl.ANY), pl.BlockSpec(memory_space=pl.ANY)],
        out_specs=sspec,
        scratch_shapes=[page_buf, page_buf, pltpu.SemaphoreType.DMA((2, 2))],
    )
    return pl.pallas_call(
        functools.partial(_moba_sample_kernel, n_pages=n_pages, t_new=t_new),
        grid_spec=grid_spec,
        out_shape=jax.ShapeDtypeStruct(qs.shape, F32),
        compiler_params=_cparams(1),
        name="moba_sample",
    )(page_table, qs, k_new, v_new, sga, cache_kt, cache_vt)


def _ab_out_sample_kernel(x_ref, matt_ref, u_ref, buf_ref, sgc_ref, tab_ref, cb_ref, lg_ref, lb_ref, w_ref,
                          y_ref, xpad_ref, m_ref):
    n_seq, t_new, _ = u_ref.shape
    hist = buf_ref.shape[1]
    per = SUBLANES // t_new
    assert per * t_new == SUBLANES and n_seq % per == 0 and hist == CONV_LEN - 1
    span = CONV_SPAN * SUBLANES
    xpad_ref[...] = jnp.zeros_like(xpad_ref)
    row8 = lax.broadcasted_iota(jnp.int32, (SUBLANES, CONV_CH), 0)

    def group(gi, carry):
        y8 = jnp.zeros((SUBLANES, CONV_CH), F32)
        for j in range(per):
            sq = gi * per + j
            off = j * t_new
            xpad_ref[j, CONV_HALO - hist + off:CONV_HALO + off, :] = buf_ref[sq]
            xpad_ref[j, CONV_HALO + off:CONV_HALO + off + t_new, :] = u_ref[sq]
            cols = []
            for lt in range(CONV_CH // LANES):
                ls = slice(lt * LANES, (lt + 1) * LANES)
                acc = jnp.broadcast_to(cb_ref[:, ls], (SUBLANES, LANES))
                for ms in range(span):
                    if _tap_used(ms):
                        acc = acc + tab_ref[ms, :, ls] * xpad_ref[j, pl.ds(ms, 1), ls]
                cols.append(acc)
            yj = jnp.concatenate(cols, axis=1)
            y8 = jnp.where((row8 >= off) & (row8 < off + t_new), yj, y8)
        r0 = pl.multiple_of(gi * SUBLANES, SUBLANES)
        rows = pl.ds(r0, SUBLANES)
        m_ref[rows, 0:ATT_WIDTH] = matt_ref[rows, :]
        m_ref[rows, ATT_WIDTH:] = _ln_silu_gate(y8, lg_ref, lb_ref, sgc_ref[rows, :]).astype(F32)
        return carry

    lax.fori_loop(0, n_seq // per, group, 0)
    y_ref[...] = x_ref[...] + jnp.dot(m_ref[...].astype(BF16), w_ref[...], preferred_element_type=F32)


def _ab_out_sample(x2d, matt2d, u3, buf, sgc2d, tab, cb, lg, lb, w_bf):
    t, d = x2d.shape
    n_seq, t_new, _ = u3.shape
    per = SUBLANES // t_new
    vm = lambda: pl.BlockSpec(memory_space=pltpu.VMEM)
    return pl.pallas_call(
        _ab_out_sample_kernel,
        in_specs=[vm() for _ in range(10)],
        out_specs=vm(),
        out_shape=jax.ShapeDtypeStruct((t, d), F32),
        scratch_shapes=[pltpu.VMEM((per, CONV_HALO + 2 * SUBLANES, CONV_CH), F32),
                        pltpu.VMEM((t, ATT_WIDTH + CONV_CH), F32)],
        compiler_params=pltpu.CompilerParams(vmem_limit_bytes=VMEM_LIMIT),
        name="ab_out_sample",
    )(x2d, matt2d, u3, buf, sgc2d, tab, cb, lg, lb, w_bf)


def _c_scan_sample_kernel(qq_ref, kk_ref, lf_ref, vv_ref, sg_ref, y0_ref, s0_ref, og_ref, w_ref,
                          y_ref, s_ref, *, t_new):
    per = SUBLANES // t_new
    c = SUBLANES
    dk, dv = HGRN_DK, HGRN_DV
    masks = _chunk_masks(c)
    r_i = lax.broadcasted_iota(jnp.int32, (c, c), 0)
    c_i = lax.broadcasted_iota(jnp.int32, (c, c), 1)
    tri = jnp.where(c_i <= r_i, 1.0, 0.0).astype(BF16)
    row = lax.broadcasted_iota(jnp.int32, (c, 1), 0)
    lf = lf_ref[0]
    m_heads = [jnp.zeros((c, dv), BF16)] * HGRN_HEADS
    for j in range(per):
        mine = (row >= j * t_new) & (row < (j + 1) * t_new)
        keep = jnp.where(mine, 1.0, 0.0)
        b_all = sum(jnp.dot(tri, part, preferred_element_type=F32) for part in _split3(lf * keep))
        for h in range(HGRN_HEADS):
            ks = slice(h * dk, (h + 1) * dk)
            vs = slice(h * dv, (h + 1) * dv)
            st = s0_ref[j, h]
            st_b = st.astype(BF16)
            v_b = (vv_ref[0, :, vs].astype(F32) * keep).astype(BF16)
            o, kt, decay = _hgrn_chunk(
                qq_ref[0, :, ks].astype(F32) * keep, kk_ref[0, :, ks].astype(F32) * keep, v_b, b_all[:, ks], masks,
                lambda qt, st_b=st_b: jnp.dot(qt, st_b, preferred_element_type=F32))
            decay_col = jnp.broadcast_to(decay, (dv, dk)).T
            s_ref[j, h] = st * decay_col + lax.dot_general(kt, v_b, TN_DIMS, preferred_element_type=F32)
            m_heads[h] = m_heads[h] + _head_norm_gate(o, og_ref[...], sg_ref[0, :, vs]).astype(F32)
    m1 = jnp.concatenate(m_heads, axis=1).astype(BF16)
    y_ref[0] = y0_ref[0] + jnp.dot(m1, w_ref[...], preferred_element_type=F32)


def _c_scan_sample(qq, kk, lf, vv, sg, y0, s0, og, w_bf, t_new):
    groups, rows, d = y0.shape
    per = SUBLANES // t_new
    key_w, val_w = qq.shape[2], vv.shape[2]
    gmap = lambda g: (g, 0, 0)
    const = lambda g: (0, 0)
    smap = lambda g: (g, 0, 0, 0)
    sspec = pl.BlockSpec((per, HGRN_HEADS, HGRN_DK, HGRN_DV), smap)
    return pl.pallas_call(
        functools.partial(_c_scan_sample_kernel, t_new=t_new),
        grid=(groups,),
        in_specs=[pl.BlockSpec((1, rows, key_w), gmap), pl.BlockSpec((1, rows, key_w), gmap),
                  pl.BlockSpec((1, rows, key_w), gmap), pl.BlockSpec((1, rows, val_w), gmap),
                  pl.BlockSpec((1, rows, val_w), gmap), pl.BlockSpec((1, rows, d), gmap), sspec,
                  pl.BlockSpec((1, HGRN_DV), const), pl.BlockSpec(w_bf.shape, const)],
        out_specs=(pl.BlockSpec((1, rows, d), gmap), sspec),
        out_shape=(jax.ShapeDtypeStruct((groups, rows, d), F32),
                   jax.ShapeDtypeStruct(s0.shape, F32)),
        compiler_params=_cparams(1),
        name="c_scan_sample",
    )(qq, kk, lf, vv, sg, y0, s0, og, w_bf)


def kernel(x_prompt, x_sample, cache_k, cache_v, state_conv, state_hgrn, page_table, norm_0, w_in_0, q_norm_0, k_norm_0, conv_w_0, conv_b_0, conv_ln_g_0, conv_ln_b_0, w_out_0, norm_1, w_in_1, lb_logits, o_norm_1, w_out_1):
    b, s, d = x_prompt.shape
    n_seq, t_new, _ = x_sample.shape
    n_tok = n_seq * t_new
    hist = CONV_LEN - 1
    gsum = jnp.kron(jnp.eye(ATT_WIDTH // 2 // ATT_HEAD_DIM, dtype=F32),
                    jnp.full((ATT_HEAD_DIM, ATT_HEAD_DIM), 1.0 / ATT_HEAD_DIM, F32)).astype(BF16)
    qg = jnp.tile(q_norm_0, ATT_HEADS)[None]
    kg = jnp.tile(k_norm_0, ATT_HEADS)[None]
    w_in_0b = w_in_0.astype(BF16)
    w_out_0b = w_out_0.astype(BF16)
    w_in_1b = w_in_1.astype(BF16)
    w_out_1b = w_out_1.astype(BF16)
    tab = _conv_tap_table(conv_w_0)
    conv_args = (tab, conv_b_0[None], conv_ln_g_0[None], conv_ln_b_0[None], w_out_0b)
    heads = lambda a, lead: a.reshape(lead + (ATT_HEADS, ATT_HEAD_DIM))

    qs, kt_p, vt_p, kb, vb, sga, u_p, sgc, kmean = _ab_in(
        x_prompt.reshape(b * s, d), norm_0[None], w_in_0b, qg, kg, gsum, ROW_TILE, seq_len=s)
    heads_t = lambda a: a.reshape(b, ATT_HEADS, ATT_HEAD_DIM, s).transpose(0, 3, 1, 2)
    r3 = lambda a: a.reshape(b, s, a.shape[-1])
    matt = _moba_prompt(r3(qs), r3(kb), r3(vb), kmean.reshape(b, s // MOBA_BLOCK, ATT_WIDTH), r3(sga))
    y0 = _ab_out_prompt(x_prompt, matt, r3(u_p), r3(sgc), *conv_args, ROW_TILE)
    qq, kk, lf, vv, sg = _c_in(y0.reshape(b * s, d), norm_1[None], w_in_1b, lb_logits, ROW_TILE)
    y_prompt, hgrn_prompt = _c_scan_prompt(r3(qq), r3(kk), r3(lf), r3(vv), r3(sg), y0, o_norm_1[None], w_out_1b,
                                           SCAN_TILE)

    qs_s, k_s, v_s, _, _, sga_s, u_s, sgc_s, _ = _ab_in(
        x_sample.reshape(n_tok, d), norm_0[None], w_in_0b, qg, kg, gsum, n_tok)
    q3 = lambda a: a.reshape(n_seq, t_new, a.shape[-1])

    def per_head(a, dt):
        a = a.reshape(n_seq, t_new, ATT_HEADS, ATT_HEAD_DIM).transpose(0, 2, 1, 3).astype(dt)
        return jnp.pad(a, ((0, 0), (0, 0), (0, SUBLANES - t_new), (0, 0)))

    kt_view = cache_k.transpose(0, 2, 3, 1)
    vt_view = cache_v.transpose(0, 2, 3, 1)
    matt_s = _moba_sample(page_table, per_head(qs_s, BF16), per_head(k_s, BF16), per_head(v_s, BF16),
                          per_head(sga_s, F32), kt_view, vt_view, t_new)
    matt_s = matt_s[:, :, :t_new, :].transpose(0, 2, 1, 3)
    y0_s = _ab_out_sample(x_sample.reshape(n_tok, d), matt_s.reshape(n_tok, ATT_WIDTH), q3(u_s), state_conv,
                          sgc_s.astype(F32), *conv_args)
    qq_s, kk_s, lf_s, vv_s, sg_s = _c_in(y0_s, norm_1[None], w_in_1b, lb_logits, n_tok)
    g8 = lambda a: a.reshape(n_tok // SUBLANES, SUBLANES, a.shape[-1])
    y_s, hgrn_sample = _c_scan_sample(g8(qq_s), g8(kk_s), g8(lf_s), g8(vv_s), g8(sg_s), g8(y0_s), state_hgrn,
                                      o_norm_1[None], w_out_1b, t_new)

    conv_prompt = r3(u_p)[:, s - hist:, :]
    conv_sample = jnp.concatenate([state_conv[:, t_new:, :], q3(u_s)], axis=1)
    return (y_prompt, y_s.reshape(n_seq, t_new, d),
            heads_t(kt_p), heads_t(vt_p), heads(k_s, (n_seq, t_new)), heads(v_s, (n_seq, t_new)),
            conv_prompt, conv_sample, hgrn_prompt, hgrn_sample)
```

```python
import functools

import jax
import jax.numpy as jnp
from jax import lax
from jax.experimental import pallas as pl
from jax.experimental.pallas import tpu as pltpu

EPS = 1e-6
ATT_HEADS = 8
ATT_HEAD_DIM = 64
ATT_WIDTH = ATT_HEADS * ATT_HEAD_DIM
MOBA_BLOCK = 256
MOBA_TOPK = 3
MOBA_PAIR = 2
PAGE_SIZE = 128
CONV_CH = 512
CONV_LEN = 31
HGRN_HEADS = 8
HGRN_DK = 128
HGRN_DV = 128
LAYER_C = 1

LANES = 128
SUBLANES = 8
LOG2_E = 1.4426950408889634
HEADS_PER_VREG = LANES // ATT_HEAD_DIM
NEG = -1e30
VMEM_LIMIT = 56 * 1024 * 1024
ROW_TILE = 512
SCAN_TILE = 256

F32 = jnp.float32
BF16 = jnp.bfloat16
NT_DIMS = (((1,), (1,)), ((), ()))
TN_DIMS = (((0,), (0,)), ((), ()))


def _silu(x):
    return x * jax.nn.sigmoid(x)


def _cparams(n_axes):
    return pltpu.CompilerParams(dimension_semantics=("arbitrary",) * n_axes,
                                vmem_limit_bytes=VMEM_LIMIT)


def _ab_in_kernel(x_ref, ng_ref, w_ref, qg_ref, kg_ref, gsum_ref,
                  qs_ref, k_ref, v_ref, kb_ref, vb_ref, sga_ref, u_ref, sgc_ref, kmean_ref, *, kv_transposed):
    x = x_ref[...]
    ms = jnp.mean(x * x, axis=-1, keepdims=True)
    h = (x * lax.rsqrt(ms + EPS) * ng_ref[...]).astype(BF16)

    def proj(c):
        return jnp.dot(h, w_ref[:, c * ATT_WIDTH:(c + 1) * ATT_WIDTH], preferred_element_type=F32)

    def head_rms(z, g):
        zz = (z * z).astype(BF16)
        half = ATT_WIDTH // 2
        ms_h = jnp.concatenate(
            [jnp.dot(zz[:, :half], gsum_ref[...], preferred_element_type=F32),
             jnp.dot(zz[:, half:], gsum_ref[...], preferred_element_type=F32)], axis=1)
        return z * lax.rsqrt(ms_h + EPS) * g

    q = head_rms(proj(0), qg_ref[...])
    qs_ref[...] = (q * (ATT_HEAD_DIM ** -0.5 * LOG2_E)).astype(BF16)
    k = head_rms(proj(1), kg_ref[...])
    if kv_transposed:
        k_ref[0] = k.T
    else:
        k_ref[...] = k
    kb_ref[...] = k.astype(BF16)
    tm = k.shape[0]
    kmean_ref[0] = jnp.mean(k.reshape(tm // MOBA_BLOCK, MOBA_BLOCK, ATT_WIDTH), axis=1)
    v = proj(2)
    if kv_transposed:
        v_t = v.T
        v_ref[0] = v_t
        vb_ref[0] = v_t.astype(BF16)
    else:
        v_ref[...] = v
        vb_ref[...] = v.astype(BF16)
    sga_ref[...] = _silu(proj(3)).astype(BF16)
    u_ref[...] = proj(4) * jax.nn.sigmoid(proj(5))
    sgc_ref[...] = _silu(proj(6)).astype(BF16)


def _ab_in(x2d, ng, w_bf, qg, kg, gsum, tm, seq_len=None):
    t, d = x2d.shape
    wn = w_bf.shape[1]
    row = lambda i: (i, 0)
    const = lambda i: (0, 0)
    tile = lambda dt: jax.ShapeDtypeStruct((t, ATT_WIDTH), dt)
    tspec = pl.BlockSpec((tm, ATT_WIDTH), row)
    if seq_len is None:
        kv_shape, kv_spec = tile, tspec
    else:
        per_seq = seq_len // tm
        kv_shape = lambda dt: jax.ShapeDtypeStruct((t // seq_len, ATT_WIDTH, seq_len), dt)
        kv_spec = pl.BlockSpec((1, ATT_WIDTH, tm), lambda i: (i // per_seq, 0, i % per_seq))
    out_shape = (tile(BF16), kv_shape(F32), kv_shape(F32), tile(BF16), kv_shape(BF16), tile(BF16), tile(F32),
                 tile(BF16), jax.ShapeDtypeStruct((t // tm, tm // MOBA_BLOCK, ATT_WIDTH), F32))
    return pl.pallas_call(
        functools.partial(_ab_in_kernel, kv_transposed=seq_len is not None),
        grid=(t // tm,),
        in_specs=[pl.BlockSpec((tm, d), row), pl.BlockSpec((1, d), const), pl.BlockSpec((d, wn), const),
                  pl.BlockSpec((1, ATT_WIDTH), const), pl.BlockSpec((1, ATT_WIDTH), const),
                  pl.BlockSpec(gsum.shape, const)],
        out_specs=(tspec, kv_spec, kv_spec, tspec, kv_spec) + (tspec,) * 3
        + (pl.BlockSpec((1, tm // MOBA_BLOCK, ATT_WIDTH), lambda i: (i, 0, 0)),),
        out_shape=out_shape,
        compiler_params=_cparams(1),
        name="ab_in",
    )(x2d, ng, w_bf, qg, kg, gsum)


def _moba_prompt_kernel(qs_ref, kb_ref, vt_ref, kmean_ref, sga_ref, o_ref, bias_ref, s_ref, *, nb):
    pair = pl.program_id(2)
    blk = MOBA_BLOCK
    cols = HEADS_PER_VREG * blk
    km_parts = _split3(kmean_ref[0])
    lane = lax.broadcasted_iota(jnp.int32, (blk, LANES), 1)
    key_i = lax.broadcasted_iota(jnp.int32, (blk, cols), 0)
    qry_i = lax.broadcasted_iota(jnp.int32, (blk, cols), 1) % blk
    causal_bias = jnp.where(key_i <= qry_i, 0.0, NEG).astype(F32)
    b_i = lax.broadcasted_iota(jnp.int32, (nb, blk), 0)

    def block_kv(jb):
        rows = pl.ds(pl.multiple_of(jb * blk, blk), blk)
        return kb_ref[0, rows, :], vt_ref[0, :, rows]

    def prepare(c):
        j = MOBA_PAIR * pair + c
        q2 = qs_ref[0, c * blk:(c + 1) * blk, :]
        valid = b_i < j
        qhs = []
        for hh in range(HEADS_PER_VREG):
            head_mask = jnp.where((lane // ATT_HEAD_DIM) == hh, 1.0, 0.0).astype(BF16)
            qh = q2 * head_mask
            g = sum(lax.dot_general(part, qh, NT_DIMS, preferred_element_type=F32) for part in km_parts)
            g = jnp.where(valid, g, -jnp.inf)
            rank = jnp.zeros((nb, blk), F32)
            for i in range(nb):
                gi = g[i:i + 1, :]
                rank = rank + jnp.where(gi > g, 1.0, jnp.where((gi == g) & (b_i > i), 1.0, 0.0))
            sel_bias = jnp.where(valid & (rank < MOBA_TOPK), 0.0, NEG)
            bias_ref[c, :, :, hh * blk:(hh + 1) * blk] = jnp.broadcast_to(
                sel_bias[:, None, :], (nb, SUBLANES, blk))
            qhs.append(qh)
        return jnp.concatenate(qhs, axis=0)

    def block_bias(c, jb):
        return jnp.broadcast_to(bias_ref[c, jb][None], (blk // SUBLANES, SUBLANES, cols)).reshape(blk, cols)

    def scores(q_cols, k_rows):
        return lax.dot_general(k_rows, q_cols, NT_DIMS, preferred_element_type=F32)

    def update(carry, s, v_cols):
        if carry is None:
            m_new = jnp.max(s, axis=0, keepdims=True)
            p = jnp.exp2(s - m_new)
            return (m_new, jnp.sum(p, axis=0, keepdims=True),
                    jnp.dot(v_cols, p.astype(BF16), preferred_element_type=F32))
        m, l, acc = carry
        m_new = jnp.maximum(m, jnp.max(s, axis=0, keepdims=True))
        alpha = jnp.exp2(m - m_new)
        p = jnp.exp2(s - m_new)
        l = alpha * l + jnp.sum(p, axis=0, keepdims=True)
        acc = alpha * acc + jnp.dot(v_cols, p.astype(BF16), preferred_element_type=F32)
        return m_new, l, acc

    span = MOBA_PAIR * blk
    last = nb // MOBA_PAIR - 1

    def step_keys(i):
        return pl.ds(pl.multiple_of(jnp.minimum(i, last) * span, span), span)

    def issue(i, buf):
        kj = kb_ref[0, step_keys(i), :]
        for c in range(MOBA_PAIR):
            s_ref[buf, c] = scores(q_cols[c], kj)

    def consume(i, buf, carry):
        vj = vt_ref[0, :, step_keys(i)]
        dead = jnp.where(i < pair, 0.0, NEG)
        out = []
        for c in range(MOBA_PAIR):
            bias = jnp.concatenate([block_bias(c, jnp.minimum(i, last) * MOBA_PAIR + u)
                                    for u in range(MOBA_PAIR)], axis=0)
            out.append(update(carry[c], s_ref[buf, c] + (bias + dead), vj))
        return tuple(out)

    def body(t, carry):
        i0 = 2 * t
        issue(i0 + 1, 1)
        carry = consume(i0, 0, carry)
        issue(i0 + 2, 0)
        return consume(i0 + 1, 1, carry)

    q_cols = [prepare(c) for c in range(MOBA_PAIR)]
    own_kv = [block_kv(MOBA_PAIR * pair + c) for c in range(MOBA_PAIR)]
    own_s = [scores(q_cols[c], own_kv[c][0]) for c in range(MOBA_PAIR)]
    cross_s = {(c, e): scores(q_cols[c], own_kv[e][0]) for c in range(1, MOBA_PAIR) for e in range(c)}
    issue(0, 0)
    state = [update(None, own_s[c] + causal_bias, own_kv[c][1]) for c in range(MOBA_PAIR)]
    for (c, e), s_ce in cross_s.items():
        state[c] = update(state[c], s_ce + block_bias(c, MOBA_PAIR * pair + e), own_kv[e][1])
    state = lax.fori_loop(0, (pair + 1) // 2, body, tuple(state))
    for c in range(MOBA_PAIR):
        _, l, acc = state[c]
        o_t = acc / l
        o_t = jnp.concatenate([o_t[hh * ATT_HEAD_DIM:(hh + 1) * ATT_HEAD_DIM, hh * blk:(hh + 1) * blk]
                               for hh in range(HEADS_PER_VREG)], axis=0)
        rows = slice(c * blk, (c + 1) * blk)
        o_ref[0, rows, :] = (o_t.T * sga_ref[0, rows, :].astype(F32)).astype(BF16)


def _moba_prompt(qs, kb, vt, kmean, sga):
    b, s, _ = qs.shape
    nb = s // MOBA_BLOCK
    assert nb % MOBA_PAIR == 0
    npair = ATT_WIDTH // LANES
    rows = MOBA_PAIR * MOBA_BLOCK
    qmap = lambda bi, hp, j: (bi, j, hp)
    kvmap = lambda bi, hp, j: (bi, 0, hp)
    return pl.pallas_call(
        functools.partial(_moba_prompt_kernel, nb=nb),
        grid=(b, npair, nb // MOBA_PAIR),
        in_specs=[pl.BlockSpec((1, rows, LANES), qmap),
                  pl.BlockSpec((1, s, LANES), kvmap),
                  pl.BlockSpec((1, LANES, s), lambda bi, hp, j: (bi, hp, 0)),
                  pl.BlockSpec((1, nb, LANES), kvmap),
                  pl.BlockSpec((1, rows, LANES), qmap)],
        out_specs=pl.BlockSpec((1, rows, LANES), qmap),
        out_shape=jax.ShapeDtypeStruct((b, s, ATT_WIDTH), BF16),
        scratch_shapes=[pltpu.VMEM((MOBA_PAIR, nb, SUBLANES, HEADS_PER_VREG * MOBA_BLOCK), F32),
                        pltpu.VMEM((2, MOBA_PAIR, rows, HEADS_PER_VREG * MOBA_BLOCK), F32)],
        compiler_params=_cparams(3),
        name="moba_prompt",
    )(qs, kb, vt, kmean, sga)


CONV_HALO = 32
CONV_ROWS = 64


CONV_LEAD = CONV_HALO - (CONV_LEN - 1)
CONV_SPAN = -(-(CONV_LEAD + SUBLANES - 1 + CONV_LEN) // SUBLANES)


def _conv_tap_table(conv_w):
    m = jnp.arange(CONV_SPAN)[:, None, None]
    s = jnp.arange(SUBLANES)[None, :, None]
    r = jnp.arange(SUBLANES)[None, None, :]
    idx = SUBLANES * m + s - r - CONV_LEAD
    ok = (idx >= 0) & (idx < CONV_LEN)
    tab = jnp.where(ok[..., None], conv_w[jnp.clip(idx, 0, CONV_LEN - 1)], 0.0)
    return tab.reshape(CONV_SPAN * SUBLANES, SUBLANES, conv_w.shape[-1])


def _tap_used(ms):
    lo = ms - (SUBLANES - 1) - CONV_LEAD
    return lo + SUBLANES - 1 >= 0 and lo < CONV_LEN


def _conv_rows(xpad_ref, ybuf_ref, tab_ref, cb_ref, n_rows):
    groups = CONV_ROWS // SUBLANES
    for lt in range(CONV_CH // LANES):
        ls = slice(lt * LANES, (lt + 1) * LANES)

        def chunk(c, carry, ls=ls):
            r0 = pl.multiple_of(c * CONV_ROWS, CONV_ROWS)
            win_ref = xpad_ref.at[pl.ds(r0, CONV_ROWS + CONV_SPAN * SUBLANES)]
            taps = {ms: tab_ref[ms, :, ls] for ms in range(CONV_SPAN * SUBLANES) if _tap_used(ms)}
            acc = [jnp.broadcast_to(cb_ref[:, ls], (SUBLANES, LANES))] * groups
            for rho in range(CONV_ROWS + CONV_SPAN * SUBLANES):
                users = [(rho // SUBLANES - m, SUBLANES * m + rho % SUBLANES) for m in range(CONV_SPAN)]
                users = [(g, ms) for g, ms in users if 0 <= g < groups and ms in taps]
                if not users:
                    continue
                xb = win_ref[pl.ds(rho, 1), ls]
                for g, ms in users:
                    acc[g] = acc[g] + taps[ms] * xb
            ybuf_ref[pl.ds(r0, CONV_ROWS), ls] = jnp.concatenate(acc, axis=0)
            return carry

        lax.fori_loop(0, n_rows // CONV_ROWS, chunk, 0)


def _ln_silu_gate(y, lg_ref, lb_ref, sgc):
    mu = jnp.mean(y, axis=-1, keepdims=True)
    cen = y - mu
    var = jnp.mean(cen * cen, axis=-1, keepdims=True)
    yn = cen * lax.rsqrt(var + EPS) * lg_ref[...] + lb_ref[...]
    return (_silu(yn) * sgc.astype(F32)).astype(BF16)


def _ab_out_prompt_kernel(x_ref, matt_ref, u_ref, halo_ref, sgc_ref, tab_ref, cb_ref, lg_ref, lb_ref, w_ref,
                          y_ref, xpad_ref, ybuf_ref, m_ref):
    t = pl.program_id(1)
    tm = u_ref.shape[1]
    halo = halo_ref[0]
    xpad_ref[0:CONV_HALO, :] = jnp.where(t == 0, jnp.zeros_like(halo), halo)
    xpad_ref[CONV_HALO:CONV_HALO + tm, :] = u_ref[0]
    m_ref[:, 0:ATT_WIDTH] = matt_ref[0]
    _conv_rows(xpad_ref, ybuf_ref, tab_ref, cb_ref, tm)

    def chunk(c, carry):
        r0 = pl.multiple_of(c * CONV_ROWS, CONV_ROWS)
        m_ref[pl.ds(r0, CONV_ROWS), ATT_WIDTH:] = _ln_silu_gate(
            ybuf_ref[pl.ds(r0, CONV_ROWS), :], lg_ref, lb_ref, sgc_ref[0, pl.ds(r0, CONV_ROWS), :])
        return carry

    lax.fori_loop(0, tm // CONV_ROWS, chunk, 0, unroll=2)
    y_ref[0] = x_ref[0] + jnp.dot(m_ref[...], w_ref[...], preferred_element_type=F32)


def _ab_out_prompt(x, matt, u, sgc, tab, cb, lg, lb, w_bf, tm):
    b, s, d = x.shape
    hpt = tm // CONV_HALO
    tmap = lambda bi, t: (bi, t, 0)
    hmap = lambda bi, t: (bi, jnp.maximum(t * hpt - 1, 0), 0)
    const = lambda bi, t: (0, 0)
    return pl.pallas_call(
        _ab_out_prompt_kernel,
        grid=(b, s // tm),
        in_specs=[pl.BlockSpec((1, tm, d), tmap), pl.BlockSpec((1, tm, ATT_WIDTH), tmap),
                  pl.BlockSpec((1, tm, CONV_CH), tmap), pl.BlockSpec((1, CONV_HALO, CONV_CH), hmap),
                  pl.BlockSpec((1, tm, CONV_CH), tmap),
                  pl.BlockSpec(tab.shape, lambda bi, t: (0, 0, 0)), pl.BlockSpec((1, CONV_CH), const),
                  pl.BlockSpec((1, CONV_CH), const), pl.BlockSpec((1, CONV_CH), const),
                  pl.BlockSpec(w_bf.shape, const)],
        out_specs=pl.BlockSpec((1, tm, d), tmap),
        out_shape=jax.ShapeDtypeStruct((b, s, d), F32),
        scratch_shapes=[pltpu.VMEM((tm + CONV_SPAN * SUBLANES, CONV_CH), F32),
                        pltpu.VMEM((tm, CONV_CH), F32),
                        pltpu.VMEM((tm, ATT_WIDTH + CONV_CH), BF16)],
        compiler_params=_cparams(2),
        name="ab_out_prompt",
    )(x, matt, u, u, sgc, tab, cb, lg, lb, w_bf)


C_COLS = 512


def _c_in_kernel(x_ref, ng_ref, w_ref, lbl_ref, qq_ref, kk_ref, lf_ref, vv_ref, sg_ref):
    x = x_ref[...]
    ms = jnp.mean(x * x, axis=-1, keepdims=True)
    h = (x * lax.rsqrt(ms + EPS) * ng_ref[...]).astype(BF16)
    p = jax.nn.softmax(lbl_ref[...], axis=0)
    lb = jnp.sum(p[0:LAYER_C + 1], axis=0, keepdims=True) - p[0:1]
    key_w = qq_ref.shape[1]
    per = key_w // C_COLS
    for c in range(per):
        cs = slice(c * C_COLS, (c + 1) * C_COLS)

        def proj(g, cs=cs):
            return jnp.dot(h, w_ref[:, g * key_w + cs.start:g * key_w + cs.stop], preferred_element_type=F32)

        qq_ref[:, cs] = _silu(proj(0)).astype(BF16)
        lbc = lb[:, cs]
        f = lbc + (1.0 - lbc) * jax.nn.sigmoid(proj(1))
        lf_ref[:, cs] = jnp.log(f)
        kk_ref[:, cs] = (1.0 - f).astype(BF16)
        vv_ref[:, cs] = proj(2).astype(BF16)
        sg_ref[:, cs] = _silu(proj(3)).astype(BF16)


def _c_in(x2d, ng, w_bf, lbl, tm):
    t, d = x2d.shape
    key_w = lbl.shape[1]
    row = lambda i: (i, 0)
    const = lambda i: (0, 0)
    tile = lambda dt: jax.ShapeDtypeStruct((t, key_w), dt)
    tspec = pl.BlockSpec((tm, key_w), row)
    return pl.pallas_call(
        _c_in_kernel,
        grid=(t // tm,),
        in_specs=[pl.BlockSpec((tm, d), row), pl.BlockSpec((1, d), const), pl.BlockSpec(w_bf.shape, const),
                  pl.BlockSpec(lbl.shape, const)],
        out_specs=(tspec,) * 5,
        out_shape=(tile(BF16), tile(BF16), tile(F32), tile(BF16), tile(BF16)),
        compiler_params=_cparams(1),
        name="c_in",
    )(x2d, ng, w_bf, lbl)


HGRN_SUB = SUBLANES


def _split3(x):
    hi = x.astype(BF16)
    r1 = x - hi.astype(F32)
    mid = r1.astype(BF16)
    lo = (r1 - mid.astype(F32)).astype(BF16)
    return hi, mid, lo


def _chunk_masks(c):
    t_i = lax.broadcasted_iota(jnp.int32, (c, c), 0)
    s_i = lax.broadcasted_iota(jnp.int32, (c, c), 1)
    same = (t_i // HGRN_SUB) == (s_i // HGRN_SUB)
    diag = [same & (t_i - s_i == d) for d in range(HGRN_SUB)]
    col_group = [(s_i // HGRN_SUB) == a for a in range(c // HGRN_SUB)]
    return diag, col_group


def _hgrn_pairs(q, k, b, f, o_inter_fn):
    c = q.shape[0]
    n_sub = c // HGRN_SUB
    b_last = b[c - 1:c, :]
    o_inter = o_inter_fn((q * jnp.exp2(b)).astype(BF16))
    kt = (k * jnp.exp2(b_last - b)).astype(BF16)

    k_b = k.astype(BF16)
    f3 = f.reshape(n_sub, HGRN_SUB, f.shape[1])
    stack = [q.astype(BF16)]
    prod = f3
    for d in range(1, HGRN_SUB):
        if d > 1:
            prod = prod * pltpu.roll(f3, d - 1, 1)
        stack.append((q * prod.reshape(f.shape)).astype(BF16))
    zd = lax.dot_general(jnp.concatenate(stack, axis=0), k_b, NT_DIMS, preferred_element_type=F32)

    zo = None
    if n_sub > 1:
        b3 = b.reshape(n_sub, HGRN_SUB, b.shape[1])
        b_end = jnp.broadcast_to(b3[:, HGRN_SUB - 1:HGRN_SUB, :], b3.shape).reshape(b.shape)
        ksc = (k * jnp.exp2(b_end - b)).astype(BF16)
        stack = []
        for g in range(1, n_sub):
            r = g * HGRN_SUB
            stack.append((q[r:, :] * jnp.exp2(b[r:, :] - b[r - 1:r, :])).astype(BF16))
        zo = lax.dot_general(jnp.concatenate(stack, axis=0), ksc, NT_DIMS, preferred_element_type=F32)
    return (o_inter, zd, zo), kt, jnp.exp2(b_last)


def _hgrn_combine(parts, v_b, masks):
    o_inter, zd, zo = parts
    c = o_inter.shape[0]
    n_sub = c // HGRN_SUB
    diag, col_group = masks
    a = jnp.zeros((c, c), F32)
    for d in range(HGRN_SUB):
        a = a + jnp.where(diag[d], zd[d * c:(d + 1) * c, :], 0.0)
    row = 0
    for g in range(1, n_sub):
        r = g * HGRN_SUB
        part = jnp.concatenate([jnp.zeros((r, c), F32), zo[row:row + c - r, :]], axis=0)
        a = a + jnp.where(col_group[g - 1], part, 0.0)
        row += c - r
    return o_inter + jnp.dot(a.astype(BF16), v_b, preferred_element_type=F32)


def _head_norm_gate(o, og, sg):
    ms = jnp.mean(o * o, axis=-1, keepdims=True)
    return ((o * lax.rsqrt(ms + EPS) * og) * sg.astype(F32)).astype(BF16)


HGRN_CHUNK = 64


def _c_scan_prompt_kernel(qq_ref, kk_ref, lf_ref, vv_ref, sg_ref, y0_ref, og_ref, w_ref,
                          y_ref, s_ref, st_ref, m_ref, zd_ref, zo_ref):
    t = pl.program_id(1)
    ts = qq_ref.shape[1]
    c = HGRN_CHUNK
    dk, dv = HGRN_DK, HGRN_DV

    @pl.when(t == 0)
    def _():
        st_ref[...] = jnp.zeros_like(st_ref)

    def chunk(ci, carry):
        masks = _chunk_masks(c)
        r_i = lax.broadcasted_iota(jnp.int32, (c, c), 0)
        c_i = lax.broadcasted_iota(jnp.int32, (c, c), 1)
        tri = jnp.where(c_i <= r_i, 1.0, 0.0).astype(BF16)
        r0 = pl.multiple_of(ci * c, c)
        rows = pl.ds(r0, c)
        lf2 = lf_ref[0, rows, :] * LOG2_E
        f_all = jnp.exp2(lf2)
        b_all = sum(jnp.dot(tri, part, preferred_element_type=F32) for part in _split3(lf2))
        parts = []
        for h in range(HGRN_HEADS):
            ks = slice(h * dk, (h + 1) * dk)
            vs = slice(h * dv, (h + 1) * dv)
            st = st_ref[h]
            st_b = st.astype(BF16)
            (o_inter, zd, zo), kt, decay = _hgrn_pairs(
                qq_ref[0, rows, ks].astype(F32), kk_ref[0, rows, ks].astype(F32), b_all[:, ks], f_all[:, ks],
                lambda qt, st_b=st_b: lax.dot_general(qt, st_b, NT_DIMS, preferred_element_type=F32))
            zd_ref[h] = zd
            zo_ref[h] = zo
            parts.append(o_inter)
            st_ref[h] = st * decay + lax.dot_general(vv_ref[0, rows, vs], kt, TN_DIMS, preferred_element_type=F32)
        for h in range(HGRN_HEADS):
            vs = slice(h * dv, (h + 1) * dv)
            o = _hgrn_combine((parts[h], zd_ref[h], zo_ref[h]), vv_ref[0, rows, vs], masks)
            m_ref[rows, vs] = _head_norm_gate(o, og_ref[...], sg_ref[0, rows, vs])
        return carry

    lax.fori_loop(0, ts // c, chunk, 0)
    y_ref[0] = y0_ref[0] + jnp.dot(m_ref[...], w_ref[...], preferred_element_type=F32)

    @pl.when(t == pl.num_programs(1) - 1)
    def _():
        for h in range(HGRN_HEADS):
            s_ref[0, h] = st_ref[h].T


def _c_scan_prompt(qq, kk, lf, vv, sg, y0, og, w_bf, ts):
    b, s, d = y0.shape
    key_w, val_w = qq.shape[2], vv.shape[2]
    n_sub = HGRN_CHUNK // HGRN_SUB
    tmap = lambda bi, t: (bi, t, 0)
    const = lambda bi, t: (0, 0)
    return pl.pallas_call(
        _c_scan_prompt_kernel,
        grid=(b, s // ts),
        in_specs=[pl.BlockSpec((1, ts, key_w), tmap), pl.BlockSpec((1, ts, key_w), tmap),
                  pl.BlockSpec((1, ts, key_w), tmap), pl.BlockSpec((1, ts, val_w), tmap),
                  pl.BlockSpec((1, ts, val_w), tmap), pl.BlockSpec((1, ts, d), tmap),
                  pl.BlockSpec((1, HGRN_DV), const), pl.BlockSpec(w_bf.shape, const)],
        out_specs=(pl.BlockSpec((1, ts, d), tmap),
                   pl.BlockSpec((1, HGRN_HEADS, HGRN_DK, HGRN_DV), lambda bi, t: (bi, 0, 0, 0))),
        out_shape=(jax.ShapeDtypeStruct((b, s, d), F32),
                   jax.ShapeDtypeStruct((b, HGRN_HEADS, HGRN_DK, HGRN_DV), F32)),
        scratch_shapes=[pltpu.VMEM((HGRN_HEADS, HGRN_DV, HGRN_DK), F32),
                        pltpu.VMEM((ts, val_w), BF16),
                        pltpu.VMEM((HGRN_HEADS, HGRN_SUB * HGRN_CHUNK, HGRN_CHUNK), F32),
                        pltpu.VMEM((HGRN_HEADS, (n_sub * (n_sub - 1) // 2) * HGRN_SUB, HGRN_CHUNK), F32)],
        compiler_params=_cparams(2),
        name="c_scan_prompt",
    )(qq, kk, lf, vv, sg, y0, og, w_bf)


def _moba_sample_kernel(pt_ref, qs_ref, kn_ref, vn_ref, sga_ref, ck_hbm, cv_hbm, o_ref,
                        kbuf, vbuf, sem, *, n_pages, t_new):
    n = pl.program_id(0)
    n_seq = pl.num_programs(0)
    past = n_pages * PAGE_SIZE
    nb = past // MOBA_BLOCK
    slot = lax.rem(n, 2)

    def page_copies(seq, sl):
        out = []
        for p in range(n_pages):
            page = pt_ref[seq, p]
            toks = pl.ds(p * PAGE_SIZE, PAGE_SIZE)
            out.append(pltpu.make_async_copy(ck_hbm.at[page], kbuf.at[sl, :, :, toks], sem.at[0, sl]))
            out.append(pltpu.make_async_copy(cv_hbm.at[page], vbuf.at[sl, :, :, toks], sem.at[1, sl]))
        return out

    @pl.when(n == 0)
    def _():
        for cp in page_copies(0, 0):
            cp.start()

    @pl.when(n + 1 < n_seq)
    def _():
        for cp in page_copies(n + 1, 1 - slot):
            cp.start()

    for cp in page_copies(n, slot):
        cp.wait()

    rows = qs_ref.shape[2]
    b_i = lax.broadcasted_iota(jnp.int32, (rows, LANES), 1)
    o_row = lax.broadcasted_iota(jnp.int32, (rows, rows), 0)
    o_col = lax.broadcasted_iota(jnp.int32, (rows, rows), 1)
    raw = []
    for h in range(ATT_HEADS):
        q_h = qs_ref[0, h]
        raw.append((jnp.dot(q_h, kbuf[slot, h].astype(BF16), preferred_element_type=F32),
                    lax.dot_general(q_h, kn_ref[0, h], NT_DIMS, preferred_element_type=F32)))
    for h in range(ATT_HEADS):
        s, s_own = raw[h]
        blocks = [s[:, j * MOBA_BLOCK:(j + 1) * MOBA_BLOCK] for j in range(nb)]
        gates = [jnp.sum(blk, axis=1, keepdims=True) for blk in blocks]
        gate = jnp.full((rows, LANES), -jnp.inf, F32)
        for i in range(nb):
            gate = jnp.where(b_i == i, gates[i], gate)
        rank = jnp.zeros((rows, LANES), F32)
        for i in range(nb):
            gi = gates[i]
            rank = rank + jnp.where(gi > gate, 1.0, jnp.where((gi == gate) & (b_i > i), 1.0, 0.0))
        sel_bias = jnp.where(rank < min(MOBA_TOPK, nb), 0.0, NEG)
        s = jnp.concatenate([blocks[j] + sel_bias[:, j:j + 1] for j in range(nb)], axis=1)

        s_own = jnp.where((o_col <= o_row) & (o_col < t_new), s_own, NEG)

        m = jnp.maximum(jnp.max(s, axis=1, keepdims=True), jnp.max(s_own, axis=1, keepdims=True))
        p = jnp.exp2(s - m)
        p_own = jnp.exp2(s_own - m)
        l = jnp.sum(p, axis=1, keepdims=True) + jnp.sum(p_own, axis=1, keepdims=True)
        o = (lax.dot_general(p.astype(BF16), vbuf[slot, h].astype(BF16), NT_DIMS, preferred_element_type=F32)
             + jnp.dot(p_own.astype(BF16), vn_ref[0, h], preferred_element_type=F32)) / l
        o_ref[0, h] = o * sga_ref[0, h]


def _moba_sample(page_table, qs, k_new, v_new, sga, cache_kt, cache_vt, t_new):
    n_seq, _, rows, _ = qs.shape
    n_pages = page_table.shape[1]
    past = n_pages * PAGE_SIZE
    assert past % MOBA_BLOCK == 0, "the sample group's past must end on a MoBA block boundary"
    seq = lambda n, pt: (n, 0, 0, 0)
    sspec = pl.BlockSpec((1, ATT_HEADS, rows, ATT_HEAD_DIM), seq)
    page_buf = pltpu.VMEM((2, ATT_HEADS, ATT_HEAD_DIM, past), F32)
    grid_spec = pltpu.PrefetchScalarGridSpec(
        num_scalar_prefetch=1,
        grid=(n_seq,),
        in_specs=[sspec, sspec, sspec, sspec,
                  pl.BlockSpec(memory_space=pl.ANY), pl.BlockSpec(memory_space=pl.ANY)],
        out_specs=sspec,
        scratch_shapes=[page_buf, page_buf, pltpu.SemaphoreType.DMA((2, 2))],
    )
    return pl.pallas_call(
        functools.partial(_moba_sample_kernel, n_pages=n_pages, t_new=t_new),
        grid_spec=grid_spec,
        out_shape=jax.ShapeDtypeStruct(qs.shape, F32),
        compiler_params=_cparams(1),
        name="moba_sample",
    )(page_table, qs, k_new, v_new, sga, cache_kt, cache_vt)


def _ab_out_sample_kernel(x_ref, matt_ref, u_ref, buf_ref, sgc_ref, tab_ref, cb_ref, lg_ref, lb_ref, w_ref,
                          y_ref, xpad_ref, m_ref):
    n_seq, t_new, _ = u_ref.shape
    hist = buf_ref.shape[1]
    per = SUBLANES // t_new
    assert per * t_new == SUBLANES and n_seq % per == 0 and hist == CONV_LEN - 1
    span = CONV_SPAN * SUBLANES
    xpad_ref[...] = jnp.zeros_like(xpad_ref)
    row8 = lax.broadcasted_iota(jnp.int32, (SUBLANES, CONV_CH), 0)

    def group(gi, carry):
        y8 = jnp.zeros((SUBLANES, CONV_CH), F32)
        for j in range(per):
            sq = gi * per + j
            off = j * t_new
            xpad_ref[j, CONV_HALO - hist + off:CONV_HALO + off, :] = buf_ref[sq]
            xpad_ref[j, CONV_HALO + off:CONV_HALO + off + t_new, :] = u_ref[sq]
            cols = []
            for lt in range(CONV_CH // LANES):
                ls = slice(lt * LANES, (lt + 1) * LANES)
                acc = jnp.broadcast_to(cb_ref[:, ls], (SUBLANES, LANES))
                for ms in range(span):
                    if _tap_used(ms):
                        acc = acc + tab_ref[ms, :, ls] * xpad_ref[j, pl.ds(ms, 1), ls]
                cols.append(acc)
            yj = jnp.concatenate(cols, axis=1)
            y8 = jnp.where((row8 >= off) & (row8 < off + t_new), yj, y8)
        r0 = pl.multiple_of(gi * SUBLANES, SUBLANES)
        rows = pl.ds(r0, SUBLANES)
        m_ref[rows, 0:ATT_WIDTH] = matt_ref[rows, :]
        m_ref[rows, ATT_WIDTH:] = _ln_silu_gate(y8, lg_ref, lb_ref, sgc_ref[rows, :]).astype(F32)
        return carry

    lax.fori_loop(0, n_seq // per, group, 0)
    y_ref[...] = x_ref[...] + jnp.dot(m_ref[...].astype(BF16), w_ref[...], preferred_element_type=F32)


def _ab_out_sample(x2d, matt2d, u3, buf, sgc2d, tab, cb, lg, lb, w_bf):
    t, d = x2d.shape
    n_seq, t_new, _ = u3.shape
    per = SUBLANES // t_new
    vm = lambda: pl.BlockSpec(memory_space=pltpu.VMEM)
    return pl.pallas_call(
        _ab_out_sample_kernel,
        in_specs=[vm() for _ in range(10)],
        out_specs=vm(),
        out_shape=jax.ShapeDtypeStruct((t, d), F32),
        scratch_shapes=[pltpu.VMEM((per, CONV_HALO + 2 * SUBLANES, CONV_CH), F32),
                        pltpu.VMEM((t, ATT_WIDTH + CONV_CH), F32)],
        compiler_params=pltpu.CompilerParams(vmem_limit_bytes=VMEM_LIMIT),
        name="ab_out_sample",
    )(x2d, matt2d, u3, buf, sgc2d, tab, cb, lg, lb, w_bf)


def _c_scan_sample_kernel(qq_ref, kk_ref, lf_ref, vv_ref, sg_ref, y0_ref, s0_ref, og_ref, w_ref,
                          y_ref, s_ref, *, t_new):
    per = SUBLANES // t_new
    c = SUBLANES
    dk, dv = HGRN_DK, HGRN_DV
    masks = _chunk_masks(c)
    r_i = lax.broadcasted_iota(jnp.int32, (c, c), 0)
    c_i = lax.broadcasted_iota(jnp.int32, (c, c), 1)
    tri = jnp.where(c_i <= r_i, 1.0, 0.0).astype(BF16)
    row = lax.broadcasted_iota(jnp.int32, (c, 1), 0)
    lf = lf_ref[0]
    m_heads = [jnp.zeros((c, dv), F32)] * HGRN_HEADS
    pending = []
    for j in range(per):
        mine = (row >= j * t_new) & (row < (j + 1) * t_new)
        keep = jnp.where(mine, 1.0, 0.0)
        lf2 = lf * (keep * LOG2_E)
        f_all = jnp.exp2(lf2)
        b_all = sum(jnp.dot(tri, part, preferred_element_type=F32) for part in _split3(lf2))
        for h in range(HGRN_HEADS):
            ks = slice(h * dk, (h + 1) * dk)
            vs = slice(h * dv, (h + 1) * dv)
            st = s0_ref[j, h]
            st_b = st.astype(BF16)
            v_b = (vv_ref[0, :, vs].astype(F32) * keep).astype(BF16)
            parts, kt, decay = _hgrn_pairs(
                qq_ref[0, :, ks].astype(F32) * keep, kk_ref[0, :, ks].astype(F32) * keep,
                b_all[:, ks], f_all[:, ks],
                lambda qt, st_b=st_b: jnp.dot(qt, st_b, preferred_element_type=F32))
            decay_col = jnp.broadcast_to(decay, (dv, dk)).T
            s_ref[j, h] = st * decay_col + lax.dot_general(kt, v_b, TN_DIMS, preferred_element_type=F32)
            pending.append((h, parts, v_b))
    for h, parts, v_b in pending:
        vs = slice(h * dv, (h + 1) * dv)
        o = _hgrn_combine(parts, v_b, masks)
        m_heads[h] = m_heads[h] + _head_norm_gate(o, og_ref[...], sg_ref[0, :, vs]).astype(F32)
    m1 = jnp.concatenate(m_heads, axis=1).astype(BF16)
    y_ref[0] = y0_ref[0] + jnp.dot(m1, w_ref[...], preferred_element_type=F32)


def _c_scan_sample(qq, kk, lf, vv, sg, y0, s0, og, w_bf, t_new):
    groups, rows, d = y0.shape
    per = SUBLANES // t_new
    key_w, val_w = qq.shape[2], vv.shape[2]
    gmap = lambda g: (g, 0, 0)
    const = lambda g: (0, 0)
    smap = lambda g: (g, 0, 0, 0)
    sspec = pl.BlockSpec((per, HGRN_HEADS, HGRN_DK, HGRN_DV), smap)
    return pl.pallas_call(
        functools.partial(_c_scan_sample_kernel, t_new=t_new),
        grid=(groups,),
        in_specs=[pl.BlockSpec((1, rows, key_w), gmap), pl.BlockSpec((1, rows, key_w), gmap),
                  pl.BlockSpec((1, rows, key_w), gmap), pl.BlockSpec((1, rows, val_w), gmap),
                  pl.BlockSpec((1, rows, val_w), gmap), pl.BlockSpec((1, rows, d), gmap), sspec,
                  pl.BlockSpec((1, HGRN_DV), const), pl.BlockSpec(w_bf.shape, const)],
        out_specs=(pl.BlockSpec((1, rows, d), gmap), sspec),
        out_shape=(jax.ShapeDtypeStruct((groups, rows, d), F32),
                   jax.ShapeDtypeStruct(s0.shape, F32)),
        compiler_params=_cparams(1),
        name="c_scan_sample",
    )(qq, kk, lf, vv, sg, y0, s0, og, w_bf)


def kernel(x_prompt, x_sample, cache_k, cache_v, state_conv, state_hgrn, page_table, norm_0, w_in_0, q_norm_0, k_norm_0, conv_w_0, conv_b_0, conv_ln_g_0, conv_ln_b_0, w_out_0, norm_1, w_in_1, lb_logits, o_norm_1, w_out_1):
    b, s, d = x_prompt.shape
    n_seq, t_new, _ = x_sample.shape
    n_tok = n_seq * t_new
    hist = CONV_LEN - 1
    gsum = jnp.kron(jnp.eye(ATT_WIDTH // 2 // ATT_HEAD_DIM, dtype=F32),
                    jnp.full((ATT_HEAD_DIM, ATT_HEAD_DIM), 1.0 / ATT_HEAD_DIM, F32)).astype(BF16)
    qg = jnp.tile(q_norm_0, ATT_HEADS)[None]
    kg = jnp.tile(k_norm_0, ATT_HEADS)[None]
    w_in_0b = w_in_0.astype(BF16)
    w_out_0b = w_out_0.astype(BF16)
    w_in_1b = w_in_1.astype(BF16)
    w_out_1b = w_out_1.astype(BF16)
    tab = _conv_tap_table(conv_w_0)
    conv_args = (tab, conv_b_0[None], conv_ln_g_0[None], conv_ln_b_0[None], w_out_0b)
    heads = lambda a, lead: a.reshape(lead + (ATT_HEADS, ATT_HEAD_DIM))

    qs, kt_p, vt_p, kb, vbt, sga, u_p, sgc, kmean = _ab_in(
        x_prompt.reshape(b * s, d), norm_0[None], w_in_0b, qg, kg, gsum, ROW_TILE, seq_len=s)
    heads_t = lambda a: a.reshape(b, ATT_HEADS, ATT_HEAD_DIM, s).transpose(0, 3, 1, 2)
    r3 = lambda a: a.reshape(b, s, a.shape[-1])
    matt = _moba_prompt(r3(qs), r3(kb), vbt, kmean.reshape(b, s // MOBA_BLOCK, ATT_WIDTH), r3(sga))
    y0 = _ab_out_prompt(x_prompt, matt, r3(u_p), r3(sgc), *conv_args, ROW_TILE)
    qq, kk, lf, vv, sg = _c_in(y0.reshape(b * s, d), norm_1[None], w_in_1b, lb_logits, ROW_TILE)
    y_prompt, hgrn_prompt = _c_scan_prompt(r3(qq), r3(kk), r3(lf), r3(vv), r3(sg), y0, o_norm_1[None], w_out_1b,
                                           SCAN_TILE)

    qs_s, k_s, v_s, _, _, sga_s, u_s, sgc_s, _ = _ab_in(
        x_sample.reshape(n_tok, d), norm_0[None], w_in_0b, qg, kg, gsum, n_tok)
    q3 = lambda a: a.reshape(n_seq, t_new, a.shape[-1])

    def per_head(a, dt):
        a = a.reshape(n_seq, t_new, ATT_HEADS, ATT_HEAD_DIM).transpose(0, 2, 1, 3).astype(dt)
        return jnp.pad(a, ((0, 0), (0, 0), (0, SUBLANES - t_new), (0, 0)))

    kt_view = cache_k.transpose(0, 2, 3, 1)
    vt_view = cache_v.transpose(0, 2, 3, 1)
    matt_s = _moba_sample(page_table, per_head(qs_s, BF16), per_head(k_s, BF16), per_head(v_s, BF16),
                          per_head(sga_s, F32), kt_view, vt_view, t_new)
    matt_s = matt_s[:, :, :t_new, :].transpose(0, 2, 1, 3)
    y0_s = _ab_out_sample(x_sample.reshape(n_tok, d), matt_s.reshape(n_tok, ATT_WIDTH), q3(u_s), state_conv,
                          sgc_s.astype(F32), *conv_args)
    qq_s, kk_s, lf_s, vv_s, sg_s = _c_in(y0_s, norm_1[None], w_in_1b, lb_logits, n_tok)
    g8 = lambda a: a.reshape(n_tok // SUBLANES, SUBLANES, a.shape[-1])
    y_s, hgrn_sample = _c_scan_sample(g8(qq_s), g8(kk_s), g8(lf_s), g8(vv_s), g8(sg_s), g8(y0_s), state_hgrn,
                                      o_norm_1[None], w_out_1b, t_new)

    conv_prompt = r3(u_p)[:, s - hist:, :]
    conv_sample = jnp.concatenate([state_conv[:, t_new:, :], q3(u_s)], axis=1)
    return (y_prompt, y_s.reshape(n_seq, t_new, d),
            heads_t(kt_p), heads_t(vt_p), heads(k_s, (n_seq, t_new)), heads(v_s, (n_seq, t_new)),
            conv_prompt, conv_sample, hgrn_prompt, hgrn_sample)
```

```python
import functools

import jax
import jax.numpy as jnp
from jax import lax
from jax.experimental import pallas as pl
from jax.experimental.pallas import tpu as pltpu

EPS = 1e-6
ATT_HEADS = 8
ATT_HEAD_DIM = 64
ATT_WIDTH = ATT_HEADS * ATT_HEAD_DIM
MOBA_BLOCK = 256
MOBA_TOPK = 3
MOBA_PAIR = 2
PAGE_SIZE = 128
CONV_CH = 512
CONV_LEN = 31
HGRN_HEADS = 8
HGRN_DK = 128
HGRN_DV = 128
LAYER_C = 1

LANES = 128
SUBLANES = 8
LOG2_E = 1.4426950408889634
HEADS_PER_VREG = LANES // ATT_HEAD_DIM
NEG = -1e30
VMEM_LIMIT = 56 * 1024 * 1024
ROW_TILE = 512
SCAN_TILE = 256

F32 = jnp.float32
BF16 = jnp.bfloat16
NT_DIMS = (((1,), (1,)), ((), ()))
TN_DIMS = (((0,), (0,)), ((), ()))


def _silu(x):
    return x * jax.nn.sigmoid(x)


def _cparams(n_axes):
    return pltpu.CompilerParams(dimension_semantics=("arbitrary",) * n_axes,
                                vmem_limit_bytes=VMEM_LIMIT)


def _ab_in_stages(h_ref, w_ref, qg_ref, kg_ref, gsum_ref,
                  qs_ref, k_ref, v_ref, kb_ref, vb_ref, sga_ref, u_ref, sgc_ref, kmean_ref, kv_transposed):
    def proj(c):
        return jnp.dot(h_ref[...], w_ref[:, c * ATT_WIDTH:(c + 1) * ATT_WIDTH], preferred_element_type=F32)

    def head_rms(z, g):
        zz = (z * z).astype(BF16)
        half = ATT_WIDTH // 2
        ms_h = jnp.concatenate(
            [jnp.dot(zz[:, :half], gsum_ref[...], preferred_element_type=F32),
             jnp.dot(zz[:, half:], gsum_ref[...], preferred_element_type=F32)], axis=1)
        return z * lax.rsqrt(ms_h + EPS) * g

    def stage_q():
        q = head_rms(proj(0), qg_ref[...])
        qs_ref[...] = (q * (ATT_HEAD_DIM ** -0.5 * LOG2_E)).astype(BF16)

    def stage_k():
        k = head_rms(proj(1), kg_ref[...])
        if kv_transposed:
            k_ref[0] = k.T
        else:
            k_ref[...] = k
        kb_ref[...] = k.astype(BF16)
        tm = k.shape[0]
        kmean_ref[0] = jnp.mean(k.reshape(tm // MOBA_BLOCK, MOBA_BLOCK, ATT_WIDTH), axis=1)

    def stage_v():
        v = proj(2)
        if kv_transposed:
            v_t = v.T
            v_ref[0] = v_t
            vb_ref[0] = v_t.astype(BF16)
        else:
            v_ref[...] = v
            vb_ref[...] = v.astype(BF16)

    def stage_gate_att():
        sga_ref[...] = _silu(proj(3)).astype(BF16)

    def stage_glu():
        u_ref[...] = proj(4) * jax.nn.sigmoid(proj(5))

    def stage_gate_conv():
        sgc_ref[...] = _silu(proj(6)).astype(BF16)

    return [stage_q, stage_k, stage_v, stage_gate_att, stage_glu, stage_gate_conv]


def _normalise_rows(x_ref, ng_ref, h_ref):
    x = x_ref[...]
    ms = jnp.mean(x * x, axis=-1, keepdims=True)
    h_ref[...] = (x * lax.rsqrt(ms + EPS) * ng_ref[...]).astype(BF16)


def _ab_in_kernel(x_ref, ng_ref, w_ref, qg_ref, kg_ref, gsum_ref, *rest, kv_transposed):
    *outs, h_ref = rest
    _normalise_rows(x_ref, ng_ref, h_ref)
    for stage in _ab_in_stages(h_ref, w_ref, qg_ref, kg_ref, gsum_ref, *outs, kv_transposed):
        stage()


def _ab_in(x2d, ng, w_bf, qg, kg, gsum, tm, seq_len=None, sample=None):
    t, d = x2d.shape
    wn = w_bf.shape[1]
    steps = t // tm
    row = lambda i, *_: (i, 0)
    const = lambda i, *_: (0, 0)
    tile = lambda dt: jax.ShapeDtypeStruct((t, ATT_WIDTH), dt)
    tspec = pl.BlockSpec((tm, ATT_WIDTH), row)
    if seq_len is None:
        kv_shape, kv_spec = tile, tspec
    else:
        per_seq = seq_len // tm
        kv_shape = lambda dt: jax.ShapeDtypeStruct((t // seq_len, ATT_WIDTH, seq_len), dt)
        kv_spec = pl.BlockSpec((1, ATT_WIDTH, tm), lambda i, *_: (i // per_seq, 0, i % per_seq))
    out_shape = (tile(BF16), kv_shape(F32), kv_shape(F32), tile(BF16), kv_shape(BF16), tile(BF16), tile(F32),
                 tile(BF16), jax.ShapeDtypeStruct((steps, tm // MOBA_BLOCK, ATT_WIDTH), F32))
    in_specs = [pl.BlockSpec((tm, d), row), pl.BlockSpec((1, d), const), pl.BlockSpec((d, wn), const),
                pl.BlockSpec((1, ATT_WIDTH), const), pl.BlockSpec((1, ATT_WIDTH), const),
                pl.BlockSpec(gsum.shape, const)]
    out_specs = ((tspec, kv_spec, kv_spec, tspec, kv_spec) + (tspec,) * 3
                 + (pl.BlockSpec((1, tm // MOBA_BLOCK, ATT_WIDTH), lambda i, *_: (i, 0, 0)),))
    scratch = [pltpu.VMEM((tm, d), BF16)]
    kv_t = seq_len is not None
    if sample is None:
        return pl.pallas_call(
            functools.partial(_ab_in_kernel, kv_transposed=kv_t),
            grid=(steps,), in_specs=in_specs, out_specs=out_specs, out_shape=out_shape,
            scratch_shapes=scratch, compiler_params=_cparams(1), name="ab_in",
        )(x2d, ng, w_bf, qg, kg, gsum)

    page_table, sq, skn, svn, ssga, cache_kt, cache_vt, t_new = sample
    n_seq, _, rows, _ = sq.shape
    n_pages = page_table.shape[1]
    past = n_pages * PAGE_SIZE
    per_step = n_seq // steps
    assert past % MOBA_BLOCK == 0, "the sample group's past must end on a MoBA block boundary"
    assert per_step * steps == n_seq and per_step % 2 == 0
    sspec = pl.BlockSpec((per_step, ATT_HEADS, rows, ATT_HEAD_DIM), lambda i, *_: (i, 0, 0, 0))
    page_buf = pltpu.VMEM((2, ATT_HEADS, ATT_HEAD_DIM, past), F32)
    grid_spec = pltpu.PrefetchScalarGridSpec(
        num_scalar_prefetch=1,
        grid=(steps,),
        in_specs=in_specs + [sspec] * 4 + [pl.BlockSpec(memory_space=pl.ANY)] * 2,
        out_specs=out_specs + (sspec,),
        scratch_shapes=scratch + [page_buf, page_buf, pltpu.SemaphoreType.DMA((2, 2))],
    )
    return pl.pallas_call(
        functools.partial(_ab_in_sample_kernel, kv_transposed=kv_t, n_pages=n_pages, t_new=t_new,
                          per_step=per_step),
        grid_spec=grid_spec,
        out_shape=out_shape + (jax.ShapeDtypeStruct(sq.shape, F32),),
        compiler_params=_cparams(1),
        name="ab_in_sample",
    )(page_table, x2d, ng, w_bf, qg, kg, gsum, sq, skn, svn, ssga, cache_kt, cache_vt)


def _moba_prompt_kernel(qs_ref, kb_ref, vt_ref, kmean_ref, sga_ref, o_ref, bias_ref, s_ref, e_ref, *, nb):
    pair = pl.program_id(2)
    blk = MOBA_BLOCK
    cols = HEADS_PER_VREG * blk
    km_parts = _split3(kmean_ref[0])
    lane = lax.broadcasted_iota(jnp.int32, (blk, LANES), 1)
    key_i = lax.broadcasted_iota(jnp.int32, (blk, cols), 0)
    qry_i = lax.broadcasted_iota(jnp.int32, (blk, cols), 1) % blk
    causal_bias = jnp.where(key_i <= qry_i, 0.0, NEG).astype(F32)
    b_i = lax.broadcasted_iota(jnp.int32, (nb, blk), 0)

    def block_kv(jb):
        rows = pl.ds(pl.multiple_of(jb * blk, blk), blk)
        return kb_ref[0, rows, :], vt_ref[0, :, rows]

    def prepare(c):
        j = MOBA_PAIR * pair + c
        q2 = qs_ref[0, c * blk:(c + 1) * blk, :]
        valid = b_i < j
        qhs = []
        for hh in range(HEADS_PER_VREG):
            head_mask = jnp.where((lane // ATT_HEAD_DIM) == hh, 1.0, 0.0).astype(BF16)
            qh = q2 * head_mask
            g = sum(lax.dot_general(part, qh, NT_DIMS, preferred_element_type=F32) for part in km_parts)
            g = jnp.where(valid, g, -jnp.inf)
            rank = jnp.zeros((nb, blk), F32)
            for i in range(nb):
                gi = g[i:i + 1, :]
                rank = rank + jnp.where(gi > g, 1.0, jnp.where((gi == g) & (b_i > i), 1.0, 0.0))
            sel_bias = jnp.where(valid & (rank < MOBA_TOPK), 0.0, NEG)
            bias_ref[c, :, :, hh * blk:(hh + 1) * blk] = jnp.broadcast_to(
                sel_bias[:, None, :], (nb, SUBLANES, blk))
            qhs.append(qh)
        return jnp.concatenate(qhs, axis=0)

    def block_bias(c, jb):
        return jnp.broadcast_to(bias_ref[c, jb][None], (blk // SUBLANES, SUBLANES, cols)).reshape(blk, cols)

    def scores(q_cols, k_rows):
        return lax.dot_general(k_rows, q_cols, NT_DIMS, preferred_element_type=F32)

    def update(carry, s, v_cols):
        if carry is None:
            m_new = jnp.max(s, axis=0, keepdims=True)
            p = jnp.exp2(s - m_new)
            return (m_new, jnp.sum(p, axis=0, keepdims=True),
                    jnp.dot(v_cols, p.astype(BF16), preferred_element_type=F32))
        m, l, acc = carry
        m_new = jnp.maximum(m, jnp.max(s, axis=0, keepdims=True))
        alpha = jnp.exp2(m - m_new)
        p = jnp.exp2(s - m_new)
        l = alpha * l + jnp.sum(p, axis=0, keepdims=True)
        acc = alpha * acc + jnp.dot(v_cols, p.astype(BF16), preferred_element_type=F32)
        return m_new, l, acc

    span = MOBA_PAIR * blk
    last = nb // MOBA_PAIR - 1

    def step_keys(i):
        return pl.ds(pl.multiple_of(jnp.minimum(i, last) * span, span), span)

    def issue(i, buf):
        kj = kb_ref[0, step_keys(i), :]
        for c in range(MOBA_PAIR):
            s_ref[buf, c] = scores(q_cols[c], kj)

    def consume(i, buf, carry):
        vj = vt_ref[0, :, step_keys(i)]
        dead = jnp.where(i < pair, 0.0, NEG)
        out = []
        for c in range(MOBA_PAIR):
            bias = jnp.concatenate([block_bias(c, jnp.minimum(i, last) * MOBA_PAIR + u)
                                    for u in range(MOBA_PAIR)], axis=0)
            out.append(update(carry[c], s_ref[buf, c] + (bias + dead), vj))
        return tuple(out)

    def body(t, carry):
        i0 = 2 * t
        issue(i0 + 1, 1)
        carry = consume(i0, 0, carry)
        issue(i0 + 2, 0)
        return consume(i0 + 1, 1, carry)

    q_cols = [prepare(c) for c in range(MOBA_PAIR)]
    own_kv = [block_kv(MOBA_PAIR * pair + c) for c in range(MOBA_PAIR)]
    cross = [(c, e) for c in range(1, MOBA_PAIR) for e in range(c)]
    for c in range(MOBA_PAIR):
        e_ref[c] = scores(q_cols[c], own_kv[c][0])
    for n, (c, e) in enumerate(cross):
        e_ref[MOBA_PAIR + n] = scores(q_cols[c], own_kv[e][0])
    issue(0, 0)
    state = [update(None, e_ref[c] + causal_bias, own_kv[c][1]) for c in range(MOBA_PAIR)]
    for n, (c, e) in enumerate(cross):
        state[c] = update(state[c], e_ref[MOBA_PAIR + n] + block_bias(c, MOBA_PAIR * pair + e), own_kv[e][1])
    state = lax.fori_loop(0, (pair + 1) // 2, body, tuple(state))
    for c in range(MOBA_PAIR):
        _, l, acc = state[c]
        o_t = acc / l
        o_t = jnp.concatenate([o_t[hh * ATT_HEAD_DIM:(hh + 1) * ATT_HEAD_DIM, hh * blk:(hh + 1) * blk]
                               for hh in range(HEADS_PER_VREG)], axis=0)
        rows = slice(c * blk, (c + 1) * blk)
        o_ref[0, rows, :] = (o_t.T * sga_ref[0, rows, :].astype(F32)).astype(BF16)


def _moba_prompt(qs, kb, vt, kmean, sga):
    b, s, _ = qs.shape
    nb = s // MOBA_BLOCK
    assert nb % MOBA_PAIR == 0
    npair = ATT_WIDTH // LANES
    rows = MOBA_PAIR * MOBA_BLOCK
    qmap = lambda bi, hp, j: (bi, j, hp)
    kvmap = lambda bi, hp, j: (bi, 0, hp)
    return pl.pallas_call(
        functools.partial(_moba_prompt_kernel, nb=nb),
        grid=(b, npair, nb // MOBA_PAIR),
        in_specs=[pl.BlockSpec((1, rows, LANES), qmap),
                  pl.BlockSpec((1, s, LANES), kvmap),
                  pl.BlockSpec((1, LANES, s), lambda bi, hp, j: (bi, hp, 0)),
                  pl.BlockSpec((1, nb, LANES), kvmap),
                  pl.BlockSpec((1, rows, LANES), qmap)],
        out_specs=pl.BlockSpec((1, rows, LANES), qmap),
        out_shape=jax.ShapeDtypeStruct((b, s, ATT_WIDTH), BF16),
        scratch_shapes=[pltpu.VMEM((MOBA_PAIR, nb, SUBLANES, HEADS_PER_VREG * MOBA_BLOCK), F32),
                        pltpu.VMEM((2, MOBA_PAIR, rows, HEADS_PER_VREG * MOBA_BLOCK), F32),
                        pltpu.VMEM((MOBA_PAIR * (MOBA_PAIR + 1) // 2, MOBA_BLOCK, HEADS_PER_VREG * MOBA_BLOCK), F32)],
        compiler_params=_cparams(3),
        name="moba_prompt",
    )(qs, kb, vt, kmean, sga)


CONV_HALO = 32
CONV_ROWS = 64


CONV_LEAD = CONV_HALO - (CONV_LEN - 1)
CONV_SPAN = -(-(CONV_LEAD + SUBLANES - 1 + CONV_LEN) // SUBLANES)


def _conv_tap_table(conv_w):
    m = jnp.arange(CONV_SPAN)[:, None, None]
    s = jnp.arange(SUBLANES)[None, :, None]
    r = jnp.arange(SUBLANES)[None, None, :]
    idx = SUBLANES * m + s - r - CONV_LEAD
    ok = (idx >= 0) & (idx < CONV_LEN)
    tab = jnp.where(ok[..., None], conv_w[jnp.clip(idx, 0, CONV_LEN - 1)], 0.0)
    return tab.reshape(CONV_SPAN * SUBLANES, SUBLANES, conv_w.shape[-1])


def _tap_used(ms):
    lo = ms - (SUBLANES - 1) - CONV_LEAD
    return lo + SUBLANES - 1 >= 0 and lo < CONV_LEN


def _conv_rows(xpad_ref, ybuf_ref, tab_ref, cb_ref, n_rows):
    groups = CONV_ROWS // SUBLANES
    for lt in range(CONV_CH // LANES):
        ls = slice(lt * LANES, (lt + 1) * LANES)

        for c in range(n_rows // CONV_ROWS):
            r0 = c * CONV_ROWS
            win_ref = xpad_ref.at[pl.ds(r0, CONV_ROWS + CONV_SPAN * SUBLANES)]
            taps = {ms: tab_ref[ms, :, ls] for ms in range(CONV_SPAN * SUBLANES) if _tap_used(ms)}
            acc = [jnp.broadcast_to(cb_ref[:, ls], (SUBLANES, LANES))] * groups
            for rho in range(CONV_ROWS + CONV_SPAN * SUBLANES):
                users = [(rho // SUBLANES - m, SUBLANES * m + rho % SUBLANES) for m in range(CONV_SPAN)]
                users = [(g, ms) for g, ms in users if 0 <= g < groups and ms in taps]
                if not users:
                    continue
                xb = win_ref[pl.ds(rho, 1), ls]
                for g, ms in users:
                    acc[g] = acc[g] + taps[ms] * xb
            ybuf_ref[pl.ds(r0, CONV_ROWS), ls] = jnp.concatenate(acc, axis=0)


def _ln_silu_gate(y, lg_ref, lb_ref, sgc):
    mu = jnp.mean(y, axis=-1, keepdims=True)
    cen = y - mu
    var = jnp.mean(cen * cen, axis=-1, keepdims=True)
    yn = cen * lax.rsqrt(var + EPS) * lg_ref[...] + lb_ref[...]
    return (_silu(yn) * sgc.astype(F32)).astype(BF16)


def _ab_out_prompt_kernel(x_ref, matt_ref, u_ref, halo_ref, sgc_ref, tab_ref, cb_ref, lg_ref, lb_ref, w_ref,
                          y_ref, xpad_ref, ybuf_ref, m_ref):
    t = pl.program_id(1)
    tm = u_ref.shape[1]
    halo = halo_ref[0]
    xpad_ref[0:CONV_HALO, :] = jnp.where(t == 0, jnp.zeros_like(halo), halo)
    xpad_ref[CONV_HALO:CONV_HALO + tm, :] = u_ref[0]
    y_att = x_ref[0] + jnp.dot(matt_ref[0], w_ref[0:ATT_WIDTH, :], preferred_element_type=F32)
    _conv_rows(xpad_ref, ybuf_ref, tab_ref, cb_ref, tm)
    for c in range(tm // CONV_ROWS):
        rows = slice(c * CONV_ROWS, (c + 1) * CONV_ROWS)
        m_ref[rows, :] = _ln_silu_gate(ybuf_ref[rows, :], lg_ref, lb_ref, sgc_ref[0, rows, :])
    y_ref[0] = y_att + jnp.dot(m_ref[...], w_ref[ATT_WIDTH:, :], preferred_element_type=F32)


def _ab_out_prompt(x, matt, u, sgc, tab, cb, lg, lb, w_bf, tm):
    b, s, d = x.shape
    hpt = tm // CONV_HALO
    tmap = lambda bi, t: (bi, t, 0)
    hmap = lambda bi, t: (bi, jnp.maximum(t * hpt - 1, 0), 0)
    const = lambda bi, t: (0, 0)
    return pl.pallas_call(
        _ab_out_prompt_kernel,
        grid=(b, s // tm),
        in_specs=[pl.BlockSpec((1, tm, d), tmap), pl.BlockSpec((1, tm, ATT_WIDTH), tmap),
                  pl.BlockSpec((1, tm, CONV_CH), tmap), pl.BlockSpec((1, CONV_HALO, CONV_CH), hmap),
                  pl.BlockSpec((1, tm, CONV_CH), tmap),
                  pl.BlockSpec(tab.shape, lambda bi, t: (0, 0, 0)), pl.BlockSpec((1, CONV_CH), const),
                  pl.BlockSpec((1, CONV_CH), const), pl.BlockSpec((1, CONV_CH), const),
                  pl.BlockSpec(w_bf.shape, const)],
        out_specs=pl.BlockSpec((1, tm, d), tmap),
        out_shape=jax.ShapeDtypeStruct((b, s, d), F32),
        scratch_shapes=[pltpu.VMEM((tm + CONV_SPAN * SUBLANES, CONV_CH), F32),
                        pltpu.VMEM((tm, CONV_CH), F32),
                        pltpu.VMEM((tm, CONV_CH), BF16)],
        compiler_params=_cparams(2),
        name="ab_out_prompt",
    )(x, matt, u, u, sgc, tab, cb, lg, lb, w_bf)


C_COLS = 512


def _c_in_kernel(x_ref, ng_ref, w_ref, lbl_ref, qq_ref, kk_ref, lf_ref, vv_ref, sg_ref):
    x = x_ref[...]
    ms = jnp.mean(x * x, axis=-1, keepdims=True)
    h = (x * lax.rsqrt(ms + EPS) * ng_ref[...]).astype(BF16)
    p = jax.nn.softmax(lbl_ref[...], axis=0)
    lb = jnp.sum(p[0:LAYER_C + 1], axis=0, keepdims=True) - p[0:1]
    key_w = qq_ref.shape[1]
    per = key_w // C_COLS
    for c in range(per):
        cs = slice(c * C_COLS, (c + 1) * C_COLS)

        def proj(g, cs=cs):
            return jnp.dot(h, w_ref[:, g * key_w + cs.start:g * key_w + cs.stop], preferred_element_type=F32)

        qq_ref[:, cs] = _silu(proj(0)).astype(BF16)
        lbc = lb[:, cs]
        f = lbc + (1.0 - lbc) * jax.nn.sigmoid(proj(1))
        lf_ref[:, cs] = jnp.log(f)
        kk_ref[:, cs] = (1.0 - f).astype(BF16)
        vv_ref[:, cs] = proj(2).astype(BF16)
        sg_ref[:, cs] = _silu(proj(3)).astype(BF16)


def _c_in(x2d, ng, w_bf, lbl, tm):
    t, d = x2d.shape
    key_w = lbl.shape[1]
    row = lambda i: (i, 0)
    const = lambda i: (0, 0)
    tile = lambda dt: jax.ShapeDtypeStruct((t, key_w), dt)
    tspec = pl.BlockSpec((tm, key_w), row)
    return pl.pallas_call(
        _c_in_kernel,
        grid=(t // tm,),
        in_specs=[pl.BlockSpec((tm, d), row), pl.BlockSpec((1, d), const), pl.BlockSpec(w_bf.shape, const),
                  pl.BlockSpec(lbl.shape, const)],
        out_specs=(tspec,) * 5,
        out_shape=(tile(BF16), tile(BF16), tile(F32), tile(BF16), tile(BF16)),
        compiler_params=_cparams(1),
        name="c_in",
    )(x2d, ng, w_bf, lbl)


HGRN_SUB = SUBLANES


def _split3(x):
    hi = x.astype(BF16)
    r1 = x - hi.astype(F32)
    mid = r1.astype(BF16)
    lo = (r1 - mid.astype(F32)).astype(BF16)
    return hi, mid, lo


def _chunk_masks(c):
    t_i = lax.broadcasted_iota(jnp.int32, (c, c), 0)
    s_i = lax.broadcasted_iota(jnp.int32, (c, c), 1)
    same = (t_i // HGRN_SUB) == (s_i // HGRN_SUB)
    diag = [t_i - s_i == d for d in range(HGRN_SUB)]
    col_group = [(s_i // HGRN_SUB) == a for a in range(c // HGRN_SUB)]
    return diag, col_group, same & (s_i <= t_i)


def _hgrn_pairs(q, k, b, f, o_inter_fn):
    c = q.shape[0]
    n_sub = c // HGRN_SUB
    b_last = b[c - 1:c, :]
    o_inter = o_inter_fn((q * jnp.exp2(b)).astype(BF16))
    kt = (k * jnp.exp2(b_last - b)).astype(BF16)

    k_b = k.astype(BF16)
    f3 = f.reshape(n_sub, HGRN_SUB, f.shape[1])
    stack = [q.astype(BF16)]
    prod = f3
    for d in range(1, HGRN_SUB):
        if d > 1:
            prod = prod * pltpu.roll(f3, d - 1, 1)
        stack.append((q * prod.reshape(f.shape)).astype(BF16))
    zd = lax.dot_general(jnp.concatenate(stack, axis=0), k_b, NT_DIMS, preferred_element_type=F32)

    zo = None
    if n_sub > 1:
        b3 = b.reshape(n_sub, HGRN_SUB, b.shape[1])
        b_end = jnp.broadcast_to(b3[:, HGRN_SUB - 1:HGRN_SUB, :], b3.shape).reshape(b.shape)
        ksc = (k * jnp.exp2(b_end - b)).astype(BF16)
        stack = []
        for g in range(1, n_sub):
            r = g * HGRN_SUB
            stack.append((q[r:, :] * jnp.exp2(b[r:, :] - b[r - 1:r, :])).astype(BF16))
        zo = lax.dot_general(jnp.concatenate(stack, axis=0), ksc, NT_DIMS, preferred_element_type=F32)
    return (o_inter, zd, zo), kt, jnp.exp2(b_last)


def _hgrn_combine(parts, v_b, masks):
    o_inter, zd, zo = parts
    c = o_inter.shape[0]
    n_sub = c // HGRN_SUB
    diag, col_group, same_lower = masks
    a = jnp.zeros((c, c), F32)
    row = 0
    for g in range(1, n_sub):
        r = g * HGRN_SUB
        part = jnp.concatenate([jnp.zeros((r, c), F32), zo[row:row + c - r, :]], axis=0)
        a = jnp.where(col_group[g - 1], part, a)
        row += c - r
    a_in = zd[0:c, :]
    for d in range(1, HGRN_SUB):
        a_in = jnp.where(diag[d], zd[d * c:(d + 1) * c, :], a_in)
    a = jnp.where(same_lower, a_in, a)
    return o_inter + jnp.dot(a.astype(BF16), v_b, preferred_element_type=F32)


def _head_norm_gate(o, og, sg):
    ms = jnp.mean(o * o, axis=-1, keepdims=True)
    return ((o * lax.rsqrt(ms + EPS) * og) * sg.astype(F32)).astype(BF16)


HGRN_CHUNK = 64


def _c_scan_prompt_kernel(qq_ref, kk_ref, lf_ref, vv_ref, sg_ref, y0_ref, og_ref, w_ref,
                          y_ref, s_ref, st_ref, m_ref, zd_ref, zo_ref):
    t = pl.program_id(1)
    ts = qq_ref.shape[1]
    c = HGRN_CHUNK
    dk, dv = HGRN_DK, HGRN_DV

    @pl.when(t == 0)
    def _():
        st_ref[...] = jnp.zeros_like(st_ref)

    def chunk(ci, carry):
        masks = _chunk_masks(c)
        r_i = lax.broadcasted_iota(jnp.int32, (c, c), 0)
        c_i = lax.broadcasted_iota(jnp.int32, (c, c), 1)
        tri = jnp.where(c_i <= r_i, 1.0, 0.0).astype(BF16)
        r0 = pl.multiple_of(ci * c, c)
        rows = pl.ds(r0, c)
        lf2 = lf_ref[0, rows, :] * LOG2_E
        f_all = jnp.exp2(lf2)
        b_all = sum(jnp.dot(tri, part, preferred_element_type=F32) for part in _split3(lf2))
        parts = []
        for h in range(HGRN_HEADS):
            ks = slice(h * dk, (h + 1) * dk)
            vs = slice(h * dv, (h + 1) * dv)
            st = st_ref[h]
            st_b = st.astype(BF16)
            (o_inter, zd, zo), kt, decay = _hgrn_pairs(
                qq_ref[0, rows, ks].astype(F32), kk_ref[0, rows, ks].astype(F32), b_all[:, ks], f_all[:, ks],
                lambda qt, st_b=st_b: lax.dot_general(qt, st_b, NT_DIMS, preferred_element_type=F32))
            zd_ref[h] = zd
            zo_ref[h] = zo
            parts.append(o_inter)
            st_ref[h] = st * decay + lax.dot_general(vv_ref[0, rows, vs], kt, TN_DIMS, preferred_element_type=F32)
        for h in range(HGRN_HEADS):
            vs = slice(h * dv, (h + 1) * dv)
            o = _hgrn_combine((parts[h], zd_ref[h], zo_ref[h]), vv_ref[0, rows, vs], masks)
            m_ref[rows, vs] = _head_norm_gate(o, og_ref[...], sg_ref[0, rows, vs])
        return carry

    lax.fori_loop(0, ts // c, chunk, 0)
    y_ref[0] = y0_ref[0] + jnp.dot(m_ref[...], w_ref[...], preferred_element_type=F32)

    @pl.when(t == pl.num_programs(1) - 1)
    def _():
        for h in range(HGRN_HEADS):
            s_ref[0, h] = st_ref[h].T


def _c_scan_prompt(qq, kk, lf, vv, sg, y0, og, w_bf, ts):
    b, s, d = y0.shape
    key_w, val_w = qq.shape[2], vv.shape[2]
    n_sub = HGRN_CHUNK // HGRN_SUB
    tmap = lambda bi, t: (bi, t, 0)
    const = lambda bi, t: (0, 0)
    return pl.pallas_call(
        _c_scan_prompt_kernel,
        grid=(b, s // ts),
        in_specs=[pl.BlockSpec((1, ts, key_w), tmap), pl.BlockSpec((1, ts, key_w), tmap),
                  pl.BlockSpec((1, ts, key_w), tmap), pl.BlockSpec((1, ts, val_w), tmap),
                  pl.BlockSpec((1, ts, val_w), tmap), pl.BlockSpec((1, ts, d), tmap),
                  pl.BlockSpec((1, HGRN_DV), const), pl.BlockSpec(w_bf.shape, const)],
        out_specs=(pl.BlockSpec((1, ts, d), tmap),
                   pl.BlockSpec((1, HGRN_HEADS, HGRN_DK, HGRN_DV), lambda bi, t: (bi, 0, 0, 0))),
        out_shape=(jax.ShapeDtypeStruct((b, s, d), F32),
                   jax.ShapeDtypeStruct((b, HGRN_HEADS, HGRN_DK, HGRN_DV), F32)),
        scratch_shapes=[pltpu.VMEM((HGRN_HEADS, HGRN_DV, HGRN_DK), F32),
                        pltpu.VMEM((ts, val_w), BF16),
                        pltpu.VMEM((HGRN_HEADS, HGRN_SUB * HGRN_CHUNK, HGRN_CHUNK), F32),
                        pltpu.VMEM((HGRN_HEADS, (n_sub * (n_sub - 1) // 2) * HGRN_SUB, HGRN_CHUNK), F32)],
        compiler_params=_cparams(2),
        name="c_scan_prompt",
    )(qq, kk, lf, vv, sg, y0, og, w_bf)


def _page_copies(pt_ref, ck_hbm, cv_hbm, kbuf, vbuf, sem, seq, sl, n_pages):
    out = []
    for p in range(n_pages):
        page = pt_ref[seq, p]
        toks = pl.ds(p * PAGE_SIZE, PAGE_SIZE)
        out.append(pltpu.make_async_copy(ck_hbm.at[page], kbuf.at[sl, :, :, toks], sem.at[0, sl]))
        out.append(pltpu.make_async_copy(cv_hbm.at[page], vbuf.at[sl, :, :, toks], sem.at[1, sl]))
    return out


def _ab_in_sample_kernel(pt_ref, x_ref, ng_ref, w_ref, qg_ref, kg_ref, gsum_ref,
                         sq_ref, skn_ref, svn_ref, ssga_ref, ck_hbm, cv_hbm, *rest,
                         kv_transposed, n_pages, t_new, per_step):
    *outs, so_ref, h_ref, kbuf, vbuf, sem = rest
    step = pl.program_id(0)
    n_seq = pl.num_programs(0) * per_step
    copies = functools.partial(_page_copies, pt_ref, ck_hbm, cv_hbm, kbuf, vbuf, sem, n_pages=n_pages)

    _normalise_rows(x_ref, ng_ref, h_ref)
    stages = _ab_in_stages(h_ref, w_ref, qg_ref, kg_ref, gsum_ref, *outs, kv_transposed)
    share = -(-len(stages) // per_step)

    @pl.when(step == 0)
    def _():
        for cp in copies(seq=0, sl=0):
            cp.start()

    for j in range(per_step):
        seq = step * per_step + j
        slot = j % 2

        @pl.when(seq + 1 < n_seq)
        def _():
            for cp in copies(seq=seq + 1, sl=1 - slot):
                cp.start()

        for cp in copies(seq=seq, sl=slot):
            cp.wait()

        def between(j=j):
            for stage in stages[j * share:(j + 1) * share]:
                stage()

        _sample_attention(j, slot, sq_ref, skn_ref, svn_ref, ssga_ref, so_ref, kbuf, vbuf,
                          t_new, n_pages * PAGE_SIZE // MOBA_BLOCK, between)


def _sample_attention(sq, slot, qs_ref, kn_ref, vn_ref, sga_ref, o_ref, kbuf, vbuf, t_new, nb, between):
    rows = qs_ref.shape[2]
    b_i = lax.broadcasted_iota(jnp.int32, (rows, LANES), 1)
    o_row = lax.broadcasted_iota(jnp.int32, (rows, rows), 0)
    o_col = lax.broadcasted_iota(jnp.int32, (rows, rows), 1)
    raw = []
    for h in range(ATT_HEADS):
        q_h = qs_ref[sq, h]
        raw.append((jnp.dot(q_h, kbuf[slot, h].astype(BF16), preferred_element_type=F32),
                    lax.dot_general(q_h, kn_ref[sq, h], NT_DIMS, preferred_element_type=F32)))
    between()
    for h in range(ATT_HEADS):
        s, s_own = raw[h]
        blocks = [s[:, j * MOBA_BLOCK:(j + 1) * MOBA_BLOCK] for j in range(nb)]
        gates = [jnp.sum(blk, axis=1, keepdims=True) for blk in blocks]
        gate = jnp.full((rows, LANES), -jnp.inf, F32)
        for i in range(nb):
            gate = jnp.where(b_i == i, gates[i], gate)
        rank = jnp.zeros((rows, LANES), F32)
        for i in range(nb):
            gi = gates[i]
            rank = rank + jnp.where(gi > gate, 1.0, jnp.where((gi == gate) & (b_i > i), 1.0, 0.0))
        sel_bias = jnp.where(rank < min(MOBA_TOPK, nb), 0.0, NEG)
        s = jnp.concatenate([blocks[j] + sel_bias[:, j:j + 1] for j in range(nb)], axis=1)

        s_own = jnp.where((o_col <= o_row) & (o_col < t_new), s_own, NEG)

        m = jnp.maximum(jnp.max(s, axis=1, keepdims=True), jnp.max(s_own, axis=1, keepdims=True))
        p = jnp.exp2(s - m)
        p_own = jnp.exp2(s_own - m)
        l = jnp.sum(p, axis=1, keepdims=True) + jnp.sum(p_own, axis=1, keepdims=True)
        o = (lax.dot_general(p.astype(BF16), vbuf[slot, h].astype(BF16), NT_DIMS, preferred_element_type=F32)
             + jnp.dot(p_own.astype(BF16), vn_ref[sq, h], preferred_element_type=F32)) / l
        o_ref[sq, h] = o * sga_ref[sq, h]


def _ab_out_sample_kernel(x_ref, matt_ref, u_ref, buf_ref, sgc_ref, tab_ref, cb_ref, lg_ref, lb_ref, w_ref,
                          y_ref, xpad_ref, m_ref):
    n_seq, t_new, _ = u_ref.shape
    hist = buf_ref.shape[1]
    per = SUBLANES // t_new
    assert per * t_new == SUBLANES and n_seq % per == 0 and hist == CONV_LEN - 1
    span = CONV_SPAN * SUBLANES
    xpad_ref[...] = jnp.zeros_like(xpad_ref)
    row8 = lax.broadcasted_iota(jnp.int32, (SUBLANES, CONV_CH), 0)

    def group(gi, carry):
        y8 = jnp.zeros((SUBLANES, CONV_CH), F32)
        for j in range(per):
            sq = gi * per + j
            off = j * t_new
            xpad_ref[j, CONV_HALO - hist + off:CONV_HALO + off, :] = buf_ref[sq]
            xpad_ref[j, CONV_HALO + off:CONV_HALO + off + t_new, :] = u_ref[sq]
            cols = []
            for lt in range(CONV_CH // LANES):
                ls = slice(lt * LANES, (lt + 1) * LANES)
                acc = jnp.broadcast_to(cb_ref[:, ls], (SUBLANES, LANES))
                for ms in range(span):
                    if _tap_used(ms):
                        acc = acc + tab_ref[ms, :, ls] * xpad_ref[j, pl.ds(ms, 1), ls]
                cols.append(acc)
            yj = jnp.concatenate(cols, axis=1)
            y8 = jnp.where((row8 >= off) & (row8 < off + t_new), yj, y8)
        r0 = pl.multiple_of(gi * SUBLANES, SUBLANES)
        rows = pl.ds(r0, SUBLANES)
        m_ref[rows, 0:ATT_WIDTH] = matt_ref[rows, :]
        m_ref[rows, ATT_WIDTH:] = _ln_silu_gate(y8, lg_ref, lb_ref, sgc_ref[rows, :]).astype(F32)
        return carry

    lax.fori_loop(0, n_seq // per, group, 0)
    y_ref[...] = x_ref[...] + jnp.dot(m_ref[...].astype(BF16), w_ref[...], preferred_element_type=F32)


def _ab_out_sample(x2d, matt2d, u3, buf, sgc2d, tab, cb, lg, lb, w_bf):
    t, d = x2d.shape
    n_seq, t_new, _ = u3.shape
    per = SUBLANES // t_new
    vm = lambda: pl.BlockSpec(memory_space=pltpu.VMEM)
    return pl.pallas_call(
        _ab_out_sample_kernel,
        in_specs=[vm() for _ in range(10)],
        out_specs=vm(),
        out_shape=jax.ShapeDtypeStruct((t, d), F32),
        scratch_shapes=[pltpu.VMEM((per, CONV_HALO + 2 * SUBLANES, CONV_CH), F32),
                        pltpu.VMEM((t, ATT_WIDTH + CONV_CH), F32)],
        compiler_params=pltpu.CompilerParams(vmem_limit_bytes=VMEM_LIMIT),
        name="ab_out_sample",
    )(x2d, matt2d, u3, buf, sgc2d, tab, cb, lg, lb, w_bf)


def _c_scan_sample_kernel(qq_ref, kk_ref, lf_ref, vv_ref, sg_ref, y0_ref, s0_ref, og_ref, w_ref,
                          y_ref, s_ref, *, t_new):
    per = SUBLANES // t_new
    c = SUBLANES
    dk, dv = HGRN_DK, HGRN_DV
    masks = _chunk_masks(c)
    r_i = lax.broadcasted_iota(jnp.int32, (c, c), 0)
    c_i = lax.broadcasted_iota(jnp.int32, (c, c), 1)
    tri = jnp.where(c_i <= r_i, 1.0, 0.0).astype(BF16)
    row = lax.broadcasted_iota(jnp.int32, (c, 1), 0)
    lf = lf_ref[0]
    m_heads = [jnp.zeros((c, dv), F32)] * HGRN_HEADS
    pending = []
    for j in range(per):
        mine = (row >= j * t_new) & (row < (j + 1) * t_new)
        keep = jnp.where(mine, 1.0, 0.0)
        lf2 = lf * (keep * LOG2_E)
        f_all = jnp.exp2(lf2)
        b_all = sum(jnp.dot(tri, part, preferred_element_type=F32) for part in _split3(lf2))
        for h in range(HGRN_HEADS):
            ks = slice(h * dk, (h + 1) * dk)
            vs = slice(h * dv, (h + 1) * dv)
            st = s0_ref[j, h]
            st_b = st.astype(BF16)
            v_b = (vv_ref[0, :, vs].astype(F32) * keep).astype(BF16)
            parts, kt, decay = _hgrn_pairs(
                qq_ref[0, :, ks].astype(F32) * keep, kk_ref[0, :, ks].astype(F32) * keep,
                b_all[:, ks], f_all[:, ks],
                lambda qt, st_b=st_b: jnp.dot(qt, st_b, preferred_element_type=F32))
            decay_col = jnp.broadcast_to(decay, (dv, dk)).T
            s_ref[j, h] = st * decay_col + lax.dot_general(kt, v_b, TN_DIMS, preferred_element_type=F32)
            pending.append((h, parts, v_b))
    for h, parts, v_b in pending:
        vs = slice(h * dv, (h + 1) * dv)
        o = _hgrn_combine(parts, v_b, masks)
        m_heads[h] = m_heads[h] + _head_norm_gate(o, og_ref[...], sg_ref[0, :, vs]).astype(F32)
    m1 = jnp.concatenate(m_heads, axis=1).astype(BF16)
    y_ref[0] = y0_ref[0] + jnp.dot(m1, w_ref[...], preferred_element_type=F32)


def _c_scan_sample(qq, kk, lf, vv, sg, y0, s0, og, w_bf, t_new):
    groups, rows, d = y0.shape
    per = SUBLANES // t_new
    key_w, val_w = qq.shape[2], vv.shape[2]
    gmap = lambda g: (g, 0, 0)
    const = lambda g: (0, 0)
    smap = lambda g: (g, 0, 0, 0)
    sspec = pl.BlockSpec((per, HGRN_HEADS, HGRN_DK, HGRN_DV), smap)
    return pl.pallas_call(
        functools.partial(_c_scan_sample_kernel, t_new=t_new),
        grid=(groups,),
        in_specs=[pl.BlockSpec((1, rows, key_w), gmap), pl.BlockSpec((1, rows, key_w), gmap),
                  pl.BlockSpec((1, rows, key_w), gmap), pl.BlockSpec((1, rows, val_w), gmap),
                  pl.BlockSpec((1, rows, val_w), gmap), pl.BlockSpec((1, rows, d), gmap), sspec,
                  pl.BlockSpec((1, HGRN_DV), const), pl.BlockSpec(w_bf.shape, const)],
        out_specs=(pl.BlockSpec((1, rows, d), gmap), sspec),
        out_shape=(jax.ShapeDtypeStruct((groups, rows, d), F32),
                   jax.ShapeDtypeStruct(s0.shape, F32)),
        compiler_params=_cparams(1),
        name="c_scan_sample",
    )(qq, kk, lf, vv, sg, y0, s0, og, w_bf)


def kernel(x_prompt, x_sample, cache_k, cache_v, state_conv, state_hgrn, page_table, norm_0, w_in_0, q_norm_0, k_norm_0, conv_w_0, conv_b_0, conv_ln_g_0, conv_ln_b_0, w_out_0, norm_1, w_in_1, lb_logits, o_norm_1, w_out_1):
    b, s, d = x_prompt.shape
    n_seq, t_new, _ = x_sample.shape
    n_tok = n_seq * t_new
    hist = CONV_LEN - 1
    gsum = jnp.kron(jnp.eye(ATT_WIDTH // 2 // ATT_HEAD_DIM, dtype=F32),
                    jnp.full((ATT_HEAD_DIM, ATT_HEAD_DIM), 1.0 / ATT_HEAD_DIM, F32)).astype(BF16)
    qg = jnp.tile(q_norm_0, ATT_HEADS)[None]
    kg = jnp.tile(k_norm_0, ATT_HEADS)[None]
    w_in_0b = w_in_0.astype(BF16)
    w_out_0b = w_out_0.astype(BF16)
    w_in_1b = w_in_1.astype(BF16)
    w_out_1b = w_out_1.astype(BF16)
    tab = _conv_tap_table(conv_w_0)
    conv_args = (tab, conv_b_0[None], conv_ln_g_0[None], conv_ln_b_0[None], w_out_0b)
    heads = lambda a, lead: a.reshape(lead + (ATT_HEADS, ATT_HEAD_DIM))

    qs_s, k_s, v_s, _, _, sga_s, u_s, sgc_s, _ = _ab_in(
        x_sample.reshape(n_tok, d), norm_0[None], w_in_0b, qg, kg, gsum, n_tok)
    q3 = lambda a: a.reshape(n_seq, t_new, a.shape[-1])

    def per_head(a, dt):
        a = a.reshape(n_seq, t_new, ATT_HEADS, ATT_HEAD_DIM).transpose(0, 2, 1, 3).astype(dt)
        return jnp.pad(a, ((0, 0), (0, 0), (0, SUBLANES - t_new), (0, 0)))

    sample_att = (page_table, per_head(qs_s, BF16), per_head(k_s, BF16), per_head(v_s, BF16), per_head(sga_s, F32),
                  cache_k.transpose(0, 2, 3, 1), cache_v.transpose(0, 2, 3, 1), t_new)

    qs, kt_p, vt_p, kb, vbt, sga, u_p, sgc, kmean, matt_s = _ab_in(
        x_prompt.reshape(b * s, d), norm_0[None], w_in_0b, qg, kg, gsum, ROW_TILE, seq_len=s, sample=sample_att)
    heads_t = lambda a: a.reshape(b, ATT_HEADS, ATT_HEAD_DIM, s).transpose(0, 3, 1, 2)
    r3 = lambda a: a.reshape(b, s, a.shape[-1])
    matt = _moba_prompt(r3(qs), r3(kb), vbt, kmean.reshape(b, s // MOBA_BLOCK, ATT_WIDTH), r3(sga))
    y0 = _ab_out_prompt(x_prompt, matt, r3(u_p), r3(sgc), *conv_args, ROW_TILE)
    qq, kk, lf, vv, sg = _c_in(y0.reshape(b * s, d), norm_1[None], w_in_1b, lb_logits, ROW_TILE)
    y_prompt, hgrn_prompt = _c_scan_prompt(r3(qq), r3(kk), r3(lf), r3(vv), r3(sg), y0, o_norm_1[None], w_out_1b,
                                           SCAN_TILE)

    matt_s = matt_s[:, :, :t_new, :].transpose(0, 2, 1, 3)
    y0_s = _ab_out_sample(x_sample.reshape(n_tok, d), matt_s.reshape(n_tok, ATT_WIDTH), q3(u_s), state_conv,
                          sgc_s.astype(F32), *conv_args)
    qq_s, kk_s, lf_s, vv_s, sg_s = _c_in(y0_s, norm_1[None], w_in_1b, lb_logits, n_tok)
    g8 = lambda a: a.reshape(n_tok // SUBLANES, SUBLANES, a.shape[-1])
    y_s, hgrn_sample = _c_scan_sample(g8(qq_s), g8(kk_s), g8(lf_s), g8(vv_s), g8(sg_s), g8(y0_s), state_hgrn,
                                      o_norm_1[None], w_out_1b, t_new)

    conv_prompt = r3(u_p)[:, s - hist:, :]
    conv_sample = jnp.concatenate([state_conv[:, t_new:, :], q3(u_s)], axis=1)
    return (y_prompt, y_s.reshape(n_seq, t_new, d),
            heads_t(kt_p), heads_t(vt_p), heads(k_s, (n_seq, t_new)), heads(v_s, (n_seq, t_new)),
            conv_prompt, conv_sample, hgrn_prompt, hgrn_sample)
```

```python
import functools

import jax
import jax.numpy as jnp
from jax import lax
from jax.experimental import pallas as pl
from jax.experimental.pallas import tpu as pltpu

EPS = 1e-6
ATT_HEADS = 8
ATT_HEAD_DIM = 64
ATT_WIDTH = ATT_HEADS * ATT_HEAD_DIM
MOBA_BLOCK = 256
MOBA_TOPK = 3
MOBA_PAIR = 2
PAGE_SLOTS = 3
PAGE_SIZE = 128
CONV_CH = 512
CONV_LEN = 31
HGRN_HEADS = 8
HGRN_DK = 128
HGRN_DV = 128
LAYER_C = 1

LANES = 128
SUBLANES = 8
LOG2_E = 1.4426950408889634
HEADS_PER_VREG = LANES // ATT_HEAD_DIM
NEG = -1e30
VMEM_LIMIT = 56 * 1024 * 1024
ROW_TILE = 512
SCAN_TILE = 512
SCAN_GROUPS = 4

F32 = jnp.float32
BF16 = jnp.bfloat16
NT_DIMS = (((1,), (1,)), ((), ()))
TN_DIMS = (((0,), (0,)), ((), ()))


def _silu(x):
    return x * jax.nn.sigmoid(x)


def _cparams(n_axes):
    return pltpu.CompilerParams(dimension_semantics=("arbitrary",) * n_axes,
                                vmem_limit_bytes=VMEM_LIMIT)


def _ab_in_stages(h_ref, w_ref, qg_ref, kg_ref, gsum_ref,
                  qs_ref, k_ref, v_ref, kb_ref, vb_ref, sga_ref, u_ref, sgc_ref, kmean_ref, kv_transposed):
    def proj(c):
        return jnp.dot(h_ref[...], w_ref[:, c * ATT_WIDTH:(c + 1) * ATT_WIDTH], preferred_element_type=F32)

    def head_rms(z, g):
        zz = (z * z).astype(BF16)
        half = ATT_WIDTH // 2
        ms_h = jnp.concatenate(
            [jnp.dot(zz[:, :half], gsum_ref[...], preferred_element_type=F32),
             jnp.dot(zz[:, half:], gsum_ref[...], preferred_element_type=F32)], axis=1)
        return z * lax.rsqrt(ms_h + EPS) * g

    def stage_q():
        q = head_rms(proj(0), qg_ref[...])
        qs_ref[...] = (q * (ATT_HEAD_DIM ** -0.5 * LOG2_E)).astype(BF16)

    def stage_k():
        k = head_rms(proj(1), kg_ref[...])
        if kv_transposed:
            k_ref[0] = k.T
        else:
            k_ref[...] = k
        kb_ref[...] = k.astype(BF16)
        tm = k.shape[0]
        kmean_ref[0] = jnp.mean(k.reshape(tm // MOBA_BLOCK, MOBA_BLOCK, ATT_WIDTH), axis=1)

    def stage_v():
        v = proj(2)
        if kv_transposed:
            v_t = v.T
            v_ref[0] = v_t
            vb_ref[0] = v_t.astype(BF16)
        else:
            v_ref[...] = v
            vb_ref[...] = v.astype(BF16)

    def stage_gate_att():
        sga_ref[...] = _silu(proj(3)).astype(BF16)

    def stage_glu():
        u_ref[...] = proj(4) * jax.nn.sigmoid(proj(5))

    def stage_gate_conv():
        sgc_ref[...] = _silu(proj(6)).astype(BF16)

    return [stage_q, stage_k, stage_v, stage_gate_att, stage_glu, stage_gate_conv]


def _normalise_rows(x_ref, ng_ref, h_ref):
    x = x_ref[...]
    ms = jnp.mean(x * x, axis=-1, keepdims=True)
    h_ref[...] = (x * lax.rsqrt(ms + EPS) * ng_ref[...]).astype(BF16)


def _ab_in_kernel(x_ref, ng_ref, w_ref, qg_ref, kg_ref, gsum_ref, *rest, kv_transposed):
    *outs, h_ref = rest
    _normalise_rows(x_ref, ng_ref, h_ref)
    for stage in _ab_in_stages(h_ref, w_ref, qg_ref, kg_ref, gsum_ref, *outs, kv_transposed):
        stage()


def _ab_in(x2d, ng, w_bf, qg, kg, gsum, tm, seq_len=None, sample=None):
    t, d = x2d.shape
    wn = w_bf.shape[1]
    steps = t // tm
    row = lambda i, *_: (i, 0)
    const = lambda i, *_: (0, 0)
    tile = lambda dt: jax.ShapeDtypeStruct((t, ATT_WIDTH), dt)
    tspec = pl.BlockSpec((tm, ATT_WIDTH), row)
    if seq_len is None:
        kv_shape, kv_spec = tile, tspec
    else:
        per_seq = seq_len // tm
        kv_shape = lambda dt: jax.ShapeDtypeStruct((t // seq_len, ATT_WIDTH, seq_len), dt)
        kv_spec = pl.BlockSpec((1, ATT_WIDTH, tm), lambda i, *_: (i // per_seq, 0, i % per_seq))
    out_shape = (tile(BF16), kv_shape(F32), kv_shape(F32), tile(BF16), kv_shape(BF16), tile(BF16), tile(F32),
                 tile(BF16), jax.ShapeDtypeStruct((steps, tm // MOBA_BLOCK, ATT_WIDTH), F32))
    in_specs = [pl.BlockSpec((tm, d), row), pl.BlockSpec((1, d), const),
                pl.BlockSpec((d, wn), const, pipeline_mode=pl.Buffered(1)),
                pl.BlockSpec((1, ATT_WIDTH), const), pl.BlockSpec((1, ATT_WIDTH), const),
                pl.BlockSpec(gsum.shape, const)]
    out_specs = ((tspec, kv_spec, kv_spec, tspec, kv_spec) + (tspec,) * 3
                 + (pl.BlockSpec((1, tm // MOBA_BLOCK, ATT_WIDTH), lambda i, *_: (i, 0, 0)),))
    scratch = [pltpu.VMEM((tm, d), BF16)]
    kv_t = seq_len is not None
    if sample is None:
        return pl.pallas_call(
            functools.partial(_ab_in_kernel, kv_transposed=kv_t),
            grid=(steps,), in_specs=in_specs, out_specs=out_specs, out_shape=out_shape,
            scratch_shapes=scratch, compiler_params=_cparams(1), name="ab_in",
        )(x2d, ng, w_bf, qg, kg, gsum)

    page_table, sq, skn, svn, ssga, cache_kt, cache_vt, t_new = sample
    n_seq, _, rows, _ = sq.shape
    n_pages = page_table.shape[1]
    past = n_pages * PAGE_SIZE
    per_step = n_seq // steps
    assert past % MOBA_BLOCK == 0, "the sample group's past must end on a MoBA block boundary"
    assert per_step * steps == n_seq and n_seq >= PAGE_SLOTS
    sspec = pl.BlockSpec((per_step, ATT_HEADS, rows, ATT_HEAD_DIM), lambda i, *_: (i, 0, 0, 0))
    page_buf = pltpu.VMEM((PAGE_SLOTS, ATT_HEADS, ATT_HEAD_DIM, past), F32)
    grid_spec = pltpu.PrefetchScalarGridSpec(
        num_scalar_prefetch=1,
        grid=(steps,),
        in_specs=in_specs + [sspec] * 4 + [pl.BlockSpec(memory_space=pl.ANY)] * 2,
        out_specs=out_specs + (sspec,),
        scratch_shapes=scratch + [page_buf, page_buf, pltpu.SemaphoreType.DMA((2, PAGE_SLOTS))],
    )
    return pl.pallas_call(
        functools.partial(_ab_in_sample_kernel, kv_transposed=kv_t, n_pages=n_pages, t_new=t_new,
                          per_step=per_step),
        grid_spec=grid_spec,
        out_shape=out_shape + (jax.ShapeDtypeStruct(sq.shape, F32),),
        compiler_params=_cparams(1),
        name="ab_in_sample",
    )(page_table, x2d, ng, w_bf, qg, kg, gsum, sq, skn, svn, ssga, cache_kt, cache_vt)


def _moba_prompt_kernel(qs_ref, kb_ref, vt_ref, kmean_ref, sga_ref, o_ref, bias_ref, s_ref, e_ref, *, nb):
    pair = pl.program_id(2)
    blk = MOBA_BLOCK
    cols = HEADS_PER_VREG * blk
    km_parts = _split3(kmean_ref[0])
    lane = lax.broadcasted_iota(jnp.int32, (blk, LANES), 1)
    key_i = lax.broadcasted_iota(jnp.int32, (blk, cols), 0)
    qry_i = lax.broadcasted_iota(jnp.int32, (blk, cols), 1) % blk
    causal_bias = jnp.where(key_i <= qry_i, 0.0, NEG).astype(F32)
    b_i = lax.broadcasted_iota(jnp.int32, (nb, blk), 0)

    def block_kv(jb):
        rows = pl.ds(pl.multiple_of(jb * blk, blk), blk)
        return kb_ref[0, rows, :], vt_ref[0, :, rows]

    def prepare(c):
        j = MOBA_PAIR * pair + c
        q2 = qs_ref[0, c * blk:(c + 1) * blk, :]
        valid = b_i < j
        qhs = []
        for hh in range(HEADS_PER_VREG):
            head_mask = jnp.where((lane // ATT_HEAD_DIM) == hh, 1.0, 0.0).astype(BF16)
            qh = q2 * head_mask
            g = sum(lax.dot_general(part, qh, NT_DIMS, preferred_element_type=F32) for part in km_parts)
            g = jnp.where(valid, g, -jnp.inf)
            rank = jnp.zeros((nb, blk), F32)
            for i in range(nb):
                gi = g[i:i + 1, :]
                rank = rank + jnp.where(gi > g, 1.0, jnp.where((gi == g) & (b_i > i), 1.0, 0.0))
            sel_bias = jnp.where(valid & (rank < MOBA_TOPK), 0.0, NEG)
            bias_ref[c, :, :, hh * blk:(hh + 1) * blk] = jnp.broadcast_to(
                sel_bias[:, None, :], (nb, SUBLANES, blk))
            qhs.append(qh)
        return jnp.concatenate(qhs, axis=0)

    def block_bias(c, jb):
        return jnp.broadcast_to(bias_ref[c, jb][None], (blk // SUBLANES, SUBLANES, cols)).reshape(blk, cols)

    def scores(q_cols, k_rows):
        return lax.dot_general(k_rows, q_cols, NT_DIMS, preferred_element_type=F32)

    def update(carry, s, v_cols):
        if carry is None:
            m_new = jnp.max(s, axis=0, keepdims=True)
            p = jnp.exp2(s - m_new)
            return (m_new, jnp.sum(p, axis=0, keepdims=True),
                    jnp.dot(v_cols, p.astype(BF16), preferred_element_type=F32))
        m, l, acc = carry
        m_new = jnp.maximum(m, jnp.max(s, axis=0, keepdims=True))
        alpha = jnp.exp2(m - m_new)
        p = jnp.exp2(s - m_new)
        l = alpha * l + jnp.sum(p, axis=0, keepdims=True)
        acc = alpha * acc + jnp.dot(v_cols, p.astype(BF16), preferred_element_type=F32)
        return m_new, l, acc

    span = MOBA_PAIR * blk
    last = nb // MOBA_PAIR - 1

    def step_keys(i):
        return pl.ds(pl.multiple_of(jnp.minimum(i, last) * span, span), span)

    def issue(i, buf):
        kj = kb_ref[0, step_keys(i), :]
        for c in range(MOBA_PAIR):
            s_ref[buf, c] = scores(q_cols[c], kj)

    def consume(i, buf, carry):
        vj = vt_ref[0, :, step_keys(i)]
        dead = jnp.where(i < pair, 0.0, NEG)
        out = []
        for c in range(MOBA_PAIR):
            bias = jnp.concatenate([block_bias(c, jnp.minimum(i, last) * MOBA_PAIR + u)
                                    for u in range(MOBA_PAIR)], axis=0)
            out.append(update(carry[c], s_ref[buf, c] + (bias + dead), vj))
        return tuple(out)

    def body(t, carry):
        i0 = 2 * t
        issue(i0 + 1, 1)
        carry = consume(i0, 0, carry)
        issue(i0 + 2, 0)
        return consume(i0 + 1, 1, carry)

    q_cols = [prepare(c) for c in range(MOBA_PAIR)]
    own_kv = [block_kv(MOBA_PAIR * pair + c) for c in range(MOBA_PAIR)]
    cross = [(c, e) for c in range(1, MOBA_PAIR) for e in range(c)]
    for c in range(MOBA_PAIR):
        e_ref[c] = scores(q_cols[c], own_kv[c][0])
    for n, (c, e) in enumerate(cross):
        e_ref[MOBA_PAIR + n] = scores(q_cols[c], own_kv[e][0])
    issue(0, 0)
    state = [update(None, e_ref[c] + causal_bias, own_kv[c][1]) for c in range(MOBA_PAIR)]
    for n, (c, e) in enumerate(cross):
        state[c] = update(state[c], e_ref[MOBA_PAIR + n] + block_bias(c, MOBA_PAIR * pair + e), own_kv[e][1])
    state = lax.fori_loop(0, (pair + 1) // 2, body, tuple(state))
    for c in range(MOBA_PAIR):
        _, l, acc = state[c]
        o_t = acc / l
        o_t = jnp.concatenate([o_t[hh * ATT_HEAD_DIM:(hh + 1) * ATT_HEAD_DIM, hh * blk:(hh + 1) * blk]
                               for hh in range(HEADS_PER_VREG)], axis=0)
        rows = slice(c * blk, (c + 1) * blk)
        o_ref[0, rows, :] = (o_t.T * sga_ref[0, rows, :].astype(F32)).astype(BF16)


def _moba_prompt(qs, kb, vt, kmean, sga):
    b, s, _ = qs.shape
    nb = s // MOBA_BLOCK
    assert nb % MOBA_PAIR == 0
    npair = ATT_WIDTH // LANES
    rows = MOBA_PAIR * MOBA_BLOCK
    qmap = lambda bi, hp, j: (bi, j, hp)
    kvmap = lambda bi, hp, j: (bi, 0, hp)
    return pl.pallas_call(
        functools.partial(_moba_prompt_kernel, nb=nb),
        grid=(b, npair, nb // MOBA_PAIR),
        in_specs=[pl.BlockSpec((1, rows, LANES), qmap),
                  pl.BlockSpec((1, s, LANES), kvmap),
                  pl.BlockSpec((1, LANES, s), lambda bi, hp, j: (bi, hp, 0)),
                  pl.BlockSpec((1, nb, LANES), kvmap),
                  pl.BlockSpec((1, rows, LANES), qmap)],
        out_specs=pl.BlockSpec((1, rows, LANES), qmap),
        out_shape=jax.ShapeDtypeStruct((b, s, ATT_WIDTH), BF16),
        scratch_shapes=[pltpu.VMEM((MOBA_PAIR, nb, SUBLANES, HEADS_PER_VREG * MOBA_BLOCK), F32),
                        pltpu.VMEM((2, MOBA_PAIR, rows, HEADS_PER_VREG * MOBA_BLOCK), F32),
                        pltpu.VMEM((MOBA_PAIR * (MOBA_PAIR + 1) // 2, MOBA_BLOCK, HEADS_PER_VREG * MOBA_BLOCK), F32)],
        compiler_params=_cparams(3),
        name="moba_prompt",
    )(qs, kb, vt, kmean, sga)


CONV_HALO = 32
CONV_ROWS = 64


CONV_LEAD = CONV_HALO - (CONV_LEN - 1)
CONV_SPAN = -(-(CONV_LEAD + SUBLANES - 1 + CONV_LEN) // SUBLANES)


def _conv_tap_table(conv_w):
    m = jnp.arange(CONV_SPAN)[:, None, None]
    s = jnp.arange(SUBLANES)[None, :, None]
    r = jnp.arange(SUBLANES)[None, None, :]
    idx = SUBLANES * m + s - r - CONV_LEAD
    ok = (idx >= 0) & (idx < CONV_LEN)
    tab = jnp.where(ok[..., None], conv_w[jnp.clip(idx, 0, CONV_LEN - 1)], 0.0)
    return tab.reshape(CONV_SPAN * SUBLANES, SUBLANES, conv_w.shape[-1])


def _tap_used(ms):
    lo = ms - (SUBLANES - 1) - CONV_LEAD
    return lo + SUBLANES - 1 >= 0 and lo < CONV_LEN


def _conv_rows(xpad_ref, ybuf_ref, tab_ref, cb_ref, n_rows):
    groups = CONV_ROWS // SUBLANES
    for lt in range(CONV_CH // LANES):
        ls = slice(lt * LANES, (lt + 1) * LANES)

        for c in range(n_rows // CONV_ROWS):
            r0 = c * CONV_ROWS
            win_ref = xpad_ref.at[pl.ds(r0, CONV_ROWS + CONV_SPAN * SUBLANES)]
            taps = {ms: tab_ref[ms, :, ls] for ms in range(CONV_SPAN * SUBLANES) if _tap_used(ms)}
            acc = [jnp.broadcast_to(cb_ref[:, ls], (SUBLANES, LANES))] * groups
            for rho in range(CONV_ROWS + CONV_SPAN * SUBLANES):
                users = [(rho // SUBLANES - m, SUBLANES * m + rho % SUBLANES) for m in range(CONV_SPAN)]
                users = [(g, ms) for g, ms in users if 0 <= g < groups and ms in taps]
                if not users:
                    continue
                xb = win_ref[pl.ds(rho, 1), ls]
                for g, ms in users:
                    acc[g] = acc[g] + taps[ms] * xb
            ybuf_ref[pl.ds(r0, CONV_ROWS), ls] = jnp.concatenate(acc, axis=0)


def _ln_silu_gate(y, lg_ref, lb_ref, sgc):
    mu = jnp.mean(y, axis=-1, keepdims=True)
    cen = y - mu
    var = jnp.mean(cen * cen, axis=-1, keepdims=True)
    yn = cen * lax.rsqrt(var + EPS) * lg_ref[...] + lb_ref[...]
    return (_silu(yn) * sgc.astype(F32)).astype(BF16)


def _ab_out_prompt_kernel(x_ref, matt_ref, u_ref, halo_ref, sgc_ref, tab_ref, cb_ref, lg_ref, lb_ref, w_ref,
                          y_ref, xpad_ref, ybuf_ref, m_ref):
    t = pl.program_id(1)
    tm = u_ref.shape[1]
    halo = halo_ref[0]
    xpad_ref[0:CONV_HALO, :] = jnp.where(t == 0, jnp.zeros_like(halo), halo)
    xpad_ref[CONV_HALO:CONV_HALO + tm, :] = u_ref[0]
    y_att = x_ref[0] + jnp.dot(matt_ref[0], w_ref[0:ATT_WIDTH, :], preferred_element_type=F32)
    _conv_rows(xpad_ref, ybuf_ref, tab_ref, cb_ref, tm)
    for c in range(tm // CONV_ROWS):
        rows = slice(c * CONV_ROWS, (c + 1) * CONV_ROWS)
        m_ref[rows, :] = _ln_silu_gate(ybuf_ref[rows, :], lg_ref, lb_ref, sgc_ref[0, rows, :])
    y_ref[0] = y_att + jnp.dot(m_ref[...], w_ref[ATT_WIDTH:, :], preferred_element_type=F32)


def _ab_out_prompt(x, matt, u, sgc, tab, cb, lg, lb, w_bf, tm):
    b, s, d = x.shape
    hpt = tm // CONV_HALO
    tmap = lambda bi, t: (bi, t, 0)
    hmap = lambda bi, t: (bi, jnp.maximum(t * hpt - 1, 0), 0)
    const = lambda bi, t: (0, 0)
    return pl.pallas_call(
        _ab_out_prompt_kernel,
        grid=(b, s // tm),
        in_specs=[pl.BlockSpec((1, tm, d), tmap), pl.BlockSpec((1, tm, ATT_WIDTH), tmap),
                  pl.BlockSpec((1, tm, CONV_CH), tmap), pl.BlockSpec((1, CONV_HALO, CONV_CH), hmap),
                  pl.BlockSpec((1, tm, CONV_CH), tmap),
                  pl.BlockSpec(tab.shape, lambda bi, t: (0, 0, 0)), pl.BlockSpec((1, CONV_CH), const),
                  pl.BlockSpec((1, CONV_CH), const), pl.BlockSpec((1, CONV_CH), const),
                  pl.BlockSpec(w_bf.shape, const)],
        out_specs=pl.BlockSpec((1, tm, d), tmap),
        out_shape=jax.ShapeDtypeStruct((b, s, d), F32),
        scratch_shapes=[pltpu.VMEM((tm + CONV_SPAN * SUBLANES, CONV_CH), F32),
                        pltpu.VMEM((tm, CONV_CH), F32),
                        pltpu.VMEM((tm, CONV_CH), BF16)],
        compiler_params=_cparams(2),
        name="ab_out_prompt",
    )(x, matt, u, u, sgc, tab, cb, lg, lb, w_bf)


C_COLS = 512


def _c_in_kernel(x_ref, ng_ref, w_ref, lbl_ref, qq_ref, kk_ref, lf_ref, vv_ref, sg_ref):
    x = x_ref[...]
    ms = jnp.mean(x * x, axis=-1, keepdims=True)
    h = (x * lax.rsqrt(ms + EPS) * ng_ref[...]).astype(BF16)
    p = jax.nn.softmax(lbl_ref[...], axis=0)
    lb = jnp.sum(p[0:LAYER_C + 1], axis=0, keepdims=True) - p[0:1]
    key_w = qq_ref.shape[1]
    per = key_w // C_COLS
    for c in range(per):
        cs = slice(c * C_COLS, (c + 1) * C_COLS)

        def proj(g, cs=cs):
            return jnp.dot(h, w_ref[:, g * key_w + cs.start:g * key_w + cs.stop], preferred_element_type=F32)

        qq_ref[:, cs] = _silu(proj(0)).astype(BF16)
        lbc = lb[:, cs]
        f = lbc + (1.0 - lbc) * jax.nn.sigmoid(proj(1))
        lf_ref[:, cs] = jnp.log(f)
        kk_ref[:, cs] = (1.0 - f).astype(BF16)
        vv_ref[:, cs] = proj(2).astype(BF16)
        sg_ref[:, cs] = _silu(proj(3)).astype(BF16)


def _c_in(x2d, ng, w_bf, lbl, tm):
    t, d = x2d.shape
    key_w = lbl.shape[1]
    row = lambda i: (i, 0)
    const = lambda i: (0, 0)
    tile = lambda dt: jax.ShapeDtypeStruct((t, key_w), dt)
    tspec = pl.BlockSpec((tm, key_w), row)
    return pl.pallas_call(
        _c_in_kernel,
        grid=(t // tm,),
        in_specs=[pl.BlockSpec((tm, d), row), pl.BlockSpec((1, d), const), pl.BlockSpec(w_bf.shape, const),
                  pl.BlockSpec(lbl.shape, const)],
        out_specs=(tspec,) * 5,
        out_shape=(tile(BF16), tile(BF16), tile(F32), tile(BF16), tile(BF16)),
        compiler_params=_cparams(1),
        name="c_in",
    )(x2d, ng, w_bf, lbl)


HGRN_SUB = SUBLANES


def _split3(x):
    hi = x.astype(BF16)
    r1 = x - hi.astype(F32)
    mid = r1.astype(BF16)
    lo = (r1 - mid.astype(F32)).astype(BF16)
    return hi, mid, lo


def _chunk_masks(c):
    t_i = lax.broadcasted_iota(jnp.int32, (c, c), 0)
    s_i = lax.broadcasted_iota(jnp.int32, (c, c), 1)
    same = (t_i // HGRN_SUB) == (s_i // HGRN_SUB)
    diag = [t_i - s_i == d for d in range(HGRN_SUB)]
    col_group = [(s_i // HGRN_SUB) == a for a in range(c // HGRN_SUB)]
    return diag, col_group, same & (s_i <= t_i)


def _hgrn_pairs(q, k, b, f, o_inter_fn):
    c = q.shape[0]
    n_sub = c // HGRN_SUB
    b_last = b[c - 1:c, :]
    o_inter = o_inter_fn((q * jnp.exp2(b)).astype(BF16))
    kt = (k * jnp.exp2(b_last - b)).astype(BF16)

    k_b = k.astype(BF16)
    f3 = f.reshape(n_sub, HGRN_SUB, f.shape[1])
    stack = [q.astype(BF16)]
    prod = f3
    for d in range(1, HGRN_SUB):
        if d > 1:
            prod = prod * pltpu.roll(f3, d - 1, 1)
        stack.append((q * prod.reshape(f.shape)).astype(BF16))
    zd = lax.dot_general(jnp.concatenate(stack, axis=0), k_b, NT_DIMS, preferred_element_type=F32)

    zo = None
    if n_sub > 1:
        b3 = b.reshape(n_sub, HGRN_SUB, b.shape[1])
        b_end = jnp.broadcast_to(b3[:, HGRN_SUB - 1:HGRN_SUB, :], b3.shape).reshape(b.shape)
        ksc = (k * jnp.exp2(b_end - b)).astype(BF16)
        stack = []
        for g in range(1, n_sub):
            r = g * HGRN_SUB
            stack.append((q[r:, :] * jnp.exp2(b[r:, :] - b[r - 1:r, :])).astype(BF16))
        zo = lax.dot_general(jnp.concatenate(stack, axis=0), ksc, NT_DIMS, preferred_element_type=F32)
    return (o_inter, zd, zo), kt, jnp.exp2(b_last)


def _hgrn_combine(parts, v_b, masks):
    o_inter, zd, zo = parts
    c = o_inter.shape[0]
    n_sub = c // HGRN_SUB
    diag, col_group, same_lower = masks
    a = jnp.zeros((c, c), F32)
    row = 0
    for g in range(1, n_sub):
        r = g * HGRN_SUB
        part = jnp.concatenate([jnp.zeros((r, c), F32), zo[row:row + c - r, :]], axis=0)
        a = jnp.where(col_group[g - 1], part, a)
        row += c - r
    a_in = zd[0:c, :]
    for d in range(1, HGRN_SUB):
        a_in = jnp.where(diag[d], zd[d * c:(d + 1) * c, :], a_in)
    a = jnp.where(same_lower, a_in, a)
    return o_inter + jnp.dot(a.astype(BF16), v_b, preferred_element_type=F32)


def _head_norm_gate(o, og, sg):
    ms = jnp.mean(o * o, axis=-1, keepdims=True)
    return ((o * lax.rsqrt(ms + EPS) * og) * sg.astype(F32)).astype(BF16)


HGRN_CHUNK = 64


def _c_scan_prompt_kernel(qq_ref, kk_ref, lf_ref, vv_ref, sg_ref, y0_ref, og_ref, w_ref,
                          y_ref, s_ref, st_ref, m_ref, zd_ref, zo_ref):
    t = pl.program_id(1)
    ts = qq_ref.shape[1]
    c = HGRN_CHUNK
    dk, dv = HGRN_DK, HGRN_DV

    @pl.when(t == 0)
    def _():
        st_ref[...] = jnp.zeros_like(st_ref)

    def chunk(ci, carry):
        masks = _chunk_masks(c)
        r_i = lax.broadcasted_iota(jnp.int32, (c, c), 0)
        c_i = lax.broadcasted_iota(jnp.int32, (c, c), 1)
        tri = jnp.where(c_i <= r_i, 1.0, 0.0).astype(BF16)
        r0 = pl.multiple_of(ci * c, c)
        rows = pl.ds(r0, c)
        lf2 = lf_ref[0, rows, :] * LOG2_E
        f_all = jnp.exp2(lf2)
        b_all = sum(jnp.dot(tri, part, preferred_element_type=F32) for part in _split3(lf2))
        parts = []
        for h in range(HGRN_HEADS):
            ks = slice(h * dk, (h + 1) * dk)
            vs = slice(h * dv, (h + 1) * dv)
            st = st_ref[h]
            st_b = st.astype(BF16)
            (o_inter, zd, zo), kt, decay = _hgrn_pairs(
                qq_ref[0, rows, ks].astype(F32), kk_ref[0, rows, ks].astype(F32), b_all[:, ks], f_all[:, ks],
                lambda qt, st_b=st_b: lax.dot_general(qt, st_b, NT_DIMS, preferred_element_type=F32))
            zd_ref[h] = zd
            zo_ref[h] = zo
            parts.append(o_inter)
            st_ref[h] = st * decay + lax.dot_general(vv_ref[0, rows, vs], kt, TN_DIMS, preferred_element_type=F32)
        for h in range(HGRN_HEADS):
            vs = slice(h * dv, (h + 1) * dv)
            o = _hgrn_combine((parts[h], zd_ref[h], zo_ref[h]), vv_ref[0, rows, vs], masks)
            m_ref[rows, vs] = _head_norm_gate(o, og_ref[...], sg_ref[0, rows, vs])
        return carry

    lax.fori_loop(0, ts // c, chunk, 0)
    y_ref[0] = y0_ref[0] + jnp.dot(m_ref[...], w_ref[...], preferred_element_type=F32)

    @pl.when(t == pl.num_programs(1) - 1)
    def _():
        for h in range(HGRN_HEADS):
            s_ref[0, h] = st_ref[h].T


def _c_scan_prompt(qq, kk, lf, vv, sg, y0, og, w_bf, ts):
    b, s, d = y0.shape
    key_w, val_w = qq.shape[2], vv.shape[2]
    n_sub = HGRN_CHUNK // HGRN_SUB
    tmap = lambda bi, t: (bi, t, 0)
    const = lambda bi, t: (0, 0)
    return pl.pallas_call(
        _c_scan_prompt_kernel,
        grid=(b, s // ts),
        in_specs=[pl.BlockSpec((1, ts, key_w), tmap), pl.BlockSpec((1, ts, key_w), tmap),
                  pl.BlockSpec((1, ts, key_w), tmap), pl.BlockSpec((1, ts, val_w), tmap),
                  pl.BlockSpec((1, ts, val_w), tmap), pl.BlockSpec((1, ts, d), tmap),
                  pl.BlockSpec((1, HGRN_DV), const), pl.BlockSpec(w_bf.shape, const)],
        out_specs=(pl.BlockSpec((1, ts, d), tmap),
                   pl.BlockSpec((1, HGRN_HEADS, HGRN_DK, HGRN_DV), lambda bi, t: (bi, 0, 0, 0))),
        out_shape=(jax.ShapeDtypeStruct((b, s, d), F32),
                   jax.ShapeDtypeStruct((b, HGRN_HEADS, HGRN_DK, HGRN_DV), F32)),
        scratch_shapes=[pltpu.VMEM((HGRN_HEADS, HGRN_DV, HGRN_DK), F32),
                        pltpu.VMEM((ts, val_w), BF16),
                        pltpu.VMEM((HGRN_HEADS, HGRN_SUB * HGRN_CHUNK, HGRN_CHUNK), F32),
                        pltpu.VMEM((HGRN_HEADS, (n_sub * (n_sub - 1) // 2) * HGRN_SUB, HGRN_CHUNK), F32)],
        compiler_params=_cparams(2),
        name="c_scan_prompt",
    )(qq, kk, lf, vv, sg, y0, og, w_bf)


def _page_copies(pt_ref, ck_hbm, cv_hbm, kbuf, vbuf, sem, seq, sl, n_pages):
    out = []
    for p in range(n_pages):
        page = pt_ref[seq, p]
        toks = pl.ds(p * PAGE_SIZE, PAGE_SIZE)
        out.append(pltpu.make_async_copy(ck_hbm.at[page], kbuf.at[sl, :, :, toks], sem.at[0, sl]))
        out.append(pltpu.make_async_copy(cv_hbm.at[page], vbuf.at[sl, :, :, toks], sem.at[1, sl]))
    return out


def _ab_in_sample_kernel(pt_ref, x_ref, ng_ref, w_ref, qg_ref, kg_ref, gsum_ref,
                         sq_ref, skn_ref, svn_ref, ssga_ref, ck_hbm, cv_hbm, *rest,
                         kv_transposed, n_pages, t_new, per_step):
    *outs, so_ref, h_ref, kbuf, vbuf, sem = rest
    step = pl.program_id(0)
    n_seq = pl.num_programs(0) * per_step
    copies = functools.partial(_page_copies, pt_ref, ck_hbm, cv_hbm, kbuf, vbuf, sem, n_pages=n_pages)

    _normalise_rows(x_ref, ng_ref, h_ref)
    stages = _ab_in_stages(h_ref, w_ref, qg_ref, kg_ref, gsum_ref, *outs, kv_transposed)
    share = -(-len(stages) // per_step)

    ahead = PAGE_SLOTS - 1

    @pl.when(step == 0)
    def _():
        for s0 in range(ahead):
            for cp in copies(seq=s0, sl=s0):
                cp.start()

    for j in range(per_step):
        seq = step * per_step + j
        slot = lax.rem(seq, PAGE_SLOTS)

        @pl.when(seq + ahead < n_seq)
        def _():
            for cp in copies(seq=seq + ahead, sl=lax.rem(seq + ahead, PAGE_SLOTS)):
                cp.start()

        for cp in copies(seq=seq, sl=slot):
            cp.wait()

        def between(j=j):
            for stage in stages[j * share:(j + 1) * share]:
                stage()

        _sample_attention(j, slot, sq_ref, skn_ref, svn_ref, ssga_ref, so_ref, kbuf, vbuf,
                          t_new, n_pages * PAGE_SIZE // MOBA_BLOCK, between)


def _sample_attention(sq, slot, qs_ref, kn_ref, vn_ref, sga_ref, o_ref, kbuf, vbuf, t_new, nb, between):
    rows = qs_ref.shape[2]
    b_i = lax.broadcasted_iota(jnp.int32, (rows, LANES), 1)
    o_row = lax.broadcasted_iota(jnp.int32, (rows, rows), 0)
    o_col = lax.broadcasted_iota(jnp.int32, (rows, rows), 1)
    raw = []
    for h in range(ATT_HEADS):
        q_h = qs_ref[sq, h]
        raw.append((jnp.dot(q_h, kbuf[slot, h].astype(BF16), preferred_element_type=F32),
                    lax.dot_general(q_h, kn_ref[sq, h], NT_DIMS, preferred_element_type=F32)))
    between()
    for h in range(ATT_HEADS):
        s, s_own = raw[h]
        blocks = [s[:, j * MOBA_BLOCK:(j + 1) * MOBA_BLOCK] for j in range(nb)]
        gates = [jnp.sum(blk, axis=1, keepdims=True) for blk in blocks]
        gate = jnp.full((rows, LANES), -jnp.inf, F32)
        for i in range(nb):
            gate = jnp.where(b_i == i, gates[i], gate)
        rank = jnp.zeros((rows, LANES), F32)
        for i in range(nb):
            gi = gates[i]
            rank = rank + jnp.where(gi > gate, 1.0, jnp.where((gi == gate) & (b_i > i), 1.0, 0.0))
        sel_bias = jnp.where(rank < min(MOBA_TOPK, nb), 0.0, NEG)
        s = jnp.concatenate([blocks[j] + sel_bias[:, j:j + 1] for j in range(nb)], axis=1)

        s_own = jnp.where((o_col <= o_row) & (o_col < t_new), s_own, NEG)

        m = jnp.maximum(jnp.max(s, axis=1, keepdims=True), jnp.max(s_own, axis=1, keepdims=True))
        p = jnp.exp2(s - m)
        p_own = jnp.exp2(s_own - m)
        l = jnp.sum(p, axis=1, keepdims=True) + jnp.sum(p_own, axis=1, keepdims=True)
        o = (lax.dot_general(p.astype(BF16), vbuf[slot, h].astype(BF16), NT_DIMS, preferred_element_type=F32)
             + jnp.dot(p_own.astype(BF16), vn_ref[sq, h], preferred_element_type=F32)) / l
        o_ref[sq, h] = o * sga_ref[sq, h]


def _ab_out_sample_kernel(x_ref, matt_ref, u_ref, buf_ref, sgc_ref, tab_ref, cb_ref, lg_ref, lb_ref, w_ref,
                          y_ref, xpad_ref, m_ref):
    n_seq, t_new, _ = u_ref.shape
    hist = buf_ref.shape[1]
    per = SUBLANES // t_new
    assert per * t_new == SUBLANES and n_seq % per == 0 and hist == CONV_LEN - 1
    span = CONV_SPAN * SUBLANES
    xpad_ref[...] = jnp.zeros_like(xpad_ref)
    row8 = lax.broadcasted_iota(jnp.int32, (SUBLANES, CONV_CH), 0)

    def group(gi, carry):
        y8 = jnp.zeros((SUBLANES, CONV_CH), F32)
        for j in range(per):
            sq = gi * per + j
            off = j * t_new
            xpad_ref[j, CONV_HALO - hist + off:CONV_HALO + off, :] = buf_ref[sq]
            xpad_ref[j, CONV_HALO + off:CONV_HALO + off + t_new, :] = u_ref[sq]
            cols = []
            for lt in range(CONV_CH // LANES):
                ls = slice(lt * LANES, (lt + 1) * LANES)
                acc = jnp.broadcast_to(cb_ref[:, ls], (SUBLANES, LANES))
                for ms in range(span):
                    if _tap_used(ms):
                        acc = acc + tab_ref[ms, :, ls] * xpad_ref[j, pl.ds(ms, 1), ls]
                cols.append(acc)
            yj = jnp.concatenate(cols, axis=1)
            y8 = jnp.where((row8 >= off) & (row8 < off + t_new), yj, y8)
        r0 = pl.multiple_of(gi * SUBLANES, SUBLANES)
        rows = pl.ds(r0, SUBLANES)
        m_ref[rows, 0:ATT_WIDTH] = matt_ref[rows, :]
        m_ref[rows, ATT_WIDTH:] = _ln_silu_gate(y8, lg_ref, lb_ref, sgc_ref[rows, :]).astype(F32)
        return carry

    lax.fori_loop(0, n_seq // per, group, 0)
    y_ref[...] = x_ref[...] + jnp.dot(m_ref[...].astype(BF16), w_ref[...], preferred_element_type=F32)


def _ab_out_sample(x2d, matt2d, u3, buf, sgc2d, tab, cb, lg, lb, w_bf):
    t, d = x2d.shape
    n_seq, t_new, _ = u3.shape
    per = SUBLANES // t_new
    vm = lambda: pl.BlockSpec(memory_space=pltpu.VMEM)
    return pl.pallas_call(
        _ab_out_sample_kernel,
        in_specs=[vm() for _ in range(10)],
        out_specs=vm(),
        out_shape=jax.ShapeDtypeStruct((t, d), F32),
        scratch_shapes=[pltpu.VMEM((per, CONV_HALO + 2 * SUBLANES, CONV_CH), F32),
                        pltpu.VMEM((t, ATT_WIDTH + CONV_CH), F32)],
        compiler_params=pltpu.CompilerParams(vmem_limit_bytes=VMEM_LIMIT),
        name="ab_out_sample",
    )(x2d, matt2d, u3, buf, sgc2d, tab, cb, lg, lb, w_bf)


def _c_scan_sample_kernel(qq_ref, kk_ref, lf_ref, vv_ref, sg_ref, y0_ref, s0_ref, og_ref, w_ref,
                          y_ref, s_ref, *, t_new):
    n_groups = qq_ref.shape[0]
    per = SUBLANES // t_new
    c = SUBLANES
    dk, dv = HGRN_DK, HGRN_DV
    masks = _chunk_masks(c)
    r_i = lax.broadcasted_iota(jnp.int32, (c, c), 0)
    c_i = lax.broadcasted_iota(jnp.int32, (c, c), 1)
    tri = jnp.where(c_i <= r_i, 1.0, 0.0).astype(BF16)
    row = lax.broadcasted_iota(jnp.int32, (c, 1), 0)
    pending = []
    for g in range(n_groups):
        lf = lf_ref[g]
        for j in range(per):
            sq = g * per + j
            mine = (row >= j * t_new) & (row < (j + 1) * t_new)
            keep = jnp.where(mine, 1.0, 0.0)
            lf2 = lf * (keep * LOG2_E)
            f_all = jnp.exp2(lf2)
            b_all = sum(jnp.dot(tri, part, preferred_element_type=F32) for part in _split3(lf2))
            for h in range(HGRN_HEADS):
                ks = slice(h * dk, (h + 1) * dk)
                vs = slice(h * dv, (h + 1) * dv)
                st = s0_ref[sq, h]
                st_b = st.astype(BF16)
                v_b = (vv_ref[g, :, vs].astype(F32) * keep).astype(BF16)
                parts, kt, decay = _hgrn_pairs(
                    qq_ref[g, :, ks].astype(F32) * keep, kk_ref[g, :, ks].astype(F32) * keep,
                    b_all[:, ks], f_all[:, ks],
                    lambda qt, st_b=st_b: jnp.dot(qt, st_b, preferred_element_type=F32))
                decay_col = jnp.broadcast_to(decay, (dv, dk)).T
                s_ref[sq, h] = st * decay_col + lax.dot_general(kt, v_b, TN_DIMS, preferred_element_type=F32)
                pending.append((g, h, parts, v_b))
    m_heads = [[jnp.zeros((c, dv), F32)] * HGRN_HEADS for _ in range(n_groups)]
    for g, h, parts, v_b in pending:
        vs = slice(h * dv, (h + 1) * dv)
        o = _hgrn_combine(parts, v_b, masks)
        m_heads[g][h] = m_heads[g][h] + _head_norm_gate(o, og_ref[...], sg_ref[g, :, vs]).astype(F32)
    m1 = jnp.concatenate([jnp.concatenate(m_heads[g], axis=1) for g in range(n_groups)], axis=0).astype(BF16)
    y = jnp.dot(m1, w_ref[...], preferred_element_type=F32)
    for g in range(n_groups):
        y_ref[g] = y0_ref[g] + y[g * c:(g + 1) * c, :]


def _c_scan_sample(qq, kk, lf, vv, sg, y0, s0, og, w_bf, t_new):
    groups, rows, d = y0.shape
    per = SUBLANES // t_new
    gs = SCAN_GROUPS
    assert groups % gs == 0
    key_w, val_w = qq.shape[2], vv.shape[2]
    gmap = lambda g: (g, 0, 0)
    const = lambda g: (0, 0)
    smap = lambda g: (g, 0, 0, 0)
    sspec = pl.BlockSpec((gs * per, HGRN_HEADS, HGRN_DK, HGRN_DV), smap)
    return pl.pallas_call(
        functools.partial(_c_scan_sample_kernel, t_new=t_new),
        grid=(groups // gs,),
        in_specs=[pl.BlockSpec((gs, rows, key_w), gmap), pl.BlockSpec((gs, rows, key_w), gmap),
                  pl.BlockSpec((gs, rows, key_w), gmap), pl.BlockSpec((gs, rows, val_w), gmap),
                  pl.BlockSpec((gs, rows, val_w), gmap), pl.BlockSpec((gs, rows, d), gmap), sspec,
                  pl.BlockSpec((1, HGRN_DV), const), pl.BlockSpec(w_bf.shape, const)],
        out_specs=(pl.BlockSpec((gs, rows, d), gmap), sspec),
        out_shape=(jax.ShapeDtypeStruct((groups, rows, d), F32),
                   jax.ShapeDtypeStruct(s0.shape, F32)),
        compiler_params=_cparams(1),
        name="c_scan_sample",
    )(qq, kk, lf, vv, sg, y0, s0, og, w_bf)


def kernel(x_prompt, x_sample, cache_k, cache_v, state_conv, state_hgrn, page_table, norm_0, w_in_0, q_norm_0, k_norm_0, conv_w_0, conv_b_0, conv_ln_g_0, conv_ln_b_0, w_out_0, norm_1, w_in_1, lb_logits, o_norm_1, w_out_1):
    b, s, d = x_prompt.shape
    n_seq, t_new, _ = x_sample.shape
    n_tok = n_seq * t_new
    hist = CONV_LEN - 1
    gsum = jnp.kron(jnp.eye(ATT_WIDTH // 2 // ATT_HEAD_DIM, dtype=F32),
                    jnp.full((ATT_HEAD_DIM, ATT_HEAD_DIM), 1.0 / ATT_HEAD_DIM, F32)).astype(BF16)
    qg = jnp.tile(q_norm_0, ATT_HEADS)[None]
    kg = jnp.tile(k_norm_0, ATT_HEADS)[None]
    w_in_0b = w_in_0.astype(BF16)
    w_out_0b = w_out_0.astype(BF16)
    w_in_1b = w_in_1.astype(BF16)
    w_out_1b = w_out_1.astype(BF16)
    tab = _conv_tap_table(conv_w_0)
    conv_args = (tab, conv_b_0[None], conv_ln_g_0[None], conv_ln_b_0[None], w_out_0b)
    heads = lambda a, lead: a.reshape(lead + (ATT_HEADS, ATT_HEAD_DIM))

    qs_s, k_s, v_s, _, _, sga_s, u_s, sgc_s, _ = _ab_in(
        x_sample.reshape(n_tok, d), norm_0[None], w_in_0b, qg, kg, gsum, n_tok)
    q3 = lambda a: a.reshape(n_seq, t_new, a.shape[-1])

    def per_head(a, dt):
        a = a.reshape(n_seq, t_new, ATT_HEADS, ATT_HEAD_DIM).transpose(0, 2, 1, 3).astype(dt)
        return jnp.pad(a, ((0, 0), (0, 0), (0, SUBLANES - t_new), (0, 0)))

    sample_att = (page_table, per_head(qs_s, BF16), per_head(k_s, BF16), per_head(v_s, BF16), per_head(sga_s, F32),
                  cache_k.transpose(0, 2, 3, 1), cache_v.transpose(0, 2, 3, 1), t_new)

    qs, kt_p, vt_p, kb, vbt, sga, u_p, sgc, kmean, matt_s = _ab_in(
        x_prompt.reshape(b * s, d), norm_0[None], w_in_0b, qg, kg, gsum, ROW_TILE, seq_len=s, sample=sample_att)
    heads_t = lambda a: a.reshape(b, ATT_HEADS, ATT_HEAD_DIM, s).transpose(0, 3, 1, 2)
    r3 = lambda a: a.reshape(b, s, a.shape[-1])
    matt = _moba_prompt(r3(qs), r3(kb), vbt, kmean.reshape(b, s // MOBA_BLOCK, ATT_WIDTH), r3(sga))
    y0 = _ab_out_prompt(x_prompt, matt, r3(u_p), r3(sgc), *conv_args, ROW_TILE)
    qq, kk, lf, vv, sg = _c_in(y0.reshape(b * s, d), norm_1[None], w_in_1b, lb_logits, ROW_TILE)
    y_prompt, hgrn_prompt = _c_scan_prompt(r3(qq), r3(kk), r3(lf), r3(vv), r3(sg), y0, o_norm_1[None], w_out_1b,
                                           SCAN_TILE)

    matt_s = matt_s[:, :, :t_new, :].transpose(0, 2, 1, 3)
    y0_s = _ab_out_sample(x_sample.reshape(n_tok, d), matt_s.reshape(n_tok, ATT_WIDTH), q3(u_s), state_conv,
                          sgc_s.astype(F32), *conv_args)
    qq_s, kk_s, lf_s, vv_s, sg_s = _c_in(y0_s, norm_1[None], w_in_1b, lb_logits, n_tok)
    g8 = lambda a: a.reshape(n_tok // SUBLANES, SUBLANES, a.shape[-1])
    y_s, hgrn_sample = _c_scan_sample(g8(qq_s), g8(kk_s), g8(lf_s), g8(vv_s), g8(sg_s), g8(y0_s), state_hgrn,
                                      o_norm_1[None], w_out_1b, t_new)

    conv_prompt = r3(u_p)[:, s - hist:, :]
    conv_sample = jnp.concatenate([state_conv[:, t_new:, :], q3(u_s)], axis=1)
    return (y_prompt, y_s.reshape(n_seq, t_new, d),
            heads_t(kt_p), heads_t(vt_p), heads(k_s, (n_seq, t_new)), heads(v_s, (n_seq, t_new)),
            conv_prompt, conv_sample, hgrn_prompt, hgrn_sample)
```

```python
import functools

import jax
import jax.numpy as jnp
from jax import lax
from jax.experimental import pallas as pl
from jax.experimental.pallas import tpu as pltpu

EPS = 1e-6
ATT_HEADS = 8
ATT_HEAD_DIM = 64
ATT_WIDTH = ATT_HEADS * ATT_HEAD_DIM
MOBA_BLOCK = 256
MOBA_TOPK = 3
MOBA_PAIR = 2
PAGE_SLOTS = 3
PAGE_SIZE = 128
CONV_CH = 512
CONV_LEN = 31
HGRN_HEADS = 8
HGRN_DK = 128
HGRN_DV = 128
LAYER_C = 1

LANES = 128
SUBLANES = 8
LOG2_E = 1.4426950408889634
HEADS_PER_VREG = LANES // ATT_HEAD_DIM
NEG = -1e30
VMEM_LIMIT = 56 * 1024 * 1024
ROW_TILE = 512
SCAN_TILE = 512
SCAN_GROUPS = 4

F32 = jnp.float32
BF16 = jnp.bfloat16
NT_DIMS = (((1,), (1,)), ((), ()))
TN_DIMS = (((0,), (0,)), ((), ()))


def _silu(x):
    return x * jax.nn.sigmoid(x)


def _cparams(n_axes):
    return pltpu.CompilerParams(dimension_semantics=("arbitrary",) * n_axes,
                                vmem_limit_bytes=VMEM_LIMIT)


def _ab_in_stages(h_ref, w_ref, qg_ref, kg_ref, gsum_ref,
                  qs_ref, k_ref, v_ref, kb_ref, vb_ref, sga_ref, u_ref, sgc_ref, kmean_ref, kv_transposed):
    def proj(c):
        return jnp.dot(h_ref[...], w_ref[:, c * ATT_WIDTH:(c + 1) * ATT_WIDTH], preferred_element_type=F32)

    def head_rms(z, g):
        zz = (z * z).astype(BF16)
        half = ATT_WIDTH // 2
        ms_h = jnp.concatenate(
            [jnp.dot(zz[:, :half], gsum_ref[...], preferred_element_type=F32),
             jnp.dot(zz[:, half:], gsum_ref[...], preferred_element_type=F32)], axis=1)
        return z * lax.rsqrt(ms_h + EPS) * g

    def stage_q():
        q = head_rms(proj(0), qg_ref[...])
        qs_ref[...] = (q * (ATT_HEAD_DIM ** -0.5 * LOG2_E)).astype(BF16)

    def stage_k():
        k = head_rms(proj(1), kg_ref[...])
        if kv_transposed:
            k_ref[0] = k.T
        else:
            k_ref[...] = k
        kb_ref[...] = k.astype(BF16)
        tm = k.shape[0]
        kmean_ref[0] = jnp.mean(k.reshape(tm // MOBA_BLOCK, MOBA_BLOCK, ATT_WIDTH), axis=1)

    def stage_v():
        v = proj(2)
        if kv_transposed:
            v_t = v.T
            v_ref[0] = v_t
            vb_ref[0] = v_t.astype(BF16)
        else:
            v_ref[...] = v
            vb_ref[...] = v.astype(BF16)

    def stage_gate_att():
        sga_ref[...] = _silu(proj(3)).astype(BF16)

    def stage_glu():
        u_ref[...] = proj(4) * jax.nn.sigmoid(proj(5))

    def stage_gate_conv():
        sgc_ref[...] = _silu(proj(6)).astype(BF16)

    return [stage_q, stage_k, stage_v, stage_gate_att, stage_glu, stage_gate_conv]


def _normalise_rows(x_ref, ng_ref, h_ref):
    x = x_ref[...]
    ms = jnp.mean(x * x, axis=-1, keepdims=True)
    h_ref[...] = (x * lax.rsqrt(ms + EPS) * ng_ref[...]).astype(BF16)


def _ab_in_kernel(x_ref, ng_ref, w_ref, qg_ref, kg_ref, gsum_ref, *rest, kv_transposed):
    *outs, h_ref = rest
    _normalise_rows(x_ref, ng_ref, h_ref)
    for stage in _ab_in_stages(h_ref, w_ref, qg_ref, kg_ref, gsum_ref, *outs, kv_transposed):
        stage()


def _ab_in(x2d, ng, w_bf, qg, kg, gsum, tm, seq_len=None, sample=None):
    t, d = x2d.shape
    wn = w_bf.shape[1]
    steps = t // tm
    row = lambda i, *_: (i, 0)
    const = lambda i, *_: (0, 0)
    tile = lambda dt: jax.ShapeDtypeStruct((t, ATT_WIDTH), dt)
    tspec = pl.BlockSpec((tm, ATT_WIDTH), row)
    if seq_len is None:
        kv_shape, kv_spec = tile, tspec
    else:
        per_seq = seq_len // tm
        kv_shape = lambda dt: jax.ShapeDtypeStruct((t // seq_len, ATT_WIDTH, seq_len), dt)
        kv_spec = pl.BlockSpec((1, ATT_WIDTH, tm), lambda i, *_: (i // per_seq, 0, i % per_seq))
    out_shape = (tile(BF16), kv_shape(F32), kv_shape(F32), tile(BF16), kv_shape(BF16), tile(BF16), tile(F32),
                 tile(BF16), jax.ShapeDtypeStruct((steps, tm // MOBA_BLOCK, ATT_WIDTH), F32))
    in_specs = [pl.BlockSpec((tm, d), row), pl.BlockSpec((1, d), const),
                pl.BlockSpec((d, wn), const, pipeline_mode=pl.Buffered(1)),
                pl.BlockSpec((1, ATT_WIDTH), const), pl.BlockSpec((1, ATT_WIDTH), const),
                pl.BlockSpec(gsum.shape, const)]
    out_specs = ((tspec, kv_spec, kv_spec, tspec, kv_spec) + (tspec,) * 3
                 + (pl.BlockSpec((1, tm // MOBA_BLOCK, ATT_WIDTH), lambda i, *_: (i, 0, 0)),))
    scratch = [pltpu.VMEM((tm, d), BF16)]
    kv_t = seq_len is not None
    if sample is None:
        return pl.pallas_call(
            functools.partial(_ab_in_kernel, kv_transposed=kv_t),
            grid=(steps,), in_specs=in_specs, out_specs=out_specs, out_shape=out_shape,
            scratch_shapes=scratch, compiler_params=_cparams(1), name="ab_in",
        )(x2d, ng, w_bf, qg, kg, gsum)

    page_table, sq, skn, svn, ssga, cache_kt, cache_vt, t_new = sample
    n_seq, _, rows, _ = sq.shape
    n_pages = page_table.shape[1]
    past = n_pages * PAGE_SIZE
    per_step = n_seq // steps
    assert past % MOBA_BLOCK == 0, "the sample group's past must end on a MoBA block boundary"
    assert per_step * steps == n_seq and n_seq >= PAGE_SLOTS
    sspec = pl.BlockSpec((per_step, ATT_HEADS, rows, ATT_HEAD_DIM), lambda i, *_: (i, 0, 0, 0))
    page_buf = pltpu.VMEM((PAGE_SLOTS, ATT_HEADS, ATT_HEAD_DIM, past), F32)
    grid_spec = pltpu.PrefetchScalarGridSpec(
        num_scalar_prefetch=1,
        grid=(steps,),
        in_specs=in_specs + [sspec] * 4 + [pl.BlockSpec(memory_space=pl.ANY)] * 2,
        out_specs=out_specs + (sspec,),
        scratch_shapes=scratch + [page_buf, page_buf, pltpu.SemaphoreType.DMA((2, PAGE_SLOTS))],
    )
    return pl.pallas_call(
        functools.partial(_ab_in_sample_kernel, kv_transposed=kv_t, n_pages=n_pages, t_new=t_new,
                          per_step=per_step),
        grid_spec=grid_spec,
        out_shape=out_shape + (jax.ShapeDtypeStruct(sq.shape, F32),),
        compiler_params=_cparams(1),
        name="ab_in_sample",
    )(page_table, x2d, ng, w_bf, qg, kg, gsum, sq, skn, svn, ssga, cache_kt, cache_vt)


def _moba_prompt_kernel(qs_ref, kb_ref, vt_ref, kmean_ref, sga_ref, o_ref, bias_ref, s_ref, e_ref, *, nb):
    pair = pl.program_id(2)
    blk = MOBA_BLOCK
    cols = HEADS_PER_VREG * blk
    km_parts = _split3(kmean_ref[0])
    lane = lax.broadcasted_iota(jnp.int32, (blk, LANES), 1)
    key_i = lax.broadcasted_iota(jnp.int32, (blk, cols), 0)
    qry_i = lax.broadcasted_iota(jnp.int32, (blk, cols), 1) % blk
    causal_bias = jnp.where(key_i <= qry_i, 0.0, NEG).astype(F32)
    b_i = lax.broadcasted_iota(jnp.int32, (nb, blk), 0)

    def block_kv(jb):
        rows = pl.ds(pl.multiple_of(jb * blk, blk), blk)
        return kb_ref[0, rows, :], vt_ref[0, :, rows]

    def prepare(c):
        j = MOBA_PAIR * pair + c
        q2 = qs_ref[0, c * blk:(c + 1) * blk, :]
        valid = b_i < j
        qhs = []
        for hh in range(HEADS_PER_VREG):
            head_mask = jnp.where((lane // ATT_HEAD_DIM) == hh, 1.0, 0.0).astype(BF16)
            qh = q2 * head_mask
            g = sum(lax.dot_general(part, qh, NT_DIMS, preferred_element_type=F32) for part in km_parts)
            g = jnp.where(valid, g, -jnp.inf)
            rank = jnp.zeros((nb, blk), F32)
            for i in range(nb):
                gi = g[i:i + 1, :]
                rank = rank + jnp.where(gi > g, 1.0, jnp.where((gi == g) & (b_i > i), 1.0, 0.0))
            sel_bias = jnp.where(valid & (rank < MOBA_TOPK), 0.0, NEG)
            bias_ref[c, :, :, hh * blk:(hh + 1) * blk] = jnp.broadcast_to(
                sel_bias[:, None, :], (nb, SUBLANES, blk))
            qhs.append(qh)
        return jnp.concatenate(qhs, axis=0)

    def block_bias(c, jb):
        return jnp.broadcast_to(bias_ref[c, jb][None], (blk // SUBLANES, SUBLANES, cols)).reshape(blk, cols)

    def scores(q_cols, k_rows):
        return lax.dot_general(k_rows, q_cols, NT_DIMS, preferred_element_type=F32)

    def update(carry, s, v_cols):
        if carry is None:
            m_new = jnp.max(s, axis=0, keepdims=True)
            p = jnp.exp2(s - m_new)
            return (m_new, jnp.sum(p, axis=0, keepdims=True),
                    jnp.dot(v_cols, p.astype(BF16), preferred_element_type=F32))
        m, l, acc = carry
        m_new = jnp.maximum(m, jnp.max(s, axis=0, keepdims=True))
        alpha = jnp.exp2(m - m_new)
        p = jnp.exp2(s - m_new)
        l = alpha * l + jnp.sum(p, axis=0, keepdims=True)
        acc = alpha * acc + jnp.dot(v_cols, p.astype(BF16), preferred_element_type=F32)
        return m_new, l, acc

    span = MOBA_PAIR * blk
    last = nb // MOBA_PAIR - 1

    def step_keys(i):
        return pl.ds(pl.multiple_of(jnp.minimum(i, last) * span, span), span)

    def issue(i, buf):
        kj = kb_ref[0, step_keys(i), :]
        for c in range(MOBA_PAIR):
            s_ref[buf, c] = scores(q_cols[c], kj)

    def consume(i, buf, carry):
        vj = vt_ref[0, :, step_keys(i)]
        dead = jnp.where(i < pair, 0.0, NEG)
        out = []
        for c in range(MOBA_PAIR):
            bias = jnp.concatenate([block_bias(c, jnp.minimum(i, last) * MOBA_PAIR + u)
                                    for u in range(MOBA_PAIR)], axis=0)
            out.append(update(carry[c], s_ref[buf, c] + (bias + dead), vj))
        return tuple(out)

    def body(t, carry):
        i0 = 2 * t
        issue(i0 + 1, 1)
        carry = consume(i0, 0, carry)
        issue(i0 + 2, 0)
        return consume(i0 + 1, 1, carry)

    q_cols = [prepare(c) for c in range(MOBA_PAIR)]
    own_kv = [block_kv(MOBA_PAIR * pair + c) for c in range(MOBA_PAIR)]
    cross = [(c, e) for c in range(1, MOBA_PAIR) for e in range(c)]
    for c in range(MOBA_PAIR):
        e_ref[c] = scores(q_cols[c], own_kv[c][0])
    for n, (c, e) in enumerate(cross):
        e_ref[MOBA_PAIR + n] = scores(q_cols[c], own_kv[e][0])
    issue(0, 0)
    state = [update(None, e_ref[c] + causal_bias, own_kv[c][1]) for c in range(MOBA_PAIR)]
    for n, (c, e) in enumerate(cross):
        state[c] = update(state[c], e_ref[MOBA_PAIR + n] + block_bias(c, MOBA_PAIR * pair + e), own_kv[e][1])
    state = lax.fori_loop(0, (pair + 1) // 2, body, tuple(state))
    for c in range(MOBA_PAIR):
        _, l, acc = state[c]
        o_t = acc / l
        o_t = jnp.concatenate([o_t[hh * ATT_HEAD_DIM:(hh + 1) * ATT_HEAD_DIM, hh * blk:(hh + 1) * blk]
                               for hh in range(HEADS_PER_VREG)], axis=0)
        rows = slice(c * blk, (c + 1) * blk)
        o_ref[0, rows, :] = (o_t.T * sga_ref[0, rows, :].astype(F32)).astype(BF16)


def _moba_prompt(qs, kb, vt, kmean, sga):
    b, s, _ = qs.shape
    nb = s // MOBA_BLOCK
    assert nb % MOBA_PAIR == 0
    npair = ATT_WIDTH // LANES
    rows = MOBA_PAIR * MOBA_BLOCK
    qmap = lambda bi, hp, j: (bi, j, hp)
    kvmap = lambda bi, hp, j: (bi, 0, hp)
    return pl.pallas_call(
        functools.partial(_moba_prompt_kernel, nb=nb),
        grid=(b, npair, nb // MOBA_PAIR),
        in_specs=[pl.BlockSpec((1, rows, LANES), qmap),
                  pl.BlockSpec((1, s, LANES), kvmap),
                  pl.BlockSpec((1, LANES, s), lambda bi, hp, j: (bi, hp, 0)),
                  pl.BlockSpec((1, nb, LANES), kvmap),
                  pl.BlockSpec((1, rows, LANES), qmap)],
        out_specs=pl.BlockSpec((1, rows, LANES), qmap),
        out_shape=jax.ShapeDtypeStruct((b, s, ATT_WIDTH), BF16),
        scratch_shapes=[pltpu.VMEM((MOBA_PAIR, nb, SUBLANES, HEADS_PER_VREG * MOBA_BLOCK), F32),
                        pltpu.VMEM((2, MOBA_PAIR, rows, HEADS_PER_VREG * MOBA_BLOCK), F32),
                        pltpu.VMEM((MOBA_PAIR * (MOBA_PAIR + 1) // 2, MOBA_BLOCK, HEADS_PER_VREG * MOBA_BLOCK), F32)],
        compiler_params=_cparams(3),
        name="moba_prompt",
    )(qs, kb, vt, kmean, sga)


CONV_HALO = 32
CONV_ROWS = 64


CONV_LEAD = CONV_HALO - (CONV_LEN - 1)
CONV_SPAN = -(-(CONV_LEAD + SUBLANES - 1 + CONV_LEN) // SUBLANES)


def _conv_tap_table(conv_w):
    m = jnp.arange(CONV_SPAN)[:, None, None]
    s = jnp.arange(SUBLANES)[None, :, None]
    r = jnp.arange(SUBLANES)[None, None, :]
    idx = SUBLANES * m + s - r - CONV_LEAD
    ok = (idx >= 0) & (idx < CONV_LEN)
    tab = jnp.where(ok[..., None], conv_w[jnp.clip(idx, 0, CONV_LEN - 1)], 0.0)
    return tab.reshape(CONV_SPAN * SUBLANES, SUBLANES, conv_w.shape[-1])


def _tap_used(ms):
    lo = ms - (SUBLANES - 1) - CONV_LEAD
    return lo + SUBLANES - 1 >= 0 and lo < CONV_LEN


def _conv_rows(xpad_ref, ybuf_ref, tab_ref, cb_ref, n_rows):
    groups = CONV_ROWS // SUBLANES
    for lt in range(CONV_CH // LANES):
        ls = slice(lt * LANES, (lt + 1) * LANES)

        for c in range(n_rows // CONV_ROWS):
            r0 = c * CONV_ROWS
            win_ref = xpad_ref.at[pl.ds(r0, CONV_ROWS + CONV_SPAN * SUBLANES)]
            taps = {ms: tab_ref[ms, :, ls] for ms in range(CONV_SPAN * SUBLANES) if _tap_used(ms)}
            acc = [jnp.broadcast_to(cb_ref[:, ls], (SUBLANES, LANES))] * groups
            for rho in range(CONV_ROWS + CONV_SPAN * SUBLANES):
                users = [(rho // SUBLANES - m, SUBLANES * m + rho % SUBLANES) for m in range(CONV_SPAN)]
                users = [(g, ms) for g, ms in users if 0 <= g < groups and ms in taps]
                if not users:
                    continue
                xb = win_ref[pl.ds(rho, 1), ls]
                for g, ms in users:
                    acc[g] = acc[g] + taps[ms] * xb
            ybuf_ref[pl.ds(r0, CONV_ROWS), ls] = jnp.concatenate(acc, axis=0)


def _ln_silu_gate(y, lg_ref, lb_ref, sgc):
    mu = jnp.mean(y, axis=-1, keepdims=True)
    cen = y - mu
    var = jnp.mean(cen * cen, axis=-1, keepdims=True)
    yn = cen * lax.rsqrt(var + EPS) * lg_ref[...] + lb_ref[...]
    return (_silu(yn) * sgc.astype(F32)).astype(BF16)


def _ab_out_prompt_kernel(x_ref, matt_ref, u_ref, halo_ref, sgc_ref, tab_ref, cb_ref, lg_ref, lb_ref, w_ref,
                          y_ref, xpad_ref, ybuf_ref, m_ref):
    t = pl.program_id(1)
    tm = u_ref.shape[1]
    halo = halo_ref[0]
    xpad_ref[0:CONV_HALO, :] = jnp.where(t == 0, jnp.zeros_like(halo), halo)
    xpad_ref[CONV_HALO:CONV_HALO + tm, :] = u_ref[0]
    y_att = x_ref[0] + jnp.dot(matt_ref[0], w_ref[0:ATT_WIDTH, :], preferred_element_type=F32)
    _conv_rows(xpad_ref, ybuf_ref, tab_ref, cb_ref, tm)
    for c in range(tm // CONV_ROWS):
        rows = slice(c * CONV_ROWS, (c + 1) * CONV_ROWS)
        m_ref[rows, :] = _ln_silu_gate(ybuf_ref[rows, :], lg_ref, lb_ref, sgc_ref[0, rows, :])
    y_ref[0] = y_att + jnp.dot(m_ref[...], w_ref[ATT_WIDTH:, :], preferred_element_type=F32)


def _ab_out_prompt(x, matt, u, sgc, tab, cb, lg, lb, w_bf, tm):
    b, s, d = x.shape
    hpt = tm // CONV_HALO
    tmap = lambda bi, t: (bi, t, 0)
    hmap = lambda bi, t: (bi, jnp.maximum(t * hpt - 1, 0), 0)
    const = lambda bi, t: (0, 0)
    return pl.pallas_call(
        _ab_out_prompt_kernel,
        grid=(b, s // tm),
        in_specs=[pl.BlockSpec((1, tm, d), tmap), pl.BlockSpec((1, tm, ATT_WIDTH), tmap),
                  pl.BlockSpec((1, tm, CONV_CH), tmap), pl.BlockSpec((1, CONV_HALO, CONV_CH), hmap),
                  pl.BlockSpec((1, tm, CONV_CH), tmap),
                  pl.BlockSpec(tab.shape, lambda bi, t: (0, 0, 0)), pl.BlockSpec((1, CONV_CH), const),
                  pl.BlockSpec((1, CONV_CH), const), pl.BlockSpec((1, CONV_CH), const),
                  pl.BlockSpec(w_bf.shape, const)],
        out_specs=pl.BlockSpec((1, tm, d), tmap),
        out_shape=jax.ShapeDtypeStruct((b, s, d), F32),
        scratch_shapes=[pltpu.VMEM((tm + CONV_SPAN * SUBLANES, CONV_CH), F32),
                        pltpu.VMEM((tm, CONV_CH), F32),
                        pltpu.VMEM((tm, CONV_CH), BF16)],
        compiler_params=_cparams(2),
        name="ab_out_prompt",
    )(x, matt, u, u, sgc, tab, cb, lg, lb, w_bf)


C_COLS = 512


def _c_in_kernel(x_ref, ng_ref, w_ref, lbl_ref, qq_ref, kk_ref, lf_ref, vv_ref, sg_ref):
    x = x_ref[...]
    ms = jnp.mean(x * x, axis=-1, keepdims=True)
    h = (x * lax.rsqrt(ms + EPS) * ng_ref[...]).astype(BF16)
    p = jax.nn.softmax(lbl_ref[...], axis=0)
    lb = jnp.sum(p[0:LAYER_C + 1], axis=0, keepdims=True) - p[0:1]
    key_w = qq_ref.shape[1]
    per = key_w // C_COLS
    for c in range(per):
        cs = slice(c * C_COLS, (c + 1) * C_COLS)

        def proj(g, cs=cs):
            return jnp.dot(h, w_ref[:, g * key_w + cs.start:g * key_w + cs.stop], preferred_element_type=F32)

        qq_ref[:, cs] = _silu(proj(0)).astype(BF16)
        lbc = lb[:, cs]
        f = lbc + (1.0 - lbc) * jax.nn.sigmoid(proj(1))
        lf_ref[:, cs] = jnp.log(f)
        kk_ref[:, cs] = (1.0 - f).astype(BF16)
        vv_ref[:, cs] = proj(2).astype(BF16)
        sg_ref[:, cs] = _silu(proj(3)).astype(BF16)


def _c_in(x2d, ng, w_bf, lbl, tm):
    t, d = x2d.shape
    key_w = lbl.shape[1]
    row = lambda i: (i, 0)
    const = lambda i: (0, 0)
    tile = lambda dt: jax.ShapeDtypeStruct((t, key_w), dt)
    tspec = pl.BlockSpec((tm, key_w), row)
    return pl.pallas_call(
        _c_in_kernel,
        grid=(t // tm,),
        in_specs=[pl.BlockSpec((tm, d), row), pl.BlockSpec((1, d), const), pl.BlockSpec(w_bf.shape, const),
                  pl.BlockSpec(lbl.shape, const)],
        out_specs=(tspec,) * 5,
        out_shape=(tile(BF16), tile(BF16), tile(F32), tile(BF16), tile(BF16)),
        compiler_params=_cparams(1),
        name="c_in",
    )(x2d, ng, w_bf, lbl)


HGRN_SUB = SUBLANES


def _split3(x):
    hi = x.astype(BF16)
    r1 = x - hi.astype(F32)
    mid = r1.astype(BF16)
    lo = (r1 - mid.astype(F32)).astype(BF16)
    return hi, mid, lo


def _chunk_masks(c):
    t_i = lax.broadcasted_iota(jnp.int32, (c, c), 0)
    s_i = lax.broadcasted_iota(jnp.int32, (c, c), 1)
    same = (t_i // HGRN_SUB) == (s_i // HGRN_SUB)
    diag = [t_i - s_i == d for d in range(HGRN_SUB)]
    col_group = [(s_i // HGRN_SUB) == a for a in range(c // HGRN_SUB)]
    return diag, col_group, same & (s_i <= t_i)


def _hgrn_pairs(q, k, b, f, o_inter_fn):
    c = q.shape[0]
    n_sub = c // HGRN_SUB
    b_last = b[c - 1:c, :]
    o_inter = o_inter_fn((q * jnp.exp2(b)).astype(BF16))
    kt = (k * jnp.exp2(b_last - b)).astype(BF16)

    k_b = k.astype(BF16)
    f3 = f.reshape(n_sub, HGRN_SUB, f.shape[1])
    stack = [q.astype(BF16)]
    prod = f3
    for d in range(1, HGRN_SUB):
        if d > 1:
            prod = prod * pltpu.roll(f3, d - 1, 1)
        stack.append((q * prod.reshape(f.shape)).astype(BF16))
    zd = lax.dot_general(jnp.concatenate(stack, axis=0), k_b, NT_DIMS, preferred_element_type=F32)

    zo = None
    if n_sub > 1:
        b3 = b.reshape(n_sub, HGRN_SUB, b.shape[1])
        b_end = jnp.broadcast_to(b3[:, HGRN_SUB - 1:HGRN_SUB, :], b3.shape).reshape(b.shape)
        ksc = (k * jnp.exp2(b_end - b)).astype(BF16)
        stack = []
        for g in range(1, n_sub):
            r = g * HGRN_SUB
            stack.append((q[r:, :] * jnp.exp2(b[r:, :] - b[r - 1:r, :])).astype(BF16))
        zo = lax.dot_general(jnp.concatenate(stack, axis=0), ksc, NT_DIMS, preferred_element_type=F32)
    return (o_inter, zd, zo), kt, jnp.exp2(b_last)


def _hgrn_combine(parts, v_b, masks):
    o_inter, zd, zo = parts
    c = o_inter.shape[0]
    n_sub = c // HGRN_SUB
    diag, col_group, same_lower = masks
    a = jnp.zeros((c, c), F32)
    row = 0
    for g in range(1, n_sub):
        r = g * HGRN_SUB
        part = jnp.concatenate([jnp.zeros((r, c), F32), zo[row:row + c - r, :]], axis=0)
        a = jnp.where(col_group[g - 1], part, a)
        row += c - r
    a_in = zd[0:c, :]
    for d in range(1, HGRN_SUB):
        a_in = jnp.where(diag[d], zd[d * c:(d + 1) * c, :], a_in)
    a = jnp.where(same_lower, a_in, a)
    return o_inter + jnp.dot(a.astype(BF16), v_b, preferred_element_type=F32)


def _head_norm_gate(o, og, sg):
    ms = jnp.mean(o * o, axis=-1, keepdims=True)
    return ((o * lax.rsqrt(ms + EPS) * og) * sg.astype(F32)).astype(BF16)


HGRN_CHUNK = 64


def _c_layer_prompt_kernel(xn_ref, y0_ref, ng_ref, wh_ref, lbl_ref, og_ref, wo_ref, y_ref, s_ref,
                           st_ref, m_ref, zd_ref, zo_ref, hn_ref, lb_ref, qq_s, kk_s, lf_s, vv_s, sg_s,
                           *, tiles_per_seq):
    g = pl.program_id(0)
    t = lax.rem(g, tiles_per_seq)
    cur = lax.rem(g, 2)
    nxt = 1 - cur
    ts = y0_ref.shape[1]
    c = HGRN_CHUNK
    dk, dv = HGRN_DK, HGRN_DV
    assert ts // c == HGRN_HEADS and dk == dv == LANES

    p = jax.nn.softmax(lbl_ref[...], axis=0)
    lb = jnp.sum(p[0:LAYER_C + 1], axis=0, keepdims=True) - p[0:1]
    for h in range(HGRN_HEADS):
        lb_ref[h] = lb[:, h * dk:(h + 1) * dk]

    def project_qf(hd, buf):
        z = jnp.dot(hn_ref[...], wh_ref[hd, :, 0:2 * dk], preferred_element_type=F32)
        qq_s[buf, hd] = _silu(z[:, 0:dk]).astype(BF16)
        lbh = lb_ref[hd]
        f = lbh + (1.0 - lbh) * jax.nn.sigmoid(z[:, dk:])
        lf_s[buf, hd] = jnp.log(f)
        kk_s[buf, hd] = (1.0 - f).astype(BF16)

    def project_vg(hd, buf):
        z = jnp.dot(hn_ref[...], wh_ref[hd, :, 2 * dk:], preferred_element_type=F32)
        vv_s[buf, hd] = z[:, 0:dv].astype(BF16)
        sg_s[buf, hd] = _silu(z[:, dv:]).astype(BF16)

    def project_head(hd, buf):
        project_qf(hd, buf)
        project_vg(hd, buf)

    def normalise(x):
        ms = jnp.mean(x * x, axis=-1, keepdims=True)
        hn_ref[...] = (x * lax.rsqrt(ms + EPS) * ng_ref[...]).astype(BF16)

    @pl.when(g == 0)
    def _():
        normalise(y0_ref[0])
        for h in range(HGRN_HEADS):
            project_head(h, 0)

    @pl.when(t == 0)
    def _():
        st_ref[...] = jnp.zeros_like(st_ref)

    normalise(xn_ref[0])

    def chunk(ci, carry):
        r0 = pl.multiple_of(ci * c, c)
        rows = pl.ds(r0, c)
        heads = range(HGRN_HEADS)
        q_c = [qq_s[cur, h, rows, :] for h in heads]
        k_c = [kk_s[cur, h, rows, :] for h in heads]
        v_c = [vv_s[cur, h, rows, :] for h in heads]
        sg_c = [sg_s[cur, h, rows, :] for h in heads]
        lf2 = jnp.concatenate([lf_s[cur, h, rows, :] for h in heads], axis=1) * LOG2_E
        project_qf(ci, nxt)
        masks = _chunk_masks(c)
        r_i = lax.broadcasted_iota(jnp.int32, (c, c), 0)
        c_i = lax.broadcasted_iota(jnp.int32, (c, c), 1)
        tri = jnp.where(c_i <= r_i, 1.0, 0.0).astype(BF16)
        f_all = jnp.exp2(lf2)
        b_all = sum(jnp.dot(tri, part, preferred_element_type=F32) for part in _split3(lf2))
        parts = []
        for h in range(HGRN_HEADS):
            ks = slice(h * dk, (h + 1) * dk)
            st = st_ref[h]
            st_b = st.astype(BF16)
            (o_inter, zd, zo), kt, decay = _hgrn_pairs(
                q_c[h].astype(F32), k_c[h].astype(F32), b_all[:, ks], f_all[:, ks],
                lambda qt, st_b=st_b: lax.dot_general(qt, st_b, NT_DIMS, preferred_element_type=F32))
            zd_ref[h] = zd
            zo_ref[h] = zo
            parts.append(o_inter)
            st_ref[h] = st * decay + lax.dot_general(v_c[h], kt, TN_DIMS, preferred_element_type=F32)
        project_vg(ci, nxt)
        for h in range(HGRN_HEADS):
            o = _hgrn_combine((parts[h], zd_ref[h], zo_ref[h]), v_c[h], masks)
            m_ref[rows, h * dv:(h + 1) * dv] = _head_norm_gate(o, og_ref[...], sg_c[h])
        return carry

    lax.fori_loop(0, ts // c, chunk, 0)
    y_ref[0] = y0_ref[0] + jnp.dot(m_ref[...], wo_ref[...], preferred_element_type=F32)

    @pl.when(t == tiles_per_seq - 1)
    def _():
        for h in range(HGRN_HEADS):
            s_ref[0, h] = st_ref[h].T


def _c_layer_prompt(y0, ng, w_heads, lbl, og, wo_bf, ts):
    b, s, d = y0.shape
    tps = s // ts
    n_tiles = b * tps
    n_sub = HGRN_CHUNK // HGRN_SUB
    cur_map = lambda g: (g // tps, g % tps, 0)
    nxt_map = lambda g: (jnp.minimum(g + 1, n_tiles - 1) // tps, jnp.minimum(g + 1, n_tiles - 1) % tps, 0)
    const = lambda g: (0, 0)
    head_buf = lambda dt: pltpu.VMEM((2, HGRN_HEADS, ts, LANES), dt)
    return pl.pallas_call(
        functools.partial(_c_layer_prompt_kernel, tiles_per_seq=tps),
        grid=(n_tiles,),
        in_specs=[pl.BlockSpec((1, ts, d), nxt_map), pl.BlockSpec((1, ts, d), cur_map),
                  pl.BlockSpec((1, d), const),
                  pl.BlockSpec(w_heads.shape, lambda g: (0, 0, 0), pipeline_mode=pl.Buffered(1)),
                  pl.BlockSpec(lbl.shape, const), pl.BlockSpec((1, HGRN_DV), const),
                  pl.BlockSpec(wo_bf.shape, const)],
        out_specs=(pl.BlockSpec((1, ts, d), cur_map),
                   pl.BlockSpec((1, HGRN_HEADS, HGRN_DK, HGRN_DV), lambda g: (g // tps, 0, 0, 0))),
        out_shape=(jax.ShapeDtypeStruct((b, s, d), F32),
                   jax.ShapeDtypeStruct((b, HGRN_HEADS, HGRN_DK, HGRN_DV), F32)),
        scratch_shapes=[pltpu.VMEM((HGRN_HEADS, HGRN_DV, HGRN_DK), F32),
                        pltpu.VMEM((ts, HGRN_HEADS * HGRN_DV), BF16),
                        pltpu.VMEM((HGRN_HEADS, HGRN_SUB * HGRN_CHUNK, HGRN_CHUNK), F32),
                        pltpu.VMEM((HGRN_HEADS, (n_sub * (n_sub - 1) // 2) * HGRN_SUB, HGRN_CHUNK), F32),
                        pltpu.VMEM((ts, d), BF16),
                        pltpu.VMEM((HGRN_HEADS, 1, LANES), F32),
                        head_buf(BF16), head_buf(BF16), head_buf(F32), head_buf(BF16), head_buf(BF16)],
        compiler_params=_cparams(1),
        name="c_layer_prompt",
    )(y0, y0, ng, w_heads, lbl, og, wo_bf)


def _page_copies(pt_ref, ck_hbm, cv_hbm, kbuf, vbuf, sem, seq, sl, n_pages):
    out = []
    for p in range(n_pages):
        page = pt_ref[seq, p]
        toks = pl.ds(p * PAGE_SIZE, PAGE_SIZE)
        out.append(pltpu.make_async_copy(ck_hbm.at[page], kbuf.at[sl, :, :, toks], sem.at[0, sl]))
        out.append(pltpu.make_async_copy(cv_hbm.at[page], vbuf.at[sl, :, :, toks], sem.at[1, sl]))
    return out


def _ab_in_sample_kernel(pt_ref, x_ref, ng_ref, w_ref, qg_ref, kg_ref, gsum_ref,
                         sq_ref, skn_ref, svn_ref, ssga_ref, ck_hbm, cv_hbm, *rest,
                         kv_transposed, n_pages, t_new, per_step):
    *outs, so_ref, h_ref, kbuf, vbuf, sem = rest
    step = pl.program_id(0)
    n_seq = pl.num_programs(0) * per_step
    copies = functools.partial(_page_copies, pt_ref, ck_hbm, cv_hbm, kbuf, vbuf, sem, n_pages=n_pages)

    _normalise_rows(x_ref, ng_ref, h_ref)
    stages = _ab_in_stages(h_ref, w_ref, qg_ref, kg_ref, gsum_ref, *outs, kv_transposed)
    share = -(-len(stages) // per_step)

    ahead = PAGE_SLOTS - 1

    @pl.when(step == 0)
    def _():
        for s0 in range(ahead):
            for cp in copies(seq=s0, sl=s0):
                cp.start()

    for j in range(per_step):
        seq = step * per_step + j
        slot = lax.rem(seq, PAGE_SLOTS)

        @pl.when(seq + ahead < n_seq)
        def _():
            for cp in copies(seq=seq + ahead, sl=lax.rem(seq + ahead, PAGE_SLOTS)):
                cp.start()

        for cp in copies(seq=seq, sl=slot):
            cp.wait()

        def between(j=j):
            for stage in stages[j * share:(j + 1) * share]:
                stage()

        _sample_attention(j, slot, sq_ref, skn_ref, svn_ref, ssga_ref, so_ref, kbuf, vbuf,
                          t_new, n_pages * PAGE_SIZE // MOBA_BLOCK, between)


def _sample_attention(sq, slot, qs_ref, kn_ref, vn_ref, sga_ref, o_ref, kbuf, vbuf, t_new, nb, between):
    rows = qs_ref.shape[2]
    b_i = lax.broadcasted_iota(jnp.int32, (rows, LANES), 1)
    o_row = lax.broadcasted_iota(jnp.int32, (rows, rows), 0)
    o_col = lax.broadcasted_iota(jnp.int32, (rows, rows), 1)
    raw = []
    for h in range(ATT_HEADS):
        q_h = qs_ref[sq, h]
        raw.append((jnp.dot(q_h, kbuf[slot, h].astype(BF16), preferred_element_type=F32),
                    lax.dot_general(q_h, kn_ref[sq, h], NT_DIMS, preferred_element_type=F32)))
    between()
    for h in range(ATT_HEADS):
        s, s_own = raw[h]
        blocks = [s[:, j * MOBA_BLOCK:(j + 1) * MOBA_BLOCK] for j in range(nb)]
        gates = [jnp.sum(blk, axis=1, keepdims=True) for blk in blocks]
        gate = jnp.full((rows, LANES), -jnp.inf, F32)
        for i in range(nb):
            gate = jnp.where(b_i == i, gates[i], gate)
        rank = jnp.zeros((rows, LANES), F32)
        for i in range(nb):
            gi = gates[i]
            rank = rank + jnp.where(gi > gate, 1.0, jnp.where((gi == gate) & (b_i > i), 1.0, 0.0))
        sel_bias = jnp.where(rank < min(MOBA_TOPK, nb), 0.0, NEG)
        s = jnp.concatenate([blocks[j] + sel_bias[:, j:j + 1] for j in range(nb)], axis=1)

        s_own = jnp.where((o_col <= o_row) & (o_col < t_new), s_own, NEG)

        m = jnp.maximum(jnp.max(s, axis=1, keepdims=True), jnp.max(s_own, axis=1, keepdims=True))
        p = jnp.exp2(s - m)
        p_own = jnp.exp2(s_own - m)
        l = jnp.sum(p, axis=1, keepdims=True) + jnp.sum(p_own, axis=1, keepdims=True)
        o = (lax.dot_general(p.astype(BF16), vbuf[slot, h].astype(BF16), NT_DIMS, preferred_element_type=F32)
             + jnp.dot(p_own.astype(BF16), vn_ref[sq, h], preferred_element_type=F32)) / l
        o_ref[sq, h] = o * sga_ref[sq, h]


def _ab_out_sample_kernel(x_ref, matt_ref, u_ref, buf_ref, sgc_ref, tab_ref, cb_ref, lg_ref, lb_ref, w_ref,
                          y_ref, xpad_ref, m_ref):
    n_seq, t_new, _ = u_ref.shape
    hist = buf_ref.shape[1]
    per = SUBLANES // t_new
    assert per * t_new == SUBLANES and n_seq % per == 0 and hist == CONV_LEN - 1
    span = CONV_SPAN * SUBLANES
    xpad_ref[...] = jnp.zeros_like(xpad_ref)
    row8 = lax.broadcasted_iota(jnp.int32, (SUBLANES, CONV_CH), 0)

    def group(gi, carry):
        y8 = jnp.zeros((SUBLANES, CONV_CH), F32)
        for j in range(per):
            sq = gi * per + j
            off = j * t_new
            xpad_ref[j, CONV_HALO - hist + off:CONV_HALO + off, :] = buf_ref[sq]
            xpad_ref[j, CONV_HALO + off:CONV_HALO + off + t_new, :] = u_ref[sq]
            cols = []
            for lt in range(CONV_CH // LANES):
                ls = slice(lt * LANES, (lt + 1) * LANES)
                acc = jnp.broadcast_to(cb_ref[:, ls], (SUBLANES, LANES))
                for ms in range(span):
                    if _tap_used(ms):
                        acc = acc + tab_ref[ms, :, ls] * xpad_ref[j, pl.ds(ms, 1), ls]
                cols.append(acc)
            yj = jnp.concatenate(cols, axis=1)
            y8 = jnp.where((row8 >= off) & (row8 < off + t_new), yj, y8)
        r0 = pl.multiple_of(gi * SUBLANES, SUBLANES)
        rows = pl.ds(r0, SUBLANES)
        m_ref[rows, 0:ATT_WIDTH] = matt_ref[rows, :]
        m_ref[rows, ATT_WIDTH:] = _ln_silu_gate(y8, lg_ref, lb_ref, sgc_ref[rows, :]).astype(F32)
        return carry

    lax.fori_loop(0, n_seq // per, group, 0)
    y_ref[...] = x_ref[...] + jnp.dot(m_ref[...].astype(BF16), w_ref[...], preferred_element_type=F32)


def _ab_out_sample(x2d, matt2d, u3, buf, sgc2d, tab, cb, lg, lb, w_bf):
    t, d = x2d.shape
    n_seq, t_new, _ = u3.shape
    per = SUBLANES // t_new
    vm = lambda: pl.BlockSpec(memory_space=pltpu.VMEM)
    return pl.pallas_call(
        _ab_out_sample_kernel,
        in_specs=[vm() for _ in range(10)],
        out_specs=vm(),
        out_shape=jax.ShapeDtypeStruct((t, d), F32),
        scratch_shapes=[pltpu.VMEM((per, CONV_HALO + 2 * SUBLANES, CONV_CH), F32),
                        pltpu.VMEM((t, ATT_WIDTH + CONV_CH), F32)],
        compiler_params=pltpu.CompilerParams(vmem_limit_bytes=VMEM_LIMIT),
        name="ab_out_sample",
    )(x2d, matt2d, u3, buf, sgc2d, tab, cb, lg, lb, w_bf)


def _c_scan_sample_kernel(qq_ref, kk_ref, lf_ref, vv_ref, sg_ref, y0_ref, s0_ref, og_ref, w_ref,
                          y_ref, s_ref, *, t_new):
    n_groups = qq_ref.shape[0]
    per = SUBLANES // t_new
    c = SUBLANES
    dk, dv = HGRN_DK, HGRN_DV
    masks = _chunk_masks(c)
    r_i = lax.broadcasted_iota(jnp.int32, (c, c), 0)
    c_i = lax.broadcasted_iota(jnp.int32, (c, c), 1)
    tri = jnp.where(c_i <= r_i, 1.0, 0.0).astype(BF16)
    row = lax.broadcasted_iota(jnp.int32, (c, 1), 0)
    pending = []
    for g in range(n_groups):
        lf = lf_ref[g]
        for j in range(per):
            sq = g * per + j
            mine = (row >= j * t_new) & (row < (j + 1) * t_new)
            keep = jnp.where(mine, 1.0, 0.0)
            lf2 = lf * (keep * LOG2_E)
            f_all = jnp.exp2(lf2)
            b_all = sum(jnp.dot(tri, part, preferred_element_type=F32) for part in _split3(lf2))
            for h in range(HGRN_HEADS):
                ks = slice(h * dk, (h + 1) * dk)
                vs = slice(h * dv, (h + 1) * dv)
                st = s0_ref[sq, h]
                st_b = st.astype(BF16)
                v_b = (vv_ref[g, :, vs].astype(F32) * keep).astype(BF16)
                parts, kt, decay = _hgrn_pairs(
                    qq_ref[g, :, ks].astype(F32) * keep, kk_ref[g, :, ks].astype(F32) * keep,
                    b_all[:, ks], f_all[:, ks],
                    lambda qt, st_b=st_b: jnp.dot(qt, st_b, preferred_element_type=F32))
                decay_col = jnp.broadcast_to(decay, (dv, dk)).T
                s_ref[sq, h] = st * decay_col + lax.dot_general(kt, v_b, TN_DIMS, preferred_element_type=F32)
                pending.append((g, h, parts, v_b))
    m_heads = [[jnp.zeros((c, dv), F32)] * HGRN_HEADS for _ in range(n_groups)]
    for g, h, parts, v_b in pending:
        vs = slice(h * dv, (h + 1) * dv)
        o = _hgrn_combine(parts, v_b, masks)
        m_heads[g][h] = m_heads[g][h] + _head_norm_gate(o, og_ref[...], sg_ref[g, :, vs]).astype(F32)
    m1 = jnp.concatenate([jnp.concatenate(m_heads[g], axis=1) for g in range(n_groups)], axis=0).astype(BF16)
    y = jnp.dot(m1, w_ref[...], preferred_element_type=F32)
    for g in range(n_groups):
        y_ref[g] = y0_ref[g] + y[g * c:(g + 1) * c, :]


def _c_scan_sample(qq, kk, lf, vv, sg, y0, s0, og, w_bf, t_new):
    groups, rows, d = y0.shape
    per = SUBLANES // t_new
    gs = SCAN_GROUPS
    assert groups % gs == 0
    key_w, val_w = qq.shape[2], vv.shape[2]
    gmap = lambda g: (g, 0, 0)
    const = lambda g: (0, 0)
    smap = lambda g: (g, 0, 0, 0)
    sspec = pl.BlockSpec((gs * per, HGRN_HEADS, HGRN_DK, HGRN_DV), smap)
    return pl.pallas_call(
        functools.partial(_c_scan_sample_kernel, t_new=t_new),
        grid=(groups // gs,),
        in_specs=[pl.BlockSpec((gs, rows, key_w), gmap), pl.BlockSpec((gs, rows, key_w), gmap),
                  pl.BlockSpec((gs, rows, key_w), gmap), pl.BlockSpec((gs, rows, val_w), gmap),
                  pl.BlockSpec((gs, rows, val_w), gmap), pl.BlockSpec((gs, rows, d), gmap), sspec,
                  pl.BlockSpec((1, HGRN_DV), const), pl.BlockSpec(w_bf.shape, const)],
        out_specs=(pl.BlockSpec((gs, rows, d), gmap), sspec),
        out_shape=(jax.ShapeDtypeStruct((groups, rows, d), F32),
                   jax.ShapeDtypeStruct(s0.shape, F32)),
        compiler_params=_cparams(1),
        name="c_scan_sample",
    )(qq, kk, lf, vv, sg, y0, s0, og, w_bf)


def kernel(x_prompt, x_sample, cache_k, cache_v, state_conv, state_hgrn, page_table, norm_0, w_in_0, q_norm_0, k_norm_0, conv_w_0, conv_b_0, conv_ln_g_0, conv_ln_b_0, w_out_0, norm_1, w_in_1, lb_logits, o_norm_1, w_out_1):
    b, s, d = x_prompt.shape
    n_seq, t_new, _ = x_sample.shape
    n_tok = n_seq * t_new
    hist = CONV_LEN - 1
    gsum = jnp.kron(jnp.eye(ATT_WIDTH // 2 // ATT_HEAD_DIM, dtype=F32),
                    jnp.full((ATT_HEAD_DIM, ATT_HEAD_DIM), 1.0 / ATT_HEAD_DIM, F32)).astype(BF16)
    qg = jnp.tile(q_norm_0, ATT_HEADS)[None]
    kg = jnp.tile(k_norm_0, ATT_HEADS)[None]
    w_in_0b = w_in_0.astype(BF16)
    w_out_0b = w_out_0.astype(BF16)
    w_in_1b = w_in_1.astype(BF16)
    w_out_1b = w_out_1.astype(BF16)
    tab = _conv_tap_table(conv_w_0)
    conv_args = (tab, conv_b_0[None], conv_ln_g_0[None], conv_ln_b_0[None], w_out_0b)
    heads = lambda a, lead: a.reshape(lead + (ATT_HEADS, ATT_HEAD_DIM))

    qs_s, k_s, v_s, _, _, sga_s, u_s, sgc_s, _ = _ab_in(
        x_sample.reshape(n_tok, d), norm_0[None], w_in_0b, qg, kg, gsum, n_tok)
    q3 = lambda a: a.reshape(n_seq, t_new, a.shape[-1])

    def per_head(a, dt):
        a = a.reshape(n_seq, t_new, ATT_HEADS, ATT_HEAD_DIM).transpose(0, 2, 1, 3).astype(dt)
        return jnp.pad(a, ((0, 0), (0, 0), (0, SUBLANES - t_new), (0, 0)))

    sample_att = (page_table, per_head(qs_s, BF16), per_head(k_s, BF16), per_head(v_s, BF16), per_head(sga_s, F32),
                  cache_k.transpose(0, 2, 3, 1), cache_v.transpose(0, 2, 3, 1), t_new)

    qs, kt_p, vt_p, kb, vbt, sga, u_p, sgc, kmean, matt_s = _ab_in(
        x_prompt.reshape(b * s, d), norm_0[None], w_in_0b, qg, kg, gsum, ROW_TILE, seq_len=s, sample=sample_att)
    heads_t = lambda a: a.reshape(b, ATT_HEADS, ATT_HEAD_DIM, s).transpose(0, 3, 1, 2)
    r3 = lambda a: a.reshape(b, s, a.shape[-1])
    matt = _moba_prompt(r3(qs), r3(kb), vbt, kmean.reshape(b, s // MOBA_BLOCK, ATT_WIDTH), r3(sga))
    y0 = _ab_out_prompt(x_prompt, matt, r3(u_p), r3(sgc), *conv_args, ROW_TILE)
    n_split = w_in_1.shape[1] // (HGRN_HEADS * HGRN_DK)
    w_heads = (w_in_1.reshape(d, n_split, HGRN_HEADS, HGRN_DK).transpose(2, 0, 1, 3)
               .reshape(HGRN_HEADS, d, n_split * HGRN_DK).astype(BF16))
    y_prompt, hgrn_prompt = _c_layer_prompt(y0, norm_1[None], w_heads, lb_logits, o_norm_1[None], w_out_1b, SCAN_TILE)

    matt_s = matt_s[:, :, :t_new, :].transpose(0, 2, 1, 3)
    y0_s = _ab_out_sample(x_sample.reshape(n_tok, d), matt_s.reshape(n_tok, ATT_WIDTH), q3(u_s), state_conv,
                          sgc_s.astype(F32), *conv_args)
    qq_s, kk_s, lf_s, vv_s, sg_s = _c_in(y0_s, norm_1[None], w_in_1b, lb_logits, n_tok)
    g8 = lambda a: a.reshape(n_tok // SUBLANES, SUBLANES, a.shape[-1])
    y_s, hgrn_sample = _c_scan_sample(g8(qq_s), g8(kk_s), g8(lf_s), g8(vv_s), g8(sg_s), g8(y0_s), state_hgrn,
                                      o_norm_1[None], w_out_1b, t_new)

    conv_prompt = r3(u_p)[:, s - hist:, :]
    conv_sample = jnp.concatenate([state_conv[:, t_new:, :], q3(u_s)], axis=1)
    return (y_prompt, y_s.reshape(n_seq, t_new, d),
            heads_t(kt_p), heads_t(vt_p), heads(k_s, (n_seq, t_new)), heads(v_s, (n_seq, t_new)),
            conv_prompt, conv_sample, hgrn_prompt, hgrn_sample)
```

```python
import functools

import jax
import jax.numpy as jnp
from jax import lax
from jax.experimental import pallas as pl
from jax.experimental.pallas import tpu as pltpu

EPS = 1e-6
ATT_HEADS = 8
ATT_HEAD_DIM = 64
ATT_WIDTH = ATT_HEADS * ATT_HEAD_DIM
MOBA_BLOCK = 256
MOBA_TOPK = 3
MOBA_PAIR = 2
PAGE_SLOTS = 3
PAGE_SIZE = 128
CONV_CH = 512
CONV_LEN = 31
HGRN_HEADS = 8
HGRN_DK = 128
HGRN_DV = 128
LAYER_C = 1

LANES = 128
SUBLANES = 8
LOG2_E = 1.4426950408889634
HEADS_PER_VREG = LANES // ATT_HEAD_DIM
NEG = -1e30
VMEM_LIMIT = 56 * 1024 * 1024
ROW_TILE = 512
SCAN_TILE = 512
SCAN_GROUPS = 4

F32 = jnp.float32
BF16 = jnp.bfloat16
NT_DIMS = (((1,), (1,)), ((), ()))
TN_DIMS = (((0,), (0,)), ((), ()))


def _silu(x):
    return x * jax.nn.sigmoid(x)


def _cparams(n_axes):
    return pltpu.CompilerParams(dimension_semantics=("arbitrary",) * n_axes,
                                vmem_limit_bytes=VMEM_LIMIT)


def _ab_in_stages(h_ref, w_ref, qg_ref, kg_ref, gsum_ref,
                  qs_ref, k_ref, v_ref, kb_ref, vb_ref, sga_ref, u_ref, sgc_ref, kmean_ref, kv_transposed):
    def proj(c):
        return jnp.dot(h_ref[...], w_ref[:, c * ATT_WIDTH:(c + 1) * ATT_WIDTH], preferred_element_type=F32)

    def head_rms(z, g):
        zz = (z * z).astype(BF16)
        half = ATT_WIDTH // 2
        ms_h = jnp.concatenate(
            [jnp.dot(zz[:, :half], gsum_ref[...], preferred_element_type=F32),
             jnp.dot(zz[:, half:], gsum_ref[...], preferred_element_type=F32)], axis=1)
        return z * lax.rsqrt(ms_h + EPS) * g

    def stage_q():
        q = head_rms(proj(0), qg_ref[...])
        qs_ref[...] = (q * (ATT_HEAD_DIM ** -0.5 * LOG2_E)).astype(BF16)

    def stage_k():
        k = head_rms(proj(1), kg_ref[...])
        if kv_transposed:
            k_ref[0] = k.T
        else:
            k_ref[...] = k
        kb_ref[...] = k.astype(BF16)
        tm = k.shape[0]
        kmean_ref[0] = jnp.mean(k.reshape(tm // MOBA_BLOCK, MOBA_BLOCK, ATT_WIDTH), axis=1)

    def stage_v():
        v = proj(2)
        if kv_transposed:
            v_t = v.T
            v_ref[0] = v_t
            vb_ref[0] = v_t.astype(BF16)
        else:
            v_ref[...] = v
            vb_ref[...] = v.astype(BF16)

    def stage_gate_att():
        sga_ref[...] = _silu(proj(3)).astype(BF16)

    def stage_glu():
        u_ref[...] = proj(4) * jax.nn.sigmoid(proj(5))

    def stage_gate_conv():
        sgc_ref[...] = _silu(proj(6)).astype(BF16)

    return [stage_q, stage_k, stage_v, stage_gate_att, stage_glu, stage_gate_conv]


def _normalise_rows(x_ref, ng_ref, h_ref):
    x = x_ref[...]
    ms = jnp.mean(x * x, axis=-1, keepdims=True)
    h_ref[...] = (x * lax.rsqrt(ms + EPS) * ng_ref[...]).astype(BF16)


def _ab_in_kernel(x_ref, ng_ref, w_ref, qg_ref, kg_ref, gsum_ref, *rest, kv_transposed):
    *outs, h_ref = rest
    _normalise_rows(x_ref, ng_ref, h_ref)
    for stage in _ab_in_stages(h_ref, w_ref, qg_ref, kg_ref, gsum_ref, *outs, kv_transposed):
        stage()


def _ab_in(x2d, ng, w_bf, qg, kg, gsum, tm, seq_len=None, sample=None):
    t, d = x2d.shape
    wn = w_bf.shape[1]
    steps = t // tm
    row = lambda i, *_: (i, 0)
    const = lambda i, *_: (0, 0)
    tile = lambda dt: jax.ShapeDtypeStruct((t, ATT_WIDTH), dt)
    tspec = pl.BlockSpec((tm, ATT_WIDTH), row)
    if seq_len is None:
        kv_shape, kv_spec = tile, tspec
    else:
        per_seq = seq_len // tm
        kv_shape = lambda dt: jax.ShapeDtypeStruct((t // seq_len, ATT_WIDTH, seq_len), dt)
        kv_spec = pl.BlockSpec((1, ATT_WIDTH, tm), lambda i, *_: (i // per_seq, 0, i % per_seq))
    out_shape = (tile(BF16), kv_shape(F32), kv_shape(F32), tile(BF16), kv_shape(BF16), tile(BF16), tile(F32),
                 tile(BF16), jax.ShapeDtypeStruct((steps, tm // MOBA_BLOCK, ATT_WIDTH), F32))
    in_specs = [pl.BlockSpec((tm, d), row), pl.BlockSpec((1, d), const),
                pl.BlockSpec((d, wn), const, pipeline_mode=pl.Buffered(1)),
                pl.BlockSpec((1, ATT_WIDTH), const), pl.BlockSpec((1, ATT_WIDTH), const),
                pl.BlockSpec(gsum.shape, const)]
    out_specs = ((tspec, kv_spec, kv_spec, tspec, kv_spec) + (tspec,) * 3
                 + (pl.BlockSpec((1, tm // MOBA_BLOCK, ATT_WIDTH), lambda i, *_: (i, 0, 0)),))
    scratch = [pltpu.VMEM((tm, d), BF16)]
    kv_t = seq_len is not None
    if sample is None:
        return pl.pallas_call(
            functools.partial(_ab_in_kernel, kv_transposed=kv_t),
            grid=(steps,), in_specs=in_specs, out_specs=out_specs, out_shape=out_shape,
            scratch_shapes=scratch, compiler_params=_cparams(1), name="ab_in",
        )(x2d, ng, w_bf, qg, kg, gsum)

    page_table, sq, skn, svn, ssga, cache_kt, cache_vt, t_new = sample
    n_seq, _, rows, _ = sq.shape
    n_pages = page_table.shape[1]
    past = n_pages * PAGE_SIZE
    per_step = n_seq // steps
    assert past % MOBA_BLOCK == 0, "the sample group's past must end on a MoBA block boundary"
    assert per_step * steps == n_seq and n_seq >= PAGE_SLOTS
    sspec = pl.BlockSpec((per_step, ATT_HEADS, rows, ATT_HEAD_DIM), lambda i, *_: (i, 0, 0, 0))
    page_buf = pltpu.VMEM((PAGE_SLOTS, ATT_HEADS, ATT_HEAD_DIM, past), F32)
    grid_spec = pltpu.PrefetchScalarGridSpec(
        num_scalar_prefetch=1,
        grid=(steps,),
        in_specs=in_specs + [sspec] * 4 + [pl.BlockSpec(memory_space=pl.ANY)] * 2,
        out_specs=out_specs + (sspec,),
        scratch_shapes=scratch + [page_buf, page_buf, pltpu.SemaphoreType.DMA((2, PAGE_SLOTS))],
    )
    return pl.pallas_call(
        functools.partial(_ab_in_sample_kernel, kv_transposed=kv_t, n_pages=n_pages, t_new=t_new,
                          per_step=per_step),
        grid_spec=grid_spec,
        out_shape=out_shape + (jax.ShapeDtypeStruct(sq.shape, F32),),
        compiler_params=_cparams(1),
        name="ab_in_sample",
    )(page_table, x2d, ng, w_bf, qg, kg, gsum, sq, skn, svn, ssga, cache_kt, cache_vt)


def _moba_prompt_kernel(qs_ref, kb_ref, vt_ref, kmean_ref, sga_ref, o_ref, bias_ref, s_ref, e_ref, *, nb):
    pair = pl.program_id(2)
    blk = MOBA_BLOCK
    cols = HEADS_PER_VREG * blk
    km_parts = _split3(kmean_ref[0])
    lane = lax.broadcasted_iota(jnp.int32, (blk, LANES), 1)
    key_i = lax.broadcasted_iota(jnp.int32, (blk, cols), 0)
    qry_i = lax.broadcasted_iota(jnp.int32, (blk, cols), 1) % blk
    causal_bias = jnp.where(key_i <= qry_i, 0.0, NEG).astype(F32)
    b_i = lax.broadcasted_iota(jnp.int32, (nb, blk), 0)

    def block_kv(jb):
        rows = pl.ds(pl.multiple_of(jb * blk, blk), blk)
        return kb_ref[0, rows, :], vt_ref[0, :, rows]

    def prepare(c):
        j = MOBA_PAIR * pair + c
        q2 = qs_ref[0, c * blk:(c + 1) * blk, :]
        valid = b_i < j
        qhs = []
        for hh in range(HEADS_PER_VREG):
            head_mask = jnp.where((lane // ATT_HEAD_DIM) == hh, 1.0, 0.0).astype(BF16)
            qh = q2 * head_mask
            g = sum(lax.dot_general(part, qh, NT_DIMS, preferred_element_type=F32) for part in km_parts)
            g = jnp.where(valid, g, -jnp.inf)
            rank = jnp.zeros((nb, blk), F32)
            for i in range(nb):
                gi = g[i:i + 1, :]
                rank = rank + jnp.where(gi > g, 1.0, jnp.where((gi == g) & (b_i > i), 1.0, 0.0))
            sel_bias = jnp.where(valid & (rank < MOBA_TOPK), 0.0, NEG)
            bias_ref[c, :, :, hh * blk:(hh + 1) * blk] = jnp.broadcast_to(
                sel_bias[:, None, :], (nb, SUBLANES, blk))
            qhs.append(qh)
        return jnp.concatenate(qhs, axis=0)

    def block_bias(c, jb):
        return jnp.broadcast_to(bias_ref[c, jb][None], (blk // SUBLANES, SUBLANES, cols)).reshape(blk, cols)

    def scores(q_cols, k_rows):
        return lax.dot_general(k_rows, q_cols, NT_DIMS, preferred_element_type=F32)

    def update(carry, s, v_cols):
        if carry is None:
            m_new = jnp.max(s, axis=0, keepdims=True)
            p = jnp.exp2(s - m_new)
            return (m_new, jnp.sum(p, axis=0, keepdims=True),
                    jnp.dot(v_cols, p.astype(BF16), preferred_element_type=F32))
        m, l, acc = carry
        m_new = jnp.maximum(m, jnp.max(s, axis=0, keepdims=True))
        alpha = jnp.exp2(m - m_new)
        p = jnp.exp2(s - m_new)
        l = alpha * l + jnp.sum(p, axis=0, keepdims=True)
        acc = alpha * acc + jnp.dot(v_cols, p.astype(BF16), preferred_element_type=F32)
        return m_new, l, acc

    span = MOBA_PAIR * blk
    last = nb // MOBA_PAIR - 1

    def step_keys(i):
        return pl.ds(pl.multiple_of(jnp.minimum(i, last) * span, span), span)

    def issue(i, buf):
        kj = kb_ref[0, step_keys(i), :]
        for c in range(MOBA_PAIR):
            s_ref[buf, c] = scores(q_cols[c], kj)

    def consume(i, buf, carry):
        vj = vt_ref[0, :, step_keys(i)]
        dead = jnp.where(i < pair, 0.0, NEG)
        out = []
        for c in range(MOBA_PAIR):
            bias = jnp.concatenate([block_bias(c, jnp.minimum(i, last) * MOBA_PAIR + u)
                                    for u in range(MOBA_PAIR)], axis=0)
            out.append(update(carry[c], s_ref[buf, c] + (bias + dead), vj))
        return tuple(out)

    def body(t, carry):
        i0 = 2 * t
        issue(i0 + 1, 1)
        carry = consume(i0, 0, carry)
        issue(i0 + 2, 0)
        return consume(i0 + 1, 1, carry)

    q_cols = [prepare(c) for c in range(MOBA_PAIR)]
    own_kv = [block_kv(MOBA_PAIR * pair + c) for c in range(MOBA_PAIR)]
    cross = [(c, e) for c in range(1, MOBA_PAIR) for e in range(c)]
    for c in range(MOBA_PAIR):
        e_ref[c] = scores(q_cols[c], own_kv[c][0])
    for n, (c, e) in enumerate(cross):
        e_ref[MOBA_PAIR + n] = scores(q_cols[c], own_kv[e][0])
    issue(0, 0)
    state = [update(None, e_ref[c] + causal_bias, own_kv[c][1]) for c in range(MOBA_PAIR)]
    for n, (c, e) in enumerate(cross):
        state[c] = update(state[c], e_ref[MOBA_PAIR + n] + block_bias(c, MOBA_PAIR * pair + e), own_kv[e][1])
    state = lax.fori_loop(0, (pair + 1) // 2, body, tuple(state))
    for c in range(MOBA_PAIR):
        _, l, acc = state[c]
        o_t = acc / l
        o_t = jnp.concatenate([o_t[hh * ATT_HEAD_DIM:(hh + 1) * ATT_HEAD_DIM, hh * blk:(hh + 1) * blk]
                               for hh in range(HEADS_PER_VREG)], axis=0)
        rows = slice(c * blk, (c + 1) * blk)
        o_ref[0, rows, :] = (o_t.T * sga_ref[0, rows, :].astype(F32)).astype(BF16)


def _moba_prompt(qs, kb, vt, kmean, sga):
    b, s, _ = qs.shape
    nb = s // MOBA_BLOCK
    assert nb % MOBA_PAIR == 0
    npair = ATT_WIDTH // LANES
    rows = MOBA_PAIR * MOBA_BLOCK
    qmap = lambda bi, hp, j: (bi, j, hp)
    kvmap = lambda bi, hp, j: (bi, 0, hp)
    return pl.pallas_call(
        functools.partial(_moba_prompt_kernel, nb=nb),
        grid=(b, npair, nb // MOBA_PAIR),
        in_specs=[pl.BlockSpec((1, rows, LANES), qmap),
                  pl.BlockSpec((1, s, LANES), kvmap),
                  pl.BlockSpec((1, LANES, s), lambda bi, hp, j: (bi, hp, 0)),
                  pl.BlockSpec((1, nb, LANES), kvmap),
                  pl.BlockSpec((1, rows, LANES), qmap)],
        out_specs=pl.BlockSpec((1, rows, LANES), qmap),
        out_shape=jax.ShapeDtypeStruct((b, s, ATT_WIDTH), BF16),
        scratch_shapes=[pltpu.VMEM((MOBA_PAIR, nb, SUBLANES, HEADS_PER_VREG * MOBA_BLOCK), F32),
                        pltpu.VMEM((2, MOBA_PAIR, rows, HEADS_PER_VREG * MOBA_BLOCK), F32),
                        pltpu.VMEM((MOBA_PAIR * (MOBA_PAIR + 1) // 2, MOBA_BLOCK, HEADS_PER_VREG * MOBA_BLOCK), F32)],
        compiler_params=_cparams(3),
        name="moba_prompt",
    )(qs, kb, vt, kmean, sga)


CONV_HALO = 32
CONV_ROWS = 64


CONV_LEAD = CONV_HALO - (CONV_LEN - 1)
CONV_SPAN = -(-(CONV_LEAD + SUBLANES - 1 + CONV_LEN) // SUBLANES)


def _conv_tap_table(conv_w):
    m = jnp.arange(CONV_SPAN)[:, None, None]
    s = jnp.arange(SUBLANES)[None, :, None]
    r = jnp.arange(SUBLANES)[None, None, :]
    idx = SUBLANES * m + s - r - CONV_LEAD
    ok = (idx >= 0) & (idx < CONV_LEN)
    tab = jnp.where(ok[..., None], conv_w[jnp.clip(idx, 0, CONV_LEN - 1)], 0.0)
    return tab.reshape(CONV_SPAN * SUBLANES, SUBLANES, conv_w.shape[-1])


def _tap_used(ms):
    lo = ms - (SUBLANES - 1) - CONV_LEAD
    return lo + SUBLANES - 1 >= 0 and lo < CONV_LEN


def _conv_rows(xpad_ref, ybuf_ref, tab_ref, cb_ref, n_rows):
    groups = CONV_ROWS // SUBLANES
    for lt in range(CONV_CH // LANES):
        ls = slice(lt * LANES, (lt + 1) * LANES)

        for c in range(n_rows // CONV_ROWS):
            r0 = c * CONV_ROWS
            win_ref = xpad_ref.at[pl.ds(r0, CONV_ROWS + CONV_SPAN * SUBLANES)]
            taps = {ms: tab_ref[ms, :, ls] for ms in range(CONV_SPAN * SUBLANES) if _tap_used(ms)}
            acc = [jnp.broadcast_to(cb_ref[:, ls], (SUBLANES, LANES))] * groups
            for rho in range(CONV_ROWS + CONV_SPAN * SUBLANES):
                users = [(rho // SUBLANES - m, SUBLANES * m + rho % SUBLANES) for m in range(CONV_SPAN)]
                users = [(g, ms) for g, ms in users if 0 <= g < groups and ms in taps]
                if not users:
                    continue
                xb = win_ref[pl.ds(rho, 1), ls]
                for g, ms in users:
                    acc[g] = acc[g] + taps[ms] * xb
            ybuf_ref[pl.ds(r0, CONV_ROWS), ls] = jnp.concatenate(acc, axis=0)


def _ln_silu_gate(y, lg_ref, lb_ref, sgc):
    mu = jnp.mean(y, axis=-1, keepdims=True)
    cen = y - mu
    var = jnp.mean(cen * cen, axis=-1, keepdims=True)
    yn = cen * lax.rsqrt(var + EPS) * lg_ref[...] + lb_ref[...]
    return (_silu(yn) * sgc.astype(F32)).astype(BF16)


def _ab_out_prompt_kernel(x_ref, matt_ref, u_ref, halo_ref, sgc_ref, tab_ref, cb_ref, lg_ref, lb_ref, w_ref,
                          y_ref, xpad_ref, ybuf_ref, m_ref):
    t = pl.program_id(1)
    tm = u_ref.shape[1]
    halo = halo_ref[0]
    xpad_ref[0:CONV_HALO, :] = jnp.where(t == 0, jnp.zeros_like(halo), halo)
    xpad_ref[CONV_HALO:CONV_HALO + tm, :] = u_ref[0]
    y_att = x_ref[0] + jnp.dot(matt_ref[0], w_ref[0:ATT_WIDTH, :], preferred_element_type=F32)
    _conv_rows(xpad_ref, ybuf_ref, tab_ref, cb_ref, tm)
    for c in range(tm // CONV_ROWS):
        rows = slice(c * CONV_ROWS, (c + 1) * CONV_ROWS)
        m_ref[rows, :] = _ln_silu_gate(ybuf_ref[rows, :], lg_ref, lb_ref, sgc_ref[0, rows, :])
    y_ref[0] = y_att + jnp.dot(m_ref[...], w_ref[ATT_WIDTH:, :], preferred_element_type=F32)


def _ab_out_prompt(x, matt, u, sgc, tab, cb, lg, lb, w_bf, tm):
    b, s, d = x.shape
    hpt = tm // CONV_HALO
    tmap = lambda bi, t: (bi, t, 0)
    hmap = lambda bi, t: (bi, jnp.maximum(t * hpt - 1, 0), 0)
    const = lambda bi, t: (0, 0)
    return pl.pallas_call(
        _ab_out_prompt_kernel,
        grid=(b, s // tm),
        in_specs=[pl.BlockSpec((1, tm, d), tmap), pl.BlockSpec((1, tm, ATT_WIDTH), tmap),
                  pl.BlockSpec((1, tm, CONV_CH), tmap), pl.BlockSpec((1, CONV_HALO, CONV_CH), hmap),
                  pl.BlockSpec((1, tm, CONV_CH), tmap),
                  pl.BlockSpec(tab.shape, lambda bi, t: (0, 0, 0)), pl.BlockSpec((1, CONV_CH), const),
                  pl.BlockSpec((1, CONV_CH), const), pl.BlockSpec((1, CONV_CH), const),
                  pl.BlockSpec(w_bf.shape, const)],
        out_specs=pl.BlockSpec((1, tm, d), tmap),
        out_shape=jax.ShapeDtypeStruct((b, s, d), F32),
        scratch_shapes=[pltpu.VMEM((tm + CONV_SPAN * SUBLANES, CONV_CH), F32),
                        pltpu.VMEM((tm, CONV_CH), F32),
                        pltpu.VMEM((tm, CONV_CH), BF16)],
        compiler_params=_cparams(2),
        name="ab_out_prompt",
    )(x, matt, u, u, sgc, tab, cb, lg, lb, w_bf)


C_COLS = 512


def _c_in_kernel(x_ref, ng_ref, w_ref, lbl_ref, qq_ref, kk_ref, lf_ref, vv_ref, sg_ref):
    x = x_ref[...]
    ms = jnp.mean(x * x, axis=-1, keepdims=True)
    h = (x * lax.rsqrt(ms + EPS) * ng_ref[...]).astype(BF16)
    p = jax.nn.softmax(lbl_ref[...], axis=0)
    lb = jnp.sum(p[0:LAYER_C + 1], axis=0, keepdims=True) - p[0:1]
    key_w = qq_ref.shape[1]
    per = key_w // C_COLS
    for c in range(per):
        cs = slice(c * C_COLS, (c + 1) * C_COLS)

        def proj(g, cs=cs):
            return jnp.dot(h, w_ref[:, g * key_w + cs.start:g * key_w + cs.stop], preferred_element_type=F32)

        qq_ref[:, cs] = _silu(proj(0)).astype(BF16)
        lbc = lb[:, cs]
        f = lbc + (1.0 - lbc) * jax.nn.sigmoid(proj(1))
        lf_ref[:, cs] = jnp.log(f)
        kk_ref[:, cs] = (1.0 - f).astype(BF16)
        vv_ref[:, cs] = proj(2).astype(BF16)
        sg_ref[:, cs] = _silu(proj(3)).astype(BF16)


def _c_in(x2d, ng, w_bf, lbl, tm):
    t, d = x2d.shape
    key_w = lbl.shape[1]
    row = lambda i: (i, 0)
    const = lambda i: (0, 0)
    tile = lambda dt: jax.ShapeDtypeStruct((t, key_w), dt)
    tspec = pl.BlockSpec((tm, key_w), row)
    return pl.pallas_call(
        _c_in_kernel,
        grid=(t // tm,),
        in_specs=[pl.BlockSpec((tm, d), row), pl.BlockSpec((1, d), const), pl.BlockSpec(w_bf.shape, const),
                  pl.BlockSpec(lbl.shape, const)],
        out_specs=(tspec,) * 5,
        out_shape=(tile(BF16), tile(BF16), tile(F32), tile(BF16), tile(BF16)),
        compiler_params=_cparams(1),
        name="c_in",
    )(x2d, ng, w_bf, lbl)


HGRN_SUB = SUBLANES


def _split3(x):
    hi = x.astype(BF16)
    r1 = x - hi.astype(F32)
    mid = r1.astype(BF16)
    lo = (r1 - mid.astype(F32)).astype(BF16)
    return hi, mid, lo


def _chunk_masks(c):
    t_i = lax.broadcasted_iota(jnp.int32, (c, c), 0)
    s_i = lax.broadcasted_iota(jnp.int32, (c, c), 1)
    same = (t_i // HGRN_SUB) == (s_i // HGRN_SUB)
    diag = [t_i - s_i == d for d in range(HGRN_SUB)]
    col_group = [(s_i // HGRN_SUB) == a for a in range(c // HGRN_SUB)]
    return diag, col_group, same & (s_i <= t_i)


def _hgrn_pairs(q, k, b, f, o_inter_fn):
    c = q.shape[0]
    n_sub = c // HGRN_SUB
    b_last = b[c - 1:c, :]
    o_inter = o_inter_fn((q * jnp.exp2(b)).astype(BF16))
    kt = (k * jnp.exp2(b_last - b)).astype(BF16)

    k_b = k.astype(BF16)
    f3 = f.reshape(n_sub, HGRN_SUB, f.shape[1])
    stack = [q.astype(BF16)]
    prod = f3
    for d in range(1, HGRN_SUB):
        if d > 1:
            prod = prod * pltpu.roll(f3, d - 1, 1)
        stack.append((q * prod.reshape(f.shape)).astype(BF16))
    zd = lax.dot_general(jnp.concatenate(stack, axis=0), k_b, NT_DIMS, preferred_element_type=F32)

    zo = None
    if n_sub > 1:
        b3 = b.reshape(n_sub, HGRN_SUB, b.shape[1])
        b_end = jnp.broadcast_to(b3[:, HGRN_SUB - 1:HGRN_SUB, :], b3.shape).reshape(b.shape)
        ksc = (k * jnp.exp2(b_end - b)).astype(BF16)
        stack = []
        for g in range(1, n_sub):
            r = g * HGRN_SUB
            stack.append((q[r:, :] * jnp.exp2(b[r:, :] - b[r - 1:r, :])).astype(BF16))
        zo = lax.dot_general(jnp.concatenate(stack, axis=0), ksc, NT_DIMS, preferred_element_type=F32)
    return (o_inter, zd, zo), kt, jnp.exp2(b_last)


def _hgrn_combine(parts, v_b, masks):
    o_inter, zd, zo = parts
    c = o_inter.shape[0]
    n_sub = c // HGRN_SUB
    diag, col_group, same_lower = masks
    a = jnp.zeros((c, c), F32)
    row = 0
    for g in range(1, n_sub):
        r = g * HGRN_SUB
        part = jnp.concatenate([jnp.zeros((r, c), F32), zo[row:row + c - r, :]], axis=0)
        a = jnp.where(col_group[g - 1], part, a)
        row += c - r
    a_in = zd[0:c, :]
    for d in range(1, HGRN_SUB):
        a_in = jnp.where(diag[d], zd[d * c:(d + 1) * c, :], a_in)
    a = jnp.where(same_lower, a_in, a)
    return o_inter + jnp.dot(a.astype(BF16), v_b, preferred_element_type=F32)


def _head_norm_gate(o, og, sg):
    ms = jnp.mean(o * o, axis=-1, keepdims=True)
    return ((o * lax.rsqrt(ms + EPS) * og) * sg.astype(F32)).astype(BF16)


HGRN_CHUNK = 64


def _c_layer_prompt_kernel(xn_ref, y0_ref, ng_ref, w_ref, lbl_ref, og_ref, wo_ref, y_ref, s_ref,
                           st_ref, m_ref, zd_ref, zo_ref, hn_ref, lb_ref, wh_ref, qq_s, kk_s, lf_s, vv_s, sg_s,
                           *, tiles_per_seq):
    g = pl.program_id(0)
    t = lax.rem(g, tiles_per_seq)
    cur = lax.rem(g, 2)
    nxt = 1 - cur
    ts = y0_ref.shape[1]
    c = HGRN_CHUNK
    dk, dv = HGRN_DK, HGRN_DV
    assert ts // c == HGRN_HEADS and dk == dv == LANES

    p = jax.nn.softmax(lbl_ref[...], axis=0)
    lb = jnp.sum(p[0:LAYER_C + 1], axis=0, keepdims=True) - p[0:1]
    for h in range(HGRN_HEADS):
        lb_ref[h] = lb[:, h * dk:(h + 1) * dk]

    def project_qf(hd, buf):
        z = jnp.dot(hn_ref[...], wh_ref[hd, :, 0:2 * dk], preferred_element_type=F32)
        qq_s[buf, hd] = _silu(z[:, 0:dk]).astype(BF16)
        lbh = lb_ref[hd]
        f = lbh + (1.0 - lbh) * jax.nn.sigmoid(z[:, dk:])
        lf_s[buf, hd] = jnp.log(f)
        kk_s[buf, hd] = (1.0 - f).astype(BF16)

    def project_vg(hd, buf):
        z = jnp.dot(hn_ref[...], wh_ref[hd, :, 2 * dk:], preferred_element_type=F32)
        vv_s[buf, hd] = z[:, 0:dv].astype(BF16)
        sg_s[buf, hd] = _silu(z[:, dv:]).astype(BF16)

    def project_head(hd, buf):
        project_qf(hd, buf)
        project_vg(hd, buf)

    def normalise(x):
        ms = jnp.mean(x * x, axis=-1, keepdims=True)
        hn_ref[...] = (x * lax.rsqrt(ms + EPS) * ng_ref[...]).astype(BF16)

    @pl.when(g == 0)
    def _():
        key_w = HGRN_HEADS * dk
        for h in range(HGRN_HEADS):
            for part in range(w_ref.shape[1] // key_w):
                wh_ref[h, :, part * dk:(part + 1) * dk] = w_ref[:, part * key_w + h * dk:part * key_w + (h + 1) * dk]
        normalise(y0_ref[0])
        for h in range(HGRN_HEADS):
            project_head(h, 0)

    @pl.when(t == 0)
    def _():
        st_ref[...] = jnp.zeros_like(st_ref)

    normalise(xn_ref[0])

    def chunk(ci, carry):
        r0 = pl.multiple_of(ci * c, c)
        rows = pl.ds(r0, c)
        heads = range(HGRN_HEADS)
        q_c = [qq_s[cur, h, rows, :] for h in heads]
        k_c = [kk_s[cur, h, rows, :] for h in heads]
        v_c = [vv_s[cur, h, rows, :] for h in heads]
        sg_c = [sg_s[cur, h, rows, :] for h in heads]
        lf2 = jnp.concatenate([lf_s[cur, h, rows, :] for h in heads], axis=1) * LOG2_E
        project_qf(ci, nxt)
        masks = _chunk_masks(c)
        r_i = lax.broadcasted_iota(jnp.int32, (c, c), 0)
        c_i = lax.broadcasted_iota(jnp.int32, (c, c), 1)
        tri = jnp.where(c_i <= r_i, 1.0, 0.0).astype(BF16)
        f_all = jnp.exp2(lf2)
        b_all = sum(jnp.dot(tri, part, preferred_element_type=F32) for part in _split3(lf2))
        parts = []
        for h in range(HGRN_HEADS):
            ks = slice(h * dk, (h + 1) * dk)
            st = st_ref[h]
            st_b = st.astype(BF16)
            (o_inter, zd, zo), kt, decay = _hgrn_pairs(
                q_c[h].astype(F32), k_c[h].astype(F32), b_all[:, ks], f_all[:, ks],
                lambda qt, st_b=st_b: lax.dot_general(qt, st_b, NT_DIMS, preferred_element_type=F32))
            zd_ref[h] = zd
            zo_ref[h] = zo
            parts.append(o_inter)
            st_ref[h] = st * decay + lax.dot_general(v_c[h], kt, TN_DIMS, preferred_element_type=F32)
        project_vg(ci, nxt)
        for h in range(HGRN_HEADS):
            o = _hgrn_combine((parts[h], zd_ref[h], zo_ref[h]), v_c[h], masks)
            m_ref[rows, h * dv:(h + 1) * dv] = _head_norm_gate(o, og_ref[...], sg_c[h])
        return carry

    lax.fori_loop(0, ts // c, chunk, 0)
    y_ref[0] = y0_ref[0] + jnp.dot(m_ref[...], wo_ref[...], preferred_element_type=F32)

    @pl.when(t == tiles_per_seq - 1)
    def _():
        for h in range(HGRN_HEADS):
            s_ref[0, h] = st_ref[h].T


def _c_layer_prompt(y0, ng, w_bf, lbl, og, wo_bf, ts):
    b, s, d = y0.shape
    per_head = w_bf.shape[1] // HGRN_HEADS
    tps = s // ts
    n_tiles = b * tps
    n_sub = HGRN_CHUNK // HGRN_SUB
    cur_map = lambda g: (g // tps, g % tps, 0)
    nxt_map = lambda g: (jnp.minimum(g + 1, n_tiles - 1) // tps, jnp.minimum(g + 1, n_tiles - 1) % tps, 0)
    const = lambda g: (0, 0)
    head_buf = lambda dt: pltpu.VMEM((2, HGRN_HEADS, ts, LANES), dt)
    return pl.pallas_call(
        functools.partial(_c_layer_prompt_kernel, tiles_per_seq=tps),
        grid=(n_tiles,),
        in_specs=[pl.BlockSpec((1, ts, d), nxt_map), pl.BlockSpec((1, ts, d), cur_map),
                  pl.BlockSpec((1, d), const),
                  pl.BlockSpec(w_bf.shape, const, pipeline_mode=pl.Buffered(1)),
                  pl.BlockSpec(lbl.shape, const), pl.BlockSpec((1, HGRN_DV), const),
                  pl.BlockSpec(wo_bf.shape, const)],
        out_specs=(pl.BlockSpec((1, ts, d), cur_map),
                   pl.BlockSpec((1, HGRN_HEADS, HGRN_DK, HGRN_DV), lambda g: (g // tps, 0, 0, 0))),
        out_shape=(jax.ShapeDtypeStruct((b, s, d), F32),
                   jax.ShapeDtypeStruct((b, HGRN_HEADS, HGRN_DK, HGRN_DV), F32)),
        scratch_shapes=[pltpu.VMEM((HGRN_HEADS, HGRN_DV, HGRN_DK), F32),
                        pltpu.VMEM((ts, HGRN_HEADS * HGRN_DV), BF16),
                        pltpu.VMEM((HGRN_HEADS, HGRN_SUB * HGRN_CHUNK, HGRN_CHUNK), F32),
                        pltpu.VMEM((HGRN_HEADS, (n_sub * (n_sub - 1) // 2) * HGRN_SUB, HGRN_CHUNK), F32),
                        pltpu.VMEM((ts, d), BF16),
                        pltpu.VMEM((HGRN_HEADS, 1, LANES), F32),
                        pltpu.VMEM((HGRN_HEADS, d, per_head), BF16),
                        head_buf(BF16), head_buf(BF16), head_buf(F32), head_buf(BF16), head_buf(BF16)],
        compiler_params=_cparams(1),
        name="c_layer_prompt",
    )(y0, y0, ng, w_bf, lbl, og, wo_bf)


def _page_copies(pt_ref, ck_hbm, cv_hbm, kbuf, vbuf, sem, seq, sl, n_pages):
    out = []
    for p in range(n_pages):
        page = pt_ref[seq, p]
        toks = pl.ds(p * PAGE_SIZE, PAGE_SIZE)
        out.append(pltpu.make_async_copy(ck_hbm.at[page], kbuf.at[sl, :, :, toks], sem.at[0, sl]))
        out.append(pltpu.make_async_copy(cv_hbm.at[page], vbuf.at[sl, :, :, toks], sem.at[1, sl]))
    return out


def _ab_in_sample_kernel(pt_ref, x_ref, ng_ref, w_ref, qg_ref, kg_ref, gsum_ref,
                         sq_ref, skn_ref, svn_ref, ssga_ref, ck_hbm, cv_hbm, *rest,
                         kv_transposed, n_pages, t_new, per_step):
    *outs, so_ref, h_ref, kbuf, vbuf, sem = rest
    step = pl.program_id(0)
    n_seq = pl.num_programs(0) * per_step
    copies = functools.partial(_page_copies, pt_ref, ck_hbm, cv_hbm, kbuf, vbuf, sem, n_pages=n_pages)

    _normalise_rows(x_ref, ng_ref, h_ref)
    stages = _ab_in_stages(h_ref, w_ref, qg_ref, kg_ref, gsum_ref, *outs, kv_transposed)
    share = -(-len(stages) // per_step)

    ahead = PAGE_SLOTS - 1

    @pl.when(step == 0)
    def _():
        for s0 in range(ahead):
            for cp in copies(seq=s0, sl=s0):
                cp.start()

    for j in range(per_step):
        seq = step * per_step + j
        slot = lax.rem(seq, PAGE_SLOTS)

        @pl.when(seq + ahead < n_seq)
        def _():
            for cp in copies(seq=seq + ahead, sl=lax.rem(seq + ahead, PAGE_SLOTS)):
                cp.start()

        for cp in copies(seq=seq, sl=slot):
            cp.wait()

        def between(j=j):
            for stage in stages[j * share:(j + 1) * share]:
                stage()

        _sample_attention(j, slot, sq_ref, skn_ref, svn_ref, ssga_ref, so_ref, kbuf, vbuf,
                          t_new, n_pages * PAGE_SIZE // MOBA_BLOCK, between)


def _sample_attention(sq, slot, qs_ref, kn_ref, vn_ref, sga_ref, o_ref, kbuf, vbuf, t_new, nb, between):
    rows = qs_ref.shape[2]
    b_i = lax.broadcasted_iota(jnp.int32, (rows, LANES), 1)
    o_row = lax.broadcasted_iota(jnp.int32, (rows, rows), 0)
    o_col = lax.broadcasted_iota(jnp.int32, (rows, rows), 1)
    raw = []
    for h in range(ATT_HEADS):
        q_h = qs_ref[sq, h]
        raw.append((jnp.dot(q_h, kbuf[slot, h].astype(BF16), preferred_element_type=F32),
                    lax.dot_general(q_h, kn_ref[sq, h], NT_DIMS, preferred_element_type=F32)))
    between()
    for h in range(ATT_HEADS):
        s, s_own = raw[h]
        blocks = [s[:, j * MOBA_BLOCK:(j + 1) * MOBA_BLOCK] for j in range(nb)]
        gates = [jnp.sum(blk, axis=1, keepdims=True) for blk in blocks]
        gate = jnp.full((rows, LANES), -jnp.inf, F32)
        for i in range(nb):
            gate = jnp.where(b_i == i, gates[i], gate)
        rank = jnp.zeros((rows, LANES), F32)
        for i in range(nb):
            gi = gates[i]
            rank = rank + jnp.where(gi > gate, 1.0, jnp.where((gi == gate) & (b_i > i), 1.0, 0.0))
        sel_bias = jnp.where(rank < min(MOBA_TOPK, nb), 0.0, NEG)
        s = jnp.concatenate([blocks[j] + sel_bias[:, j:j + 1] for j in range(nb)], axis=1)

        s_own = jnp.where((o_col <= o_row) & (o_col < t_new), s_own, NEG)

        m = jnp.maximum(jnp.max(s, axis=1, keepdims=True), jnp.max(s_own, axis=1, keepdims=True))
        p = jnp.exp2(s - m)
        p_own = jnp.exp2(s_own - m)
        l = jnp.sum(p, axis=1, keepdims=True) + jnp.sum(p_own, axis=1, keepdims=True)
        o = (lax.dot_general(p.astype(BF16), vbuf[slot, h].astype(BF16), NT_DIMS, preferred_element_type=F32)
             + jnp.dot(p_own.astype(BF16), vn_ref[sq, h], preferred_element_type=F32)) / l
        o_ref[sq, h] = o * sga_ref[sq, h]


def _ab_out_sample_kernel(x_ref, matt_ref, u_ref, buf_ref, sgc_ref, tab_ref, cb_ref, lg_ref, lb_ref, w_ref,
                          y_ref, xpad_ref, m_ref):
    n_seq, t_new, _ = u_ref.shape
    hist = buf_ref.shape[1]
    per = SUBLANES // t_new
    assert per * t_new == SUBLANES and n_seq % per == 0 and hist == CONV_LEN - 1
    span = CONV_SPAN * SUBLANES
    xpad_ref[...] = jnp.zeros_like(xpad_ref)
    row8 = lax.broadcasted_iota(jnp.int32, (SUBLANES, CONV_CH), 0)

    def group(gi, carry):
        y8 = jnp.zeros((SUBLANES, CONV_CH), F32)
        for j in range(per):
            sq = gi * per + j
            off = j * t_new
            xpad_ref[j, CONV_HALO - hist + off:CONV_HALO + off, :] = buf_ref[sq]
            xpad_ref[j, CONV_HALO + off:CONV_HALO + off + t_new, :] = u_ref[sq]
            cols = []
            for lt in range(CONV_CH // LANES):
                ls = slice(lt * LANES, (lt + 1) * LANES)
                acc = jnp.broadcast_to(cb_ref[:, ls], (SUBLANES, LANES))
                for ms in range(span):
                    if _tap_used(ms):
                        acc = acc + tab_ref[ms, :, ls] * xpad_ref[j, pl.ds(ms, 1), ls]
                cols.append(acc)
            yj = jnp.concatenate(cols, axis=1)
            y8 = jnp.where((row8 >= off) & (row8 < off + t_new), yj, y8)
        r0 = pl.multiple_of(gi * SUBLANES, SUBLANES)
        rows = pl.ds(r0, SUBLANES)
        m_ref[rows, 0:ATT_WIDTH] = matt_ref[rows, :]
        m_ref[rows, ATT_WIDTH:] = _ln_silu_gate(y8, lg_ref, lb_ref, sgc_ref[rows, :]).astype(F32)
        return carry

    lax.fori_loop(0, n_seq // per, group, 0)
    y_ref[...] = x_ref[...] + jnp.dot(m_ref[...].astype(BF16), w_ref[...], preferred_element_type=F32)


def _ab_out_sample(x2d, matt2d, u3, buf, sgc2d, tab, cb, lg, lb, w_bf):
    t, d = x2d.shape
    n_seq, t_new, _ = u3.shape
    per = SUBLANES // t_new
    vm = lambda: pl.BlockSpec(memory_space=pltpu.VMEM)
    return pl.pallas_call(
        _ab_out_sample_kernel,
        in_specs=[vm() for _ in range(10)],
        out_specs=vm(),
        out_shape=jax.ShapeDtypeStruct((t, d), F32),
        scratch_shapes=[pltpu.VMEM((per, CONV_HALO + 2 * SUBLANES, CONV_CH), F32),
                        pltpu.VMEM((t, ATT_WIDTH + CONV_CH), F32)],
        compiler_params=pltpu.CompilerParams(vmem_limit_bytes=VMEM_LIMIT),
        name="ab_out_sample",
    )(x2d, matt2d, u3, buf, sgc2d, tab, cb, lg, lb, w_bf)


def _c_scan_sample_kernel(qq_ref, kk_ref, lf_ref, vv_ref, sg_ref, y0_ref, s0_ref, og_ref, w_ref,
                          y_ref, s_ref, *, t_new):
    n_groups = qq_ref.shape[0]
    per = SUBLANES // t_new
    c = SUBLANES
    dk, dv = HGRN_DK, HGRN_DV
    masks = _chunk_masks(c)
    r_i = lax.broadcasted_iota(jnp.int32, (c, c), 0)
    c_i = lax.broadcasted_iota(jnp.int32, (c, c), 1)
    tri = jnp.where(c_i <= r_i, 1.0, 0.0).astype(BF16)
    row = lax.broadcasted_iota(jnp.int32, (c, 1), 0)
    pending = []
    for g in range(n_groups):
        lf = lf_ref[g]
        for j in range(per):
            sq = g * per + j
            mine = (row >= j * t_new) & (row < (j + 1) * t_new)
            keep = jnp.where(mine, 1.0, 0.0)
            lf2 = lf * (keep * LOG2_E)
            f_all = jnp.exp2(lf2)
            b_all = sum(jnp.dot(tri, part, preferred_element_type=F32) for part in _split3(lf2))
            for h in range(HGRN_HEADS):
                ks = slice(h * dk, (h + 1) * dk)
                vs = slice(h * dv, (h + 1) * dv)
                st = s0_ref[sq, h]
                st_b = st.astype(BF16)
                v_b = (vv_ref[g, :, vs].astype(F32) * keep).astype(BF16)
                parts, kt, decay = _hgrn_pairs(
                    qq_ref[g, :, ks].astype(F32) * keep, kk_ref[g, :, ks].astype(F32) * keep,
                    b_all[:, ks], f_all[:, ks],
                    lambda qt, st_b=st_b: jnp.dot(qt, st_b, preferred_element_type=F32))
                decay_col = jnp.broadcast_to(decay, (dv, dk)).T
                s_ref[sq, h] = st * decay_col + lax.dot_general(kt, v_b, TN_DIMS, preferred_element_type=F32)
                pending.append((g, h, parts, v_b))
    m_heads = [[jnp.zeros((c, dv), F32)] * HGRN_HEADS for _ in range(n_groups)]
    for g, h, parts, v_b in pending:
        vs = slice(h * dv, (h + 1) * dv)
        o = _hgrn_combine(parts, v_b, masks)
        m_heads[g][h] = m_heads[g][h] + _head_norm_gate(o, og_ref[...], sg_ref[g, :, vs]).astype(F32)
    m1 = jnp.concatenate([jnp.concatenate(m_heads[g], axis=1) for g in range(n_groups)], axis=0).astype(BF16)
    y = jnp.dot(m1, w_ref[...], preferred_element_type=F32)
    for g in range(n_groups):
        y_ref[g] = y0_ref[g] + y[g * c:(g + 1) * c, :]


def _c_scan_sample(qq, kk, lf, vv, sg, y0, s0, og, w_bf, t_new):
    groups, rows, d = y0.shape
    per = SUBLANES // t_new
    gs = SCAN_GROUPS
    assert groups % gs == 0
    key_w, val_w = qq.shape[2], vv.shape[2]
    gmap = lambda g: (g, 0, 0)
    const = lambda g: (0, 0)
    smap = lambda g: (g, 0, 0, 0)
    sspec = pl.BlockSpec((gs * per, HGRN_HEADS, HGRN_DK, HGRN_DV), smap)
    return pl.pallas_call(
        functools.partial(_c_scan_sample_kernel, t_new=t_new),
        grid=(groups // gs,),
        in_specs=[pl.BlockSpec((gs, rows, key_w), gmap), pl.BlockSpec((gs, rows, key_w), gmap),
                  pl.BlockSpec((gs, rows, key_w), gmap), pl.BlockSpec((gs, rows, val_w), gmap),
                  pl.BlockSpec((gs, rows, val_w), gmap), pl.BlockSpec((gs, rows, d), gmap), sspec,
                  pl.BlockSpec((1, HGRN_DV), const), pl.BlockSpec(w_bf.shape, const)],
        out_specs=(pl.BlockSpec((gs, rows, d), gmap), sspec),
        out_shape=(jax.ShapeDtypeStruct((groups, rows, d), F32),
                   jax.ShapeDtypeStruct(s0.shape, F32)),
        compiler_params=_cparams(1),
        name="c_scan_sample",
    )(qq, kk, lf, vv, sg, y0, s0, og, w_bf)


def kernel(x_prompt, x_sample, cache_k, cache_v, state_conv, state_hgrn, page_table, norm_0, w_in_0, q_norm_0, k_norm_0, conv_w_0, conv_b_0, conv_ln_g_0, conv_ln_b_0, w_out_0, norm_1, w_in_1, lb_logits, o_norm_1, w_out_1):
    b, s, d = x_prompt.shape
    n_seq, t_new, _ = x_sample.shape
    n_tok = n_seq * t_new
    hist = CONV_LEN - 1
    gsum = jnp.kron(jnp.eye(ATT_WIDTH // 2 // ATT_HEAD_DIM, dtype=F32),
                    jnp.full((ATT_HEAD_DIM, ATT_HEAD_DIM), 1.0 / ATT_HEAD_DIM, F32)).astype(BF16)
    qg = jnp.tile(q_norm_0, ATT_HEADS)[None]
    kg = jnp.tile(k_norm_0, ATT_HEADS)[None]
    w_in_0b = w_in_0.astype(BF16)
    w_out_0b = w_out_0.astype(BF16)
    w_in_1b = w_in_1.astype(BF16)
    w_out_1b = w_out_1.astype(BF16)
    tab = _conv_tap_table(conv_w_0)
    conv_args = (tab, conv_b_0[None], conv_ln_g_0[None], conv_ln_b_0[None], w_out_0b)
    heads = lambda a, lead: a.reshape(lead + (ATT_HEADS, ATT_HEAD_DIM))

    qs_s, k_s, v_s, _, _, sga_s, u_s, sgc_s, _ = _ab_in(
        x_sample.reshape(n_tok, d), norm_0[None], w_in_0b, qg, kg, gsum, n_tok)
    q3 = lambda a: a.reshape(n_seq, t_new, a.shape[-1])

    def per_head(a, dt):
        a = a.reshape(n_seq, t_new, ATT_HEADS, ATT_HEAD_DIM).transpose(0, 2, 1, 3).astype(dt)
        return jnp.pad(a, ((0, 0), (0, 0), (0, SUBLANES - t_new), (0, 0)))

    sample_att = (page_table, per_head(qs_s, BF16), per_head(k_s, BF16), per_head(v_s, BF16), per_head(sga_s, F32),
                  cache_k.transpose(0, 2, 3, 1), cache_v.transpose(0, 2, 3, 1), t_new)

    qs, kt_p, vt_p, kb, vbt, sga, u_p, sgc, kmean, matt_s = _ab_in(
        x_prompt.reshape(b * s, d), norm_0[None], w_in_0b, qg, kg, gsum, ROW_TILE, seq_len=s, sample=sample_att)
    heads_t = lambda a: a.reshape(b, ATT_HEADS, ATT_HEAD_DIM, s).transpose(0, 3, 1, 2)
    r3 = lambda a: a.reshape(b, s, a.shape[-1])
    matt = _moba_prompt(r3(qs), r3(kb), vbt, kmean.reshape(b, s // MOBA_BLOCK, ATT_WIDTH), r3(sga))
    y0 = _ab_out_prompt(x_prompt, matt, r3(u_p), r3(sgc), *conv_args, ROW_TILE)
    y_prompt, hgrn_prompt = _c_layer_prompt(y0, norm_1[None], w_in_1b, lb_logits, o_norm_1[None], w_out_1b, SCAN_TILE)

    matt_s = matt_s[:, :, :t_new, :].transpose(0, 2, 1, 3)
    y0_s = _ab_out_sample(x_sample.reshape(n_tok, d), matt_s.reshape(n_tok, ATT_WIDTH), q3(u_s), state_conv,
                          sgc_s.astype(F32), *conv_args)
    qq_s, kk_s, lf_s, vv_s, sg_s = _c_in(y0_s, norm_1[None], w_in_1b, lb_logits, n_tok)
    g8 = lambda a: a.reshape(n_tok // SUBLANES, SUBLANES, a.shape[-1])
    y_s, hgrn_sample = _c_scan_sample(g8(qq_s), g8(kk_s), g8(lf_s), g8(vv_s), g8(sg_s), g8(y0_s), state_hgrn,
                                      o_norm_1[None], w_out_1b, t_new)

    conv_prompt = r3(u_p)[:, s - hist:, :]
    conv_sample = jnp.concatenate([state_conv[:, t_new:, :], q3(u_s)], axis=1)
    return (y_prompt, y_s.reshape(n_seq, t_new, d),
            heads_t(kt_p), heads_t(vt_p), heads(k_s, (n_seq, t_new)), heads(v_s, (n_seq, t_new)),
            conv_prompt, conv_sample, hgrn_prompt, hgrn_sample)
```

```python
import functools

import jax
import jax.numpy as jnp
from jax import lax
from jax.experimental import pallas as pl
from jax.experimental.pallas import tpu as pltpu

EPS = 1e-6
ATT_HEADS = 8
ATT_HEAD_DIM = 64
ATT_WIDTH = ATT_HEADS * ATT_HEAD_DIM
MOBA_BLOCK = 256
MOBA_TOPK = 3
MOBA_PAIR = 2
PAGE_SLOTS = 3
PAGE_SIZE = 128
CONV_CH = 512
CONV_LEN = 31
HGRN_HEADS = 8
HGRN_DK = 128
HGRN_DV = 128
LAYER_C = 1

LANES = 128
SUBLANES = 8
LOG2_E = 1.4426950408889634
HEADS_PER_VREG = LANES // ATT_HEAD_DIM
NEG = -1e30
VMEM_LIMIT = 56 * 1024 * 1024
ROW_TILE = 512
SCAN_TILE = 512
SCAN_GROUPS = 4

F32 = jnp.float32
BF16 = jnp.bfloat16
NT_DIMS = (((1,), (1,)), ((), ()))
TN_DIMS = (((0,), (0,)), ((), ()))


def _silu(x):
    return x * jax.nn.sigmoid(x)


def _cparams(n_axes):
    return pltpu.CompilerParams(dimension_semantics=("arbitrary",) * n_axes,
                                vmem_limit_bytes=VMEM_LIMIT)


def _ab_in_stages(h_ref, w_ref, qg_ref, kg_ref, gsum_ref,
                  qs_ref, k_ref, v_ref, kb_ref, vb_ref, sga_ref, u_ref, sgc_ref, kmean_ref, kv_transposed):
    def proj(c):
        return jnp.dot(h_ref[...], w_ref[:, c * ATT_WIDTH:(c + 1) * ATT_WIDTH], preferred_element_type=F32)

    def head_rms(z, g):
        zz = (z * z).astype(BF16)
        half = ATT_WIDTH // 2
        ms_h = jnp.concatenate(
            [jnp.dot(zz[:, :half], gsum_ref[...], preferred_element_type=F32),
             jnp.dot(zz[:, half:], gsum_ref[...], preferred_element_type=F32)], axis=1)
        return z * lax.rsqrt(ms_h + EPS) * g

    def stage_q():
        q = head_rms(proj(0), qg_ref[...])
        qs_ref[...] = (q * (ATT_HEAD_DIM ** -0.5 * LOG2_E)).astype(BF16)

    def stage_k():
        k = head_rms(proj(1), kg_ref[...])
        if kv_transposed:
            k_ref[0] = k.T
        else:
            k_ref[...] = k
        kb_ref[...] = k.astype(BF16)
        tm = k.shape[0]
        kmean_ref[0] = jnp.mean(k.reshape(tm // MOBA_BLOCK, MOBA_BLOCK, ATT_WIDTH), axis=1)

    def stage_v():
        v = proj(2)
        if kv_transposed:
            v_t = v.T
            v_ref[0] = v_t
            vb_ref[0] = v_t.astype(BF16)
        else:
            v_ref[...] = v
            vb_ref[...] = v.astype(BF16)

    def stage_gate_att():
        sga_ref[...] = _silu(proj(3)).astype(BF16)

    def stage_glu():
        u_ref[...] = proj(4) * jax.nn.sigmoid(proj(5))

    def stage_gate_conv():
        sgc_ref[...] = _silu(proj(6)).astype(BF16)

    return [stage_q, stage_k, stage_v, stage_gate_att, stage_glu, stage_gate_conv]


def _normalise_rows(x_ref, ng_ref, h_ref):
    x = x_ref[...]
    ms = jnp.mean(x * x, axis=-1, keepdims=True)
    h_ref[...] = (x * lax.rsqrt(ms + EPS) * ng_ref[...]).astype(BF16)


def _ab_in_kernel(x_ref, ng_ref, w_ref, qg_ref, kg_ref, gsum_ref, *rest, kv_transposed):
    *outs, h_ref = rest
    _normalise_rows(x_ref, ng_ref, h_ref)
    for stage in _ab_in_stages(h_ref, w_ref, qg_ref, kg_ref, gsum_ref, *outs, kv_transposed):
        stage()


def _ab_in(x2d, ng, w_bf, qg, kg, gsum, tm, seq_len=None, sample=None):
    t, d = x2d.shape
    wn = w_bf.shape[1]
    steps = t // tm
    row = lambda i, *_: (i, 0)
    const = lambda i, *_: (0, 0)
    tile = lambda dt: jax.ShapeDtypeStruct((t, ATT_WIDTH), dt)
    tspec = pl.BlockSpec((tm, ATT_WIDTH), row)
    if seq_len is None:
        kv_shape, kv_spec = tile, tspec
    else:
        per_seq = seq_len // tm
        kv_shape = lambda dt: jax.ShapeDtypeStruct((t // seq_len, ATT_WIDTH, seq_len), dt)
        kv_spec = pl.BlockSpec((1, ATT_WIDTH, tm), lambda i, *_: (i // per_seq, 0, i % per_seq))
    out_shape = (tile(BF16), kv_shape(F32), kv_shape(F32), tile(BF16), kv_shape(BF16), tile(BF16), tile(F32),
                 tile(BF16), jax.ShapeDtypeStruct((steps, tm // MOBA_BLOCK, ATT_WIDTH), F32))
    in_specs = [pl.BlockSpec((tm, d), row), pl.BlockSpec((1, d), const),
                pl.BlockSpec((d, wn), const, pipeline_mode=pl.Buffered(1)),
                pl.BlockSpec((1, ATT_WIDTH), const), pl.BlockSpec((1, ATT_WIDTH), const),
                pl.BlockSpec(gsum.shape, const)]
    out_specs = ((tspec, kv_spec, kv_spec, tspec, kv_spec) + (tspec,) * 3
                 + (pl.BlockSpec((1, tm // MOBA_BLOCK, ATT_WIDTH), lambda i, *_: (i, 0, 0)),))
    scratch = [pltpu.VMEM((tm, d), BF16)]
    kv_t = seq_len is not None
    if sample is None:
        return pl.pallas_call(
            functools.partial(_ab_in_kernel, kv_transposed=kv_t),
            grid=(steps,), in_specs=in_specs, out_specs=out_specs, out_shape=out_shape,
            scratch_shapes=scratch, compiler_params=_cparams(1), name="ab_in",
        )(x2d, ng, w_bf, qg, kg, gsum)

    page_table, sq, skn, svn, ssga, cache_kt, cache_vt, t_new = sample
    n_seq, _, rows, _ = sq.shape
    n_pages = page_table.shape[1]
    past = n_pages * PAGE_SIZE
    per_step = n_seq // steps
    assert past % MOBA_BLOCK == 0, "the sample group's past must end on a MoBA block boundary"
    assert per_step * steps == n_seq and n_seq >= PAGE_SLOTS
    sspec = pl.BlockSpec((per_step, ATT_HEADS, rows, ATT_HEAD_DIM), lambda i, *_: (i, 0, 0, 0))
    page_buf = pltpu.VMEM((PAGE_SLOTS, ATT_HEADS, ATT_HEAD_DIM, past), F32)
    grid_spec = pltpu.PrefetchScalarGridSpec(
        num_scalar_prefetch=1,
        grid=(steps,),
        in_specs=in_specs + [sspec] * 4 + [pl.BlockSpec(memory_space=pl.ANY)] * 2,
        out_specs=out_specs + (sspec,),
        scratch_shapes=scratch + [page_buf, page_buf, pltpu.SemaphoreType.DMA((2, PAGE_SLOTS))],
    )
    return pl.pallas_call(
        functools.partial(_ab_in_sample_kernel, kv_transposed=kv_t, n_pages=n_pages, t_new=t_new,
                          per_step=per_step),
        grid_spec=grid_spec,
        out_shape=out_shape + (jax.ShapeDtypeStruct(sq.shape, F32),),
        compiler_params=_cparams(1),
        name="ab_in_sample",
    )(page_table, x2d, ng, w_bf, qg, kg, gsum, sq, skn, svn, ssga, cache_kt, cache_vt)


def _moba_prompt_kernel(qs_ref, kb_ref, vt_ref, kmean_ref, sga_ref, o_ref, bias_ref, s_ref, e_ref, *, nb):
    pair = pl.program_id(2)
    blk = MOBA_BLOCK
    cols = HEADS_PER_VREG * blk
    km_parts = _split3(kmean_ref[0])
    lane = lax.broadcasted_iota(jnp.int32, (blk, LANES), 1)
    key_i = lax.broadcasted_iota(jnp.int32, (blk, cols), 0)
    qry_i = lax.broadcasted_iota(jnp.int32, (blk, cols), 1) % blk
    causal_bias = jnp.where(key_i <= qry_i, 0.0, NEG).astype(F32)
    b_i = lax.broadcasted_iota(jnp.int32, (nb, blk), 0)

    def block_kv(jb):
        rows = pl.ds(pl.multiple_of(jb * blk, blk), blk)
        return kb_ref[0, rows, :], vt_ref[0, :, rows]

    def prepare(c):
        j = MOBA_PAIR * pair + c
        q2 = qs_ref[0, c * blk:(c + 1) * blk, :]
        valid = b_i < j
        qhs = []
        for hh in range(HEADS_PER_VREG):
            head_mask = jnp.where((lane // ATT_HEAD_DIM) == hh, 1.0, 0.0).astype(BF16)
            qh = q2 * head_mask
            g = sum(lax.dot_general(part, qh, NT_DIMS, preferred_element_type=F32) for part in km_parts)
            g = jnp.where(valid, g, -jnp.inf)
            rank = jnp.zeros((nb, blk), F32)
            for i in range(nb):
                gi = g[i:i + 1, :]
                rank = rank + jnp.where(gi > g, 1.0, jnp.where((gi == g) & (b_i > i), 1.0, 0.0))
            sel_bias = jnp.where(valid & (rank < MOBA_TOPK), 0.0, NEG)
            bias_ref[c, :, :, hh * blk:(hh + 1) * blk] = jnp.broadcast_to(
                sel_bias[:, None, :], (nb, SUBLANES, blk))
            qhs.append(qh)
        return jnp.concatenate(qhs, axis=0)

    def block_bias(c, jb):
        return jnp.broadcast_to(bias_ref[c, jb][None], (blk // SUBLANES, SUBLANES, cols)).reshape(blk, cols)

    def scores(q_cols, k_rows):
        return lax.dot_general(k_rows, q_cols, NT_DIMS, preferred_element_type=F32)

    def update(carry, s, v_cols):
        if carry is None:
            m_new = jnp.max(s, axis=0, keepdims=True)
            p = jnp.exp2(s - m_new)
            return (m_new, jnp.sum(p, axis=0, keepdims=True),
                    jnp.dot(v_cols, p.astype(BF16), preferred_element_type=F32))
        m, l, acc = carry
        m_new = jnp.maximum(m, jnp.max(s, axis=0, keepdims=True))
        alpha = jnp.exp2(m - m_new)
        p = jnp.exp2(s - m_new)
        l = alpha * l + jnp.sum(p, axis=0, keepdims=True)
        acc = alpha * acc + jnp.dot(v_cols, p.astype(BF16), preferred_element_type=F32)
        return m_new, l, acc

    span = MOBA_PAIR * blk
    last = nb // MOBA_PAIR - 1

    def step_keys(i):
        return pl.ds(pl.multiple_of(jnp.minimum(i, last) * span, span), span)

    def issue(i, buf):
        kj = kb_ref[0, step_keys(i), :]
        for c in range(MOBA_PAIR):
            s_ref[buf, c] = scores(q_cols[c], kj)

    def consume(i, buf, carry):
        vj = vt_ref[0, :, step_keys(i)]
        out = []
        for c in range(MOBA_PAIR):
            bias = jnp.concatenate([block_bias(c, i * MOBA_PAIR + u) for u in range(MOBA_PAIR)], axis=0)
            out.append(update(carry[c], s_ref[buf, c] + bias, vj))
        return tuple(out)

    def body(t, carry):
        i0 = 2 * t
        issue(i0 + 1, 1)
        carry = consume(i0, 0, carry)
        issue(i0 + 2, 0)
        return consume(i0 + 1, 1, carry)

    q_cols = [prepare(c) for c in range(MOBA_PAIR)]
    own_kv = [block_kv(MOBA_PAIR * pair + c) for c in range(MOBA_PAIR)]
    cross = [(c, e) for c in range(1, MOBA_PAIR) for e in range(c)]
    for c in range(MOBA_PAIR):
        e_ref[c] = scores(q_cols[c], own_kv[c][0])
    for n, (c, e) in enumerate(cross):
        e_ref[MOBA_PAIR + n] = scores(q_cols[c], own_kv[e][0])
    issue(0, 0)
    state = [update(None, e_ref[c] + causal_bias, own_kv[c][1]) for c in range(MOBA_PAIR)]
    for n, (c, e) in enumerate(cross):
        state[c] = update(state[c], e_ref[MOBA_PAIR + n] + block_bias(c, MOBA_PAIR * pair + e), own_kv[e][1])
    state = lax.fori_loop(0, pair // 2, body, tuple(state))
    state = lax.cond(lax.rem(pair, 2) == 1, lambda st: consume(pair - 1, 0, st), lambda st: st, state)
    for c in range(MOBA_PAIR):
        _, l, acc = state[c]
        o_t = acc / l
        o_t = jnp.concatenate([o_t[hh * ATT_HEAD_DIM:(hh + 1) * ATT_HEAD_DIM, hh * blk:(hh + 1) * blk]
                               for hh in range(HEADS_PER_VREG)], axis=0)
        rows = slice(c * blk, (c + 1) * blk)
        o_ref[0, rows, :] = (o_t.T * sga_ref[0, rows, :].astype(F32)).astype(BF16)


def _moba_prompt(qs, kb, vt, kmean, sga):
    b, s, _ = qs.shape
    nb = s // MOBA_BLOCK
    assert nb % MOBA_PAIR == 0
    npair = ATT_WIDTH // LANES
    rows = MOBA_PAIR * MOBA_BLOCK
    qmap = lambda bi, hp, j: (bi, j, hp)
    kvmap = lambda bi, hp, j: (bi, 0, hp)
    return pl.pallas_call(
        functools.partial(_moba_prompt_kernel, nb=nb),
        grid=(b, npair, nb // MOBA_PAIR),
        in_specs=[pl.BlockSpec((1, rows, LANES), qmap),
                  pl.BlockSpec((1, s, LANES), kvmap),
                  pl.BlockSpec((1, LANES, s), lambda bi, hp, j: (bi, hp, 0)),
                  pl.BlockSpec((1, nb, LANES), kvmap),
                  pl.BlockSpec((1, rows, LANES), qmap)],
        out_specs=pl.BlockSpec((1, rows, LANES), qmap),
        out_shape=jax.ShapeDtypeStruct((b, s, ATT_WIDTH), BF16),
        scratch_shapes=[pltpu.VMEM((MOBA_PAIR, nb, SUBLANES, HEADS_PER_VREG * MOBA_BLOCK), F32),
                        pltpu.VMEM((2, MOBA_PAIR, rows, HEADS_PER_VREG * MOBA_BLOCK), F32),
                        pltpu.VMEM((MOBA_PAIR * (MOBA_PAIR + 1) // 2, MOBA_BLOCK, HEADS_PER_VREG * MOBA_BLOCK), F32)],
        compiler_params=_cparams(3),
        name="moba_prompt",
    )(qs, kb, vt, kmean, sga)


CONV_HALO = 32
CONV_ROWS = 64


CONV_LEAD = CONV_HALO - (CONV_LEN - 1)
CONV_SPAN = -(-(CONV_LEAD + SUBLANES - 1 + CONV_LEN) // SUBLANES)


def _conv_tap_table(conv_w):
    m = jnp.arange(CONV_SPAN)[:, None, None]
    s = jnp.arange(SUBLANES)[None, :, None]
    r = jnp.arange(SUBLANES)[None, None, :]
    idx = SUBLANES * m + s - r - CONV_LEAD
    ok = (idx >= 0) & (idx < CONV_LEN)
    tab = jnp.where(ok[..., None], conv_w[jnp.clip(idx, 0, CONV_LEN - 1)], 0.0)
    return tab.reshape(CONV_SPAN * SUBLANES, SUBLANES, conv_w.shape[-1])


def _tap_used(ms):
    lo = ms - (SUBLANES - 1) - CONV_LEAD
    return lo + SUBLANES - 1 >= 0 and lo < CONV_LEN


def _conv_rows(xpad_ref, ybuf_ref, tab_ref, cb_ref, n_rows):
    groups = CONV_ROWS // SUBLANES
    for lt in range(CONV_CH // LANES):
        ls = slice(lt * LANES, (lt + 1) * LANES)

        for c in range(n_rows // CONV_ROWS):
            r0 = c * CONV_ROWS
            win_ref = xpad_ref.at[pl.ds(r0, CONV_ROWS + CONV_SPAN * SUBLANES)]
            taps = {ms: tab_ref[ms, :, ls] for ms in range(CONV_SPAN * SUBLANES) if _tap_used(ms)}
            acc = [jnp.broadcast_to(cb_ref[:, ls], (SUBLANES, LANES))] * groups
            for rho in range(CONV_ROWS + CONV_SPAN * SUBLANES):
                users = [(rho // SUBLANES - m, SUBLANES * m + rho % SUBLANES) for m in range(CONV_SPAN)]
                users = [(g, ms) for g, ms in users if 0 <= g < groups and ms in taps]
                if not users:
                    continue
                xb = win_ref[pl.ds(rho, 1), ls]
                for g, ms in users:
                    acc[g] = acc[g] + taps[ms] * xb
            ybuf_ref[pl.ds(r0, CONV_ROWS), ls] = jnp.concatenate(acc, axis=0)


def _ln_silu_gate(y, lg_ref, lb_ref, sgc):
    mu = jnp.mean(y, axis=-1, keepdims=True)
    cen = y - mu
    var = jnp.mean(cen * cen, axis=-1, keepdims=True)
    yn = cen * lax.rsqrt(var + EPS) * lg_ref[...] + lb_ref[...]
    return (_silu(yn) * sgc.astype(F32)).astype(BF16)


def _ab_out_prompt_kernel(x_ref, matt_ref, u_ref, halo_ref, sgc_ref, tab_ref, cb_ref, lg_ref, lb_ref, w_ref,
                          y_ref, xpad_ref, ybuf_ref, m_ref):
    t = pl.program_id(1)
    tm = u_ref.shape[1]
    halo = halo_ref[0]
    xpad_ref[0:CONV_HALO, :] = jnp.where(t == 0, jnp.zeros_like(halo), halo)
    xpad_ref[CONV_HALO:CONV_HALO + tm, :] = u_ref[0]
    y_att = x_ref[0] + jnp.dot(matt_ref[0], w_ref[0:ATT_WIDTH, :], preferred_element_type=F32)
    _conv_rows(xpad_ref, ybuf_ref, tab_ref, cb_ref, tm)
    for c in range(tm // CONV_ROWS):
        rows = slice(c * CONV_ROWS, (c + 1) * CONV_ROWS)
        m_ref[rows, :] = _ln_silu_gate(ybuf_ref[rows, :], lg_ref, lb_ref, sgc_ref[0, rows, :])
    y_ref[0] = y_att + jnp.dot(m_ref[...], w_ref[ATT_WIDTH:, :], preferred_element_type=F32)


def _ab_out_prompt(x, matt, u, sgc, tab, cb, lg, lb, w_bf, tm):
    b, s, d = x.shape
    hpt = tm // CONV_HALO
    tmap = lambda bi, t: (bi, t, 0)
    hmap = lambda bi, t: (bi, jnp.maximum(t * hpt - 1, 0), 0)
    const = lambda bi, t: (0, 0)
    return pl.pallas_call(
        _ab_out_prompt_kernel,
        grid=(b, s // tm),
        in_specs=[pl.BlockSpec((1, tm, d), tmap), pl.BlockSpec((1, tm, ATT_WIDTH), tmap),
                  pl.BlockSpec((1, tm, CONV_CH), tmap), pl.BlockSpec((1, CONV_HALO, CONV_CH), hmap),
                  pl.BlockSpec((1, tm, CONV_CH), tmap),
                  pl.BlockSpec(tab.shape, lambda bi, t: (0, 0, 0)), pl.BlockSpec((1, CONV_CH), const),
                  pl.BlockSpec((1, CONV_CH), const), pl.BlockSpec((1, CONV_CH), const),
                  pl.BlockSpec(w_bf.shape, const)],
        out_specs=pl.BlockSpec((1, tm, d), tmap),
        out_shape=jax.ShapeDtypeStruct((b, s, d), F32),
        scratch_shapes=[pltpu.VMEM((tm + CONV_SPAN * SUBLANES, CONV_CH), F32),
                        pltpu.VMEM((tm, CONV_CH), F32),
                        pltpu.VMEM((tm, CONV_CH), BF16)],
        compiler_params=_cparams(2),
        name="ab_out_prompt",
    )(x, matt, u, u, sgc, tab, cb, lg, lb, w_bf)


C_COLS = 512


def _c_in_kernel(x_ref, ng_ref, w_ref, lbl_ref, qq_ref, kk_ref, lf_ref, vv_ref, sg_ref):
    x = x_ref[...]
    ms = jnp.mean(x * x, axis=-1, keepdims=True)
    h = (x * lax.rsqrt(ms + EPS) * ng_ref[...]).astype(BF16)
    p = jax.nn.softmax(lbl_ref[...], axis=0)
    lb = jnp.sum(p[0:LAYER_C + 1], axis=0, keepdims=True) - p[0:1]
    key_w = qq_ref.shape[1]
    per = key_w // C_COLS
    for c in range(per):
        cs = slice(c * C_COLS, (c + 1) * C_COLS)

        def proj(g, cs=cs):
            return jnp.dot(h, w_ref[:, g * key_w + cs.start:g * key_w + cs.stop], preferred_element_type=F32)

        qq_ref[:, cs] = _silu(proj(0)).astype(BF16)
        lbc = lb[:, cs]
        f = lbc + (1.0 - lbc) * jax.nn.sigmoid(proj(1))
        lf_ref[:, cs] = jnp.log(f)
        kk_ref[:, cs] = (1.0 - f).astype(BF16)
        vv_ref[:, cs] = proj(2).astype(BF16)
        sg_ref[:, cs] = _silu(proj(3)).astype(BF16)


def _c_in(x2d, ng, w_bf, lbl, tm):
    t, d = x2d.shape
    key_w = lbl.shape[1]
    row = lambda i: (i, 0)
    const = lambda i: (0, 0)
    tile = lambda dt: jax.ShapeDtypeStruct((t, key_w), dt)
    tspec = pl.BlockSpec((tm, key_w), row)
    return pl.pallas_call(
        _c_in_kernel,
        grid=(t // tm,),
        in_specs=[pl.BlockSpec((tm, d), row), pl.BlockSpec((1, d), const), pl.BlockSpec(w_bf.shape, const),
                  pl.BlockSpec(lbl.shape, const)],
        out_specs=(tspec,) * 5,
        out_shape=(tile(BF16), tile(BF16), tile(F32), tile(BF16), tile(BF16)),
        compiler_params=_cparams(1),
        name="c_in",
    )(x2d, ng, w_bf, lbl)


HGRN_SUB = SUBLANES


def _split3(x):
    hi = x.astype(BF16)
    r1 = x - hi.astype(F32)
    mid = r1.astype(BF16)
    lo = (r1 - mid.astype(F32)).astype(BF16)
    return hi, mid, lo


def _chunk_masks(c):
    t_i = lax.broadcasted_iota(jnp.int32, (c, c), 0)
    s_i = lax.broadcasted_iota(jnp.int32, (c, c), 1)
    same = (t_i // HGRN_SUB) == (s_i // HGRN_SUB)
    diag = [t_i - s_i == d for d in range(HGRN_SUB)]
    col_group = [(s_i // HGRN_SUB) == a for a in range(c // HGRN_SUB)]
    return diag, col_group, same & (s_i <= t_i)


def _hgrn_pairs(q, k, b, f, o_inter_fn):
    c = q.shape[0]
    n_sub = c // HGRN_SUB
    b_last = b[c - 1:c, :]
    o_inter = o_inter_fn((q * jnp.exp2(b)).astype(BF16))
    kt = (k * jnp.exp2(b_last - b)).astype(BF16)

    k_b = k.astype(BF16)
    f3 = f.reshape(n_sub, HGRN_SUB, f.shape[1])
    stack = [q.astype(BF16)]
    prod = f3
    for d in range(1, HGRN_SUB):
        if d > 1:
            prod = prod * pltpu.roll(f3, d - 1, 1)
        stack.append((q * prod.reshape(f.shape)).astype(BF16))
    zd = lax.dot_general(jnp.concatenate(stack, axis=0), k_b, NT_DIMS, preferred_element_type=F32)

    zo = None
    if n_sub > 1:
        b3 = b.reshape(n_sub, HGRN_SUB, b.shape[1])
        b_end = jnp.broadcast_to(b3[:, HGRN_SUB - 1:HGRN_SUB, :], b3.shape).reshape(b.shape)
        ksc = (k * jnp.exp2(b_end - b)).astype(BF16)
        stack = []
        for g in range(1, n_sub):
            r = g * HGRN_SUB
            stack.append((q[r:, :] * jnp.exp2(b[r:, :] - b[r - 1:r, :])).astype(BF16))
        zo = lax.dot_general(jnp.concatenate(stack, axis=0), ksc, NT_DIMS, preferred_element_type=F32)
    return (o_inter, zd, zo), kt, jnp.exp2(b_last)


def _hgrn_combine(parts, v_b, masks):
    o_inter, zd, zo = parts
    c = o_inter.shape[0]
    n_sub = c // HGRN_SUB
    diag, col_group, same_lower = masks
    a = jnp.zeros((c, c), F32)
    row = 0
    for g in range(1, n_sub):
        r = g * HGRN_SUB
        part = jnp.concatenate([jnp.zeros((r, c), F32), zo[row:row + c - r, :]], axis=0)
        a = jnp.where(col_group[g - 1], part, a)
        row += c - r
    a_in = zd[0:c, :]
    for d in range(1, HGRN_SUB):
        a_in = jnp.where(diag[d], zd[d * c:(d + 1) * c, :], a_in)
    a = jnp.where(same_lower, a_in, a)
    return o_inter + jnp.dot(a.astype(BF16), v_b, preferred_element_type=F32)


def _head_norm_gate(o, og, sg):
    ms = jnp.mean(o * o, axis=-1, keepdims=True)
    return ((o * lax.rsqrt(ms + EPS) * og) * sg.astype(F32)).astype(BF16)


HGRN_CHUNK = 64


def _c_layer_prompt_kernel(xn_ref, y0_ref, ng_ref, w_ref, lbl_ref, og_ref, wo_ref, y_ref, s_ref,
                           st_ref, m_ref, zd_ref, zo_ref, hn_ref, lb_ref, wh_ref, qq_s, kk_s, lf_s, vv_s, sg_s,
                           *, tiles_per_seq):
    g = pl.program_id(0)
    t = lax.rem(g, tiles_per_seq)
    cur = lax.rem(g, 2)
    nxt = 1 - cur
    ts = y0_ref.shape[1]
    c = HGRN_CHUNK
    dk, dv = HGRN_DK, HGRN_DV
    assert ts // c == HGRN_HEADS and dk == dv == LANES

    p = jax.nn.softmax(lbl_ref[...], axis=0)
    lb = jnp.sum(p[0:LAYER_C + 1], axis=0, keepdims=True) - p[0:1]
    for h in range(HGRN_HEADS):
        lb_ref[h] = lb[:, h * dk:(h + 1) * dk]

    def project_qf(hd, buf):
        z = jnp.dot(hn_ref[...], wh_ref[hd, :, 0:2 * dk], preferred_element_type=F32)
        qq_s[buf, hd] = _silu(z[:, 0:dk]).astype(BF16)
        lbh = lb_ref[hd]
        f = lbh + (1.0 - lbh) * jax.nn.sigmoid(z[:, dk:])
        lf_s[buf, hd] = jnp.log(f)
        kk_s[buf, hd] = (1.0 - f).astype(BF16)

    def project_vg(hd, buf):
        z = jnp.dot(hn_ref[...], wh_ref[hd, :, 2 * dk:], preferred_element_type=F32)
        vv_s[buf, hd] = z[:, 0:dv].astype(BF16)
        sg_s[buf, hd] = _silu(z[:, dv:]).astype(BF16)

    def project_head(hd, buf):
        project_qf(hd, buf)
        project_vg(hd, buf)

    def normalise(x):
        ms = jnp.mean(x * x, axis=-1, keepdims=True)
        hn_ref[...] = (x * lax.rsqrt(ms + EPS) * ng_ref[...]).astype(BF16)

    @pl.when(g == 0)
    def _():
        key_w = HGRN_HEADS * dk
        for h in range(HGRN_HEADS):
            for part in range(w_ref.shape[1] // key_w):
                wh_ref[h, :, part * dk:(part + 1) * dk] = w_ref[:, part * key_w + h * dk:part * key_w + (h + 1) * dk]
        normalise(y0_ref[0])
        for h in range(HGRN_HEADS):
            project_head(h, 0)

    @pl.when(t == 0)
    def _():
        st_ref[...] = jnp.zeros_like(st_ref)

    normalise(xn_ref[0])

    def chunk(ci, carry):
        r0 = pl.multiple_of(ci * c, c)
        rows = pl.ds(r0, c)
        heads = range(HGRN_HEADS)
        q_c = [qq_s[cur, h, rows, :] for h in heads]
        k_c = [kk_s[cur, h, rows, :] for h in heads]
        v_c = [vv_s[cur, h, rows, :] for h in heads]
        sg_c = [sg_s[cur, h, rows, :] for h in heads]
        lf2 = jnp.concatenate([lf_s[cur, h, rows, :] for h in heads], axis=1) * LOG2_E
        project_qf(ci, nxt)
        masks = _chunk_masks(c)
        r_i = lax.broadcasted_iota(jnp.int32, (c, c), 0)
        c_i = lax.broadcasted_iota(jnp.int32, (c, c), 1)
        tri = jnp.where(c_i <= r_i, 1.0, 0.0).astype(BF16)
        f_all = jnp.exp2(lf2)
        b_all = sum(jnp.dot(tri, part, preferred_element_type=F32) for part in _split3(lf2))
        parts = []
        for h in range(HGRN_HEADS):
            ks = slice(h * dk, (h + 1) * dk)
            st = st_ref[h]
            st_b = st.astype(BF16)
            (o_inter, zd, zo), kt, decay = _hgrn_pairs(
                q_c[h].astype(F32), k_c[h].astype(F32), b_all[:, ks], f_all[:, ks],
                lambda qt, st_b=st_b: lax.dot_general(qt, st_b, NT_DIMS, preferred_element_type=F32))
            zd_ref[h] = zd
            zo_ref[h] = zo
            parts.append(o_inter)
            st_ref[h] = st * decay + lax.dot_general(v_c[h], kt, TN_DIMS, preferred_element_type=F32)
        project_vg(ci, nxt)
        for h in range(HGRN_HEADS):
            o = _hgrn_combine((parts[h], zd_ref[h], zo_ref[h]), v_c[h], masks)
            m_ref[rows, h * dv:(h + 1) * dv] = _head_norm_gate(o, og_ref[...], sg_c[h])
        return carry

    lax.fori_loop(0, ts // c, chunk, 0)
    y_ref[0] = y0_ref[0] + jnp.dot(m_ref[...], wo_ref[...], preferred_element_type=F32)

    @pl.when(t == tiles_per_seq - 1)
    def _():
        for h in range(HGRN_HEADS):
            s_ref[0, h] = st_ref[h].T


def _c_layer_prompt(y0, ng, w_bf, lbl, og, wo_bf, ts):
    b, s, d = y0.shape
    per_head = w_bf.shape[1] // HGRN_HEADS
    tps = s // ts
    n_tiles = b * tps
    n_sub = HGRN_CHUNK // HGRN_SUB
    cur_map = lambda g: (g // tps, g % tps, 0)
    nxt_map = lambda g: (jnp.minimum(g + 1, n_tiles - 1) // tps, jnp.minimum(g + 1, n_tiles - 1) % tps, 0)
    const = lambda g: (0, 0)
    head_buf = lambda dt: pltpu.VMEM((2, HGRN_HEADS, ts, LANES), dt)
    return pl.pallas_call(
        functools.partial(_c_layer_prompt_kernel, tiles_per_seq=tps),
        grid=(n_tiles,),
        in_specs=[pl.BlockSpec((1, ts, d), nxt_map), pl.BlockSpec((1, ts, d), cur_map),
                  pl.BlockSpec((1, d), const),
                  pl.BlockSpec(w_bf.shape, const, pipeline_mode=pl.Buffered(1)),
                  pl.BlockSpec(lbl.shape, const), pl.BlockSpec((1, HGRN_DV), const),
                  pl.BlockSpec(wo_bf.shape, const)],
        out_specs=(pl.BlockSpec((1, ts, d), cur_map),
                   pl.BlockSpec((1, HGRN_HEADS, HGRN_DK, HGRN_DV), lambda g: (g // tps, 0, 0, 0))),
        out_shape=(jax.ShapeDtypeStruct((b, s, d), F32),
                   jax.ShapeDtypeStruct((b, HGRN_HEADS, HGRN_DK, HGRN_DV), F32)),
        scratch_shapes=[pltpu.VMEM((HGRN_HEADS, HGRN_DV, HGRN_DK), F32),
                        pltpu.VMEM((ts, HGRN_HEADS * HGRN_DV), BF16),
                        pltpu.VMEM((HGRN_HEADS, HGRN_SUB * HGRN_CHUNK, HGRN_CHUNK), F32),
                        pltpu.VMEM((HGRN_HEADS, (n_sub * (n_sub - 1) // 2) * HGRN_SUB, HGRN_CHUNK), F32),
                        pltpu.VMEM((ts, d), BF16),
                        pltpu.VMEM((HGRN_HEADS, 1, LANES), F32),
                        pltpu.VMEM((HGRN_HEADS, d, per_head), BF16),
                        head_buf(BF16), head_buf(BF16), head_buf(F32), head_buf(BF16), head_buf(BF16)],
        compiler_params=_cparams(1),
        name="c_layer_prompt",
    )(y0, y0, ng, w_bf, lbl, og, wo_bf)


def _page_copies(pt_ref, ck_hbm, cv_hbm, kbuf, vbuf, sem, seq, sl, n_pages):
    out = []
    for p in range(n_pages):
        page = pt_ref[seq, p]
        toks = pl.ds(p * PAGE_SIZE, PAGE_SIZE)
        out.append(pltpu.make_async_copy(ck_hbm.at[page], kbuf.at[sl, :, :, toks], sem.at[0, sl]))
        out.append(pltpu.make_async_copy(cv_hbm.at[page], vbuf.at[sl, :, :, toks], sem.at[1, sl]))
    return out


def _ab_in_sample_kernel(pt_ref, x_ref, ng_ref, w_ref, qg_ref, kg_ref, gsum_ref,
                         sq_ref, skn_ref, svn_ref, ssga_ref, ck_hbm, cv_hbm, *rest,
                         kv_transposed, n_pages, t_new, per_step):
    *outs, so_ref, h_ref, kbuf, vbuf, sem = rest
    step = pl.program_id(0)
    n_seq = pl.num_programs(0) * per_step
    copies = functools.partial(_page_copies, pt_ref, ck_hbm, cv_hbm, kbuf, vbuf, sem, n_pages=n_pages)

    _normalise_rows(x_ref, ng_ref, h_ref)
    stages = _ab_in_stages(h_ref, w_ref, qg_ref, kg_ref, gsum_ref, *outs, kv_transposed)
    share = -(-len(stages) // per_step)

    ahead = PAGE_SLOTS - 1

    @pl.when(step == 0)
    def _():
        for s0 in range(ahead):
            for cp in copies(seq=s0, sl=s0):
                cp.start()

    for j in range(per_step):
        seq = step * per_step + j
        slot = lax.rem(seq, PAGE_SLOTS)

        @pl.when(seq + ahead < n_seq)
        def _():
            for cp in copies(seq=seq + ahead, sl=lax.rem(seq + ahead, PAGE_SLOTS)):
                cp.start()

        for cp in copies(seq=seq, sl=slot):
            cp.wait()

        def between(j=j):
            for stage in stages[j * share:(j + 1) * share]:
                stage()

        _sample_attention(j, slot, sq_ref, skn_ref, svn_ref, ssga_ref, so_ref, kbuf, vbuf,
                          t_new, n_pages * PAGE_SIZE // MOBA_BLOCK, between)


def _sample_attention(sq, slot, qs_ref, kn_ref, vn_ref, sga_ref, o_ref, kbuf, vbuf, t_new, nb, between):
    rows = qs_ref.shape[2]
    b_i = lax.broadcasted_iota(jnp.int32, (rows, LANES), 1)
    o_row = lax.broadcasted_iota(jnp.int32, (rows, rows), 0)
    o_col = lax.broadcasted_iota(jnp.int32, (rows, rows), 1)
    raw = []
    for h in range(ATT_HEADS):
        q_h = qs_ref[sq, h]
        raw.append((jnp.dot(q_h, kbuf[slot, h].astype(BF16), preferred_element_type=F32),
                    lax.dot_general(q_h, kn_ref[sq, h], NT_DIMS, preferred_element_type=F32)))
    between()
    for h in range(ATT_HEADS):
        s, s_own = raw[h]
        blocks = [s[:, j * MOBA_BLOCK:(j + 1) * MOBA_BLOCK] for j in range(nb)]
        gates = [jnp.sum(blk, axis=1, keepdims=True) for blk in blocks]
        gate = jnp.full((rows, LANES), -jnp.inf, F32)
        for i in range(nb):
            gate = jnp.where(b_i == i, gates[i], gate)
        rank = jnp.zeros((rows, LANES), F32)
        for i in range(nb):
            gi = gates[i]
            rank = rank + jnp.where(gi > gate, 1.0, jnp.where((gi == gate) & (b_i > i), 1.0, 0.0))
        sel_bias = jnp.where(rank < min(MOBA_TOPK, nb), 0.0, NEG)
        s = jnp.concatenate([blocks[j] + sel_bias[:, j:j + 1] for j in range(nb)], axis=1)

        s_own = jnp.where((o_col <= o_row) & (o_col < t_new), s_own, NEG)

        m = jnp.maximum(jnp.max(s, axis=1, keepdims=True), jnp.max(s_own, axis=1, keepdims=True))
        p = jnp.exp2(s - m)
        p_own = jnp.exp2(s_own - m)
        l = jnp.sum(p, axis=1, keepdims=True) + jnp.sum(p_own, axis=1, keepdims=True)
        o = (lax.dot_general(p.astype(BF16), vbuf[slot, h].astype(BF16), NT_DIMS, preferred_element_type=F32)
             + jnp.dot(p_own.astype(BF16), vn_ref[sq, h], preferred_element_type=F32)) / l
        o_ref[sq, h] = o * sga_ref[sq, h]


def _ab_out_sample_kernel(x_ref, matt_ref, u_ref, buf_ref, sgc_ref, tab_ref, cb_ref, lg_ref, lb_ref, w_ref,
                          y_ref, xpad_ref, m_ref):
    n_seq, t_new, _ = u_ref.shape
    hist = buf_ref.shape[1]
    per = SUBLANES // t_new
    assert per * t_new == SUBLANES and n_seq % per == 0 and hist == CONV_LEN - 1
    span = CONV_SPAN * SUBLANES
    xpad_ref[...] = jnp.zeros_like(xpad_ref)
    row8 = lax.broadcasted_iota(jnp.int32, (SUBLANES, CONV_CH), 0)

    def group(gi, carry):
        y8 = jnp.zeros((SUBLANES, CONV_CH), F32)
        for j in range(per):
            sq = gi * per + j
            off = j * t_new
            xpad_ref[j, CONV_HALO - hist + off:CONV_HALO + off, :] = buf_ref[sq]
            xpad_ref[j, CONV_HALO + off:CONV_HALO + off + t_new, :] = u_ref[sq]
            cols = []
            for lt in range(CONV_CH // LANES):
                ls = slice(lt * LANES, (lt + 1) * LANES)
                acc = jnp.broadcast_to(cb_ref[:, ls], (SUBLANES, LANES))
                for ms in range(span):
                    if _tap_used(ms):
                        acc = acc + tab_ref[ms, :, ls] * xpad_ref[j, pl.ds(ms, 1), ls]
                cols.append(acc)
            yj = jnp.concatenate(cols, axis=1)
            y8 = jnp.where((row8 >= off) & (row8 < off + t_new), yj, y8)
        r0 = pl.multiple_of(gi * SUBLANES, SUBLANES)
        rows = pl.ds(r0, SUBLANES)
        m_ref[rows, 0:ATT_WIDTH] = matt_ref[rows, :]
        m_ref[rows, ATT_WIDTH:] = _ln_silu_gate(y8, lg_ref, lb_ref, sgc_ref[rows, :]).astype(F32)
        return carry

    lax.fori_loop(0, n_seq // per, group, 0)
    y_ref[...] = x_ref[...] + jnp.dot(m_ref[...].astype(BF16), w_ref[...], preferred_element_type=F32)


def _ab_out_sample(x2d, matt2d, u3, buf, sgc2d, tab, cb, lg, lb, w_bf):
    t, d = x2d.shape
    n_seq, t_new, _ = u3.shape
    per = SUBLANES // t_new
    vm = lambda: pl.BlockSpec(memory_space=pltpu.VMEM)
    return pl.pallas_call(
        _ab_out_sample_kernel,
        in_specs=[vm() for _ in range(10)],
        out_specs=vm(),
        out_shape=jax.ShapeDtypeStruct((t, d), F32),
        scratch_shapes=[pltpu.VMEM((per, CONV_HALO + 2 * SUBLANES, CONV_CH), F32),
                        pltpu.VMEM((t, ATT_WIDTH + CONV_CH), F32)],
        compiler_params=pltpu.CompilerParams(vmem_limit_bytes=VMEM_LIMIT),
        name="ab_out_sample",
    )(x2d, matt2d, u3, buf, sgc2d, tab, cb, lg, lb, w_bf)


def _c_scan_sample_kernel(qq_ref, kk_ref, lf_ref, vv_ref, sg_ref, y0_ref, s0_ref, og_ref, w_ref,
                          y_ref, s_ref, *, t_new):
    n_groups = qq_ref.shape[0]
    per = SUBLANES // t_new
    c = SUBLANES
    dk, dv = HGRN_DK, HGRN_DV
    masks = _chunk_masks(c)
    r_i = lax.broadcasted_iota(jnp.int32, (c, c), 0)
    c_i = lax.broadcasted_iota(jnp.int32, (c, c), 1)
    tri = jnp.where(c_i <= r_i, 1.0, 0.0).astype(BF16)
    row = lax.broadcasted_iota(jnp.int32, (c, 1), 0)
    pending = []
    for g in range(n_groups):
        lf = lf_ref[g]
        for j in range(per):
            sq = g * per + j
            mine = (row >= j * t_new) & (row < (j + 1) * t_new)
            keep = jnp.where(mine, 1.0, 0.0)
            lf2 = lf * (keep * LOG2_E)
            f_all = jnp.exp2(lf2)
            b_all = sum(jnp.dot(tri, part, preferred_element_type=F32) for part in _split3(lf2))
            for h in range(HGRN_HEADS):
                ks = slice(h * dk, (h + 1) * dk)
                vs = slice(h * dv, (h + 1) * dv)
                st = s0_ref[sq, h]
                st_b = st.astype(BF16)
                v_b = (vv_ref[g, :, vs].astype(F32) * keep).astype(BF16)
                parts, kt, decay = _hgrn_pairs(
                    qq_ref[g, :, ks].astype(F32) * keep, kk_ref[g, :, ks].astype(F32) * keep,
                    b_all[:, ks], f_all[:, ks],
                    lambda qt, st_b=st_b: jnp.dot(qt, st_b, preferred_element_type=F32))
                decay_col = jnp.broadcast_to(decay, (dv, dk)).T
                s_ref[sq, h] = st * decay_col + lax.dot_general(kt, v_b, TN_DIMS, preferred_element_type=F32)
                pending.append((g, h, parts, v_b))
    m_heads = [[jnp.zeros((c, dv), F32)] * HGRN_HEADS for _ in range(n_groups)]
    for g, h, parts, v_b in pending:
        vs = slice(h * dv, (h + 1) * dv)
        o = _hgrn_combine(parts, v_b, masks)
        m_heads[g][h] = m_heads[g][h] + _head_norm_gate(o, og_ref[...], sg_ref[g, :, vs]).astype(F32)
    m1 = jnp.concatenate([jnp.concatenate(m_heads[g], axis=1) for g in range(n_groups)], axis=0).astype(BF16)
    y = jnp.dot(m1, w_ref[...], preferred_element_type=F32)
    for g in range(n_groups):
        y_ref[g] = y0_ref[g] + y[g * c:(g + 1) * c, :]


def _c_scan_sample(qq, kk, lf, vv, sg, y0, s0, og, w_bf, t_new):
    groups, rows, d = y0.shape
    per = SUBLANES // t_new
    gs = SCAN_GROUPS
    assert groups % gs == 0
    key_w, val_w = qq.shape[2], vv.shape[2]
    gmap = lambda g: (g, 0, 0)
    const = lambda g: (0, 0)
    smap = lambda g: (g, 0, 0, 0)
    sspec = pl.BlockSpec((gs * per, HGRN_HEADS, HGRN_DK, HGRN_DV), smap)
    return pl.pallas_call(
        functools.partial(_c_scan_sample_kernel, t_new=t_new),
        grid=(groups // gs,),
        in_specs=[pl.BlockSpec((gs, rows, key_w), gmap), pl.BlockSpec((gs, rows, key_w), gmap),
                  pl.BlockSpec((gs, rows, key_w), gmap), pl.BlockSpec((gs, rows, val_w), gmap),
                  pl.BlockSpec((gs, rows, val_w), gmap), pl.BlockSpec((gs, rows, d), gmap), sspec,
                  pl.BlockSpec((1, HGRN_DV), const), pl.BlockSpec(w_bf.shape, const)],
        out_specs=(pl.BlockSpec((gs, rows, d), gmap), sspec),
        out_shape=(jax.ShapeDtypeStruct((groups, rows, d), F32),
                   jax.ShapeDtypeStruct(s0.shape, F32)),
        compiler_params=_cparams(1),
        name="c_scan_sample",
    )(qq, kk, lf, vv, sg, y0, s0, og, w_bf)


def kernel(x_prompt, x_sample, cache_k, cache_v, state_conv, state_hgrn, page_table, norm_0, w_in_0, q_norm_0, k_norm_0, conv_w_0, conv_b_0, conv_ln_g_0, conv_ln_b_0, w_out_0, norm_1, w_in_1, lb_logits, o_norm_1, w_out_1):
    b, s, d = x_prompt.shape
    n_seq, t_new, _ = x_sample.shape
    n_tok = n_seq * t_new
    hist = CONV_LEN - 1
    gsum = jnp.kron(jnp.eye(ATT_WIDTH // 2 // ATT_HEAD_DIM, dtype=F32),
                    jnp.full((ATT_HEAD_DIM, ATT_HEAD_DIM), 1.0 / ATT_HEAD_DIM, F32)).astype(BF16)
    qg = jnp.tile(q_norm_0, ATT_HEADS)[None]
    kg = jnp.tile(k_norm_0, ATT_HEADS)[None]
    w_in_0b = w_in_0.astype(BF16)
    w_out_0b = w_out_0.astype(BF16)
    w_in_1b = w_in_1.astype(BF16)
    w_out_1b = w_out_1.astype(BF16)
    tab = _conv_tap_table(conv_w_0)
    conv_args = (tab, conv_b_0[None], conv_ln_g_0[None], conv_ln_b_0[None], w_out_0b)
    heads = lambda a, lead: a.reshape(lead + (ATT_HEADS, ATT_HEAD_DIM))

    qs_s, k_s, v_s, _, _, sga_s, u_s, sgc_s, _ = _ab_in(
        x_sample.reshape(n_tok, d), norm_0[None], w_in_0b, qg, kg, gsum, n_tok)
    q3 = lambda a: a.reshape(n_seq, t_new, a.shape[-1])

    def per_head(a, dt):
        a = a.reshape(n_seq, t_new, ATT_HEADS, ATT_HEAD_DIM).transpose(0, 2, 1, 3).astype(dt)
        return jnp.pad(a, ((0, 0), (0, 0), (0, SUBLANES - t_new), (0, 0)))

    sample_att = (page_table, per_head(qs_s, BF16), per_head(k_s, BF16), per_head(v_s, BF16), per_head(sga_s, F32),
                  cache_k.transpose(0, 2, 3, 1), cache_v.transpose(0, 2, 3, 1), t_new)

    qs, kt_p, vt_p, kb, vbt, sga, u_p, sgc, kmean, matt_s = _ab_in(
        x_prompt.reshape(b * s, d), norm_0[None], w_in_0b, qg, kg, gsum, ROW_TILE, seq_len=s, sample=sample_att)
    heads_t = lambda a: a.reshape(b, ATT_HEADS, ATT_HEAD_DIM, s).transpose(0, 3, 1, 2)
    r3 = lambda a: a.reshape(b, s, a.shape[-1])
    matt = _moba_prompt(r3(qs), r3(kb), vbt, kmean.reshape(b, s // MOBA_BLOCK, ATT_WIDTH), r3(sga))
    y0 = _ab_out_prompt(x_prompt, matt, r3(u_p), r3(sgc), *conv_args, ROW_TILE)
    y_prompt, hgrn_prompt = _c_layer_prompt(y0, norm_1[None], w_in_1b, lb_logits, o_norm_1[None], w_out_1b, SCAN_TILE)

    matt_s = matt_s[:, :, :t_new, :].transpose(0, 2, 1, 3)
    y0_s = _ab_out_sample(x_sample.reshape(n_tok, d), matt_s.reshape(n_tok, ATT_WIDTH), q3(u_s), state_conv,
                          sgc_s.astype(F32), *conv_args)
    qq_s, kk_s, lf_s, vv_s, sg_s = _c_in(y0_s, norm_1[None], w_in_1b, lb_logits, n_tok)
    g8 = lambda a: a.reshape(n_tok // SUBLANES, SUBLANES, a.shape[-1])
    y_s, hgrn_sample = _c_scan_sample(g8(qq_s), g8(kk_s), g8(lf_s), g8(vv_s), g8(sg_s), g8(y0_s), state_hgrn,
                                      o_norm_1[None], w_out_1b, t_new)

    conv_prompt = r3(u_p)[:, s - hist:, :]
    conv_sample = jnp.concatenate([state_conv[:, t_new:, :], q3(u_s)], axis=1)
    return (y_prompt, y_s.reshape(n_seq, t_new, d),
            heads_t(kt_p), heads_t(vt_p), heads(k_s, (n_seq, t_new)), heads(v_s, (n_seq, t_new)),
            conv_prompt, conv_sample, hgrn_prompt, hgrn_sample)
```

```python
import functools

import jax
import jax.numpy as jnp
from jax import lax
from jax.experimental import pallas as pl
from jax.experimental.pallas import tpu as pltpu

EPS = 1e-6
ATT_HEADS = 8
ATT_HEAD_DIM = 64
ATT_WIDTH = ATT_HEADS * ATT_HEAD_DIM
MOBA_BLOCK = 256
MOBA_TOPK = 3
MOBA_PAIR = 2
PAGE_SLOTS = 3
PAGE_SIZE = 128
CONV_CH = 512
CONV_LEN = 31
HGRN_HEADS = 8
HGRN_DK = 128
HGRN_DV = 128
LAYER_C = 1

LANES = 128
SUBLANES = 8
LOG2_E = 1.4426950408889634
HEADS_PER_VREG = LANES // ATT_HEAD_DIM
NEG = -1e30
VMEM_LIMIT = 56 * 1024 * 1024
ROW_TILE = 512
SCAN_TILE = 512
SCAN_GROUPS = 4

F32 = jnp.float32
BF16 = jnp.bfloat16
NT_DIMS = (((1,), (1,)), ((), ()))
TN_DIMS = (((0,), (0,)), ((), ()))


def _silu(x):
    return x * jax.nn.sigmoid(x)


def _cparams(n_axes):
    return pltpu.CompilerParams(dimension_semantics=("arbitrary",) * n_axes,
                                vmem_limit_bytes=VMEM_LIMIT)


def _ab_in_stages(h_ref, w_ref, qg_ref, kg_ref, gsum_ref,
                  qs_ref, k_ref, v_ref, kb_ref, vb_ref, sga_ref, u_ref, sgc_ref, kmean_ref, kv_transposed):
    def proj(c):
        return jnp.dot(h_ref[...], w_ref[:, c * ATT_WIDTH:(c + 1) * ATT_WIDTH], preferred_element_type=F32)

    def head_rms(z, g):
        zz = (z * z).astype(BF16)
        half = ATT_WIDTH // 2
        ms_h = jnp.concatenate(
            [jnp.dot(zz[:, :half], gsum_ref[...], preferred_element_type=F32),
             jnp.dot(zz[:, half:], gsum_ref[...], preferred_element_type=F32)], axis=1)
        return z * lax.rsqrt(ms_h + EPS) * g

    def stage_q():
        q = head_rms(proj(0), qg_ref[...])
        qs_ref[...] = (q * (ATT_HEAD_DIM ** -0.5 * LOG2_E)).astype(BF16)

    def stage_k():
        k = head_rms(proj(1), kg_ref[...])
        if kv_transposed:
            k_ref[0] = k.T
        else:
            k_ref[...] = k
        kb_ref[...] = k.astype(BF16)
        tm = k.shape[0]
        kmean_ref[0] = jnp.mean(k.reshape(tm // MOBA_BLOCK, MOBA_BLOCK, ATT_WIDTH), axis=1)

    def stage_v():
        v = proj(2)
        if kv_transposed:
            v_t = v.T
            v_ref[0] = v_t
            vb_ref[0] = v_t.astype(BF16)
        else:
            v_ref[...] = v
            vb_ref[...] = v.astype(BF16)

    def stage_gate_att():
        sga_ref[...] = _silu(proj(3)).astype(BF16)

    def stage_glu():
        u_ref[...] = proj(4) * jax.nn.sigmoid(proj(5))

    def stage_gate_conv():
        sgc_ref[...] = _silu(proj(6)).astype(BF16)

    return [stage_q, stage_k, stage_v, stage_gate_att, stage_glu, stage_gate_conv]


def _normalise_rows(x_ref, ng_ref, h_ref):
    x = x_ref[...]
    ms = jnp.mean(x * x, axis=-1, keepdims=True)
    h_ref[...] = (x * lax.rsqrt(ms + EPS) * ng_ref[...]).astype(BF16)


def _ab_in_kernel(x_ref, ng_ref, w_ref, qg_ref, kg_ref, gsum_ref, *rest, kv_transposed):
    *outs, h_ref = rest
    _normalise_rows(x_ref, ng_ref, h_ref)
    for stage in _ab_in_stages(h_ref, w_ref, qg_ref, kg_ref, gsum_ref, *outs, kv_transposed):
        stage()


def _ab_in(x2d, ng, w_bf, qg, kg, gsum, tm, seq_len=None, sample=None):
    t, d = x2d.shape
    wn = w_bf.shape[1]
    steps = t // tm
    row = lambda i, *_: (i, 0)
    const = lambda i, *_: (0, 0)
    tile = lambda dt: jax.ShapeDtypeStruct((t, ATT_WIDTH), dt)
    tspec = pl.BlockSpec((tm, ATT_WIDTH), row)
    if seq_len is None:
        kv_shape, kv_spec = tile, tspec
    else:
        per_seq = seq_len // tm
        kv_shape = lambda dt: jax.ShapeDtypeStruct((t // seq_len, ATT_WIDTH, seq_len), dt)
        kv_spec = pl.BlockSpec((1, ATT_WIDTH, tm), lambda i, *_: (i // per_seq, 0, i % per_seq))
    out_shape = (tile(BF16), kv_shape(F32), kv_shape(F32), tile(BF16), kv_shape(BF16), tile(BF16), tile(F32),
                 tile(BF16), jax.ShapeDtypeStruct((steps, tm // MOBA_BLOCK, ATT_WIDTH), F32))
    in_specs = [pl.BlockSpec((tm, d), row), pl.BlockSpec((1, d), const),
                pl.BlockSpec((d, wn), const, pipeline_mode=pl.Buffered(1)),
                pl.BlockSpec((1, ATT_WIDTH), const), pl.BlockSpec((1, ATT_WIDTH), const),
                pl.BlockSpec(gsum.shape, const)]
    out_specs = ((tspec, kv_spec, kv_spec, tspec, kv_spec) + (tspec,) * 3
                 + (pl.BlockSpec((1, tm // MOBA_BLOCK, ATT_WIDTH), lambda i, *_: (i, 0, 0)),))
    scratch = [pltpu.VMEM((tm, d), BF16)]
    kv_t = seq_len is not None
    if sample is None:
        return pl.pallas_call(
            functools.partial(_ab_in_kernel, kv_transposed=kv_t),
            grid=(steps,), in_specs=in_specs, out_specs=out_specs, out_shape=out_shape,
            scratch_shapes=scratch, compiler_params=_cparams(1), name="ab_in",
        )(x2d, ng, w_bf, qg, kg, gsum)

    page_table, sq, skn, svn, ssga, cache_kt, cache_vt, t_new = sample
    n_seq, _, rows, _ = sq.shape
    n_pages = page_table.shape[1]
    past = n_pages * PAGE_SIZE
    per_step = n_seq // steps
    assert past % MOBA_BLOCK == 0, "the sample group's past must end on a MoBA block boundary"
    assert per_step * steps == n_seq and n_seq >= PAGE_SLOTS
    sspec = pl.BlockSpec((per_step, ATT_HEADS, rows, ATT_HEAD_DIM), lambda i, *_: (i, 0, 0, 0))
    page_buf = pltpu.VMEM((PAGE_SLOTS, ATT_HEADS, ATT_HEAD_DIM, past), F32)
    grid_spec = pltpu.PrefetchScalarGridSpec(
        num_scalar_prefetch=1,
        grid=(steps,),
        in_specs=in_specs + [sspec] * 4 + [pl.BlockSpec(memory_space=pl.ANY)] * 2,
        out_specs=out_specs + (sspec,),
        scratch_shapes=scratch + [page_buf, page_buf, pltpu.SemaphoreType.DMA((2, PAGE_SLOTS))],
    )
    return pl.pallas_call(
        functools.partial(_ab_in_sample_kernel, kv_transposed=kv_t, n_pages=n_pages, t_new=t_new,
                          per_step=per_step),
        grid_spec=grid_spec,
        out_shape=out_shape + (jax.ShapeDtypeStruct(sq.shape, F32),),
        compiler_params=_cparams(1),
        name="ab_in_sample",
    )(page_table, x2d, ng, w_bf, qg, kg, gsum, sq, skn, svn, ssga, cache_kt, cache_vt)


def _moba_prompt_kernel(qs_ref, kb_ref, vt_ref, kmean_ref, sga_ref, o_ref, bias_ref, s_ref, e_ref, *, nb):
    pair = pl.program_id(2)
    blk = MOBA_BLOCK
    cols = HEADS_PER_VREG * blk
    km_parts = _split3(kmean_ref[0])
    lane = lax.broadcasted_iota(jnp.int32, (blk, LANES), 1)
    key_i = lax.broadcasted_iota(jnp.int32, (blk, cols), 0)
    qry_i = lax.broadcasted_iota(jnp.int32, (blk, cols), 1) % blk
    causal_bias = jnp.where(key_i <= qry_i, 0.0, NEG).astype(F32)
    b_i = lax.broadcasted_iota(jnp.int32, (nb, blk), 0)

    def block_kv(jb):
        rows = pl.ds(pl.multiple_of(jb * blk, blk), blk)
        return kb_ref[0, rows, :], vt_ref[0, :, rows]

    def prepare(c):
        j = MOBA_PAIR * pair + c
        q2 = qs_ref[0, c * blk:(c + 1) * blk, :]
        valid = b_i < j
        qhs = []
        for hh in range(HEADS_PER_VREG):
            head_mask = jnp.where((lane // ATT_HEAD_DIM) == hh, 1.0, 0.0).astype(BF16)
            qh = q2 * head_mask
            g = sum(lax.dot_general(part, qh, NT_DIMS, preferred_element_type=F32) for part in km_parts)
            g = jnp.where(valid, g, -jnp.inf)
            rank = jnp.zeros((nb, blk), F32)
            for i in range(nb):
                gi = g[i:i + 1, :]
                rank = rank + jnp.where(gi > g, 1.0, jnp.where((gi == g) & (b_i > i), 1.0, 0.0))
            sel_bias = jnp.where(valid & (rank < MOBA_TOPK), 0.0, NEG)
            bias_ref[c, :, :, hh * blk:(hh + 1) * blk] = jnp.broadcast_to(
                sel_bias[:, None, :], (nb, SUBLANES, blk))
            qhs.append(qh)
        return jnp.concatenate(qhs, axis=0)

    def block_bias(c, jb):
        return jnp.broadcast_to(bias_ref[c, jb][None], (blk // SUBLANES, SUBLANES, cols)).reshape(blk, cols)

    def scores(q_cols, k_rows):
        return lax.dot_general(k_rows, q_cols, NT_DIMS, preferred_element_type=F32)

    def update(carry, s, v_cols):
        if carry is None:
            m_new = jnp.max(s, axis=0, keepdims=True)
            p = jnp.exp2(s - m_new)
            return (m_new, jnp.sum(p, axis=0, keepdims=True),
                    jnp.dot(v_cols, p.astype(BF16), preferred_element_type=F32))
        m, l, acc = carry
        m_new = jnp.maximum(m, jnp.max(s, axis=0, keepdims=True))
        alpha = jnp.exp2(m - m_new)
        p = jnp.exp2(s - m_new)
        l = alpha * l + jnp.sum(p, axis=0, keepdims=True)
        acc = alpha * acc + jnp.dot(v_cols, p.astype(BF16), preferred_element_type=F32)
        return m_new, l, acc

    span = MOBA_PAIR * blk
    last = nb // MOBA_PAIR - 1

    def step_keys(i):
        return pl.ds(pl.multiple_of(jnp.minimum(i, last) * span, span), span)

    def issue(i, buf):
        kj = kb_ref[0, step_keys(i), :]
        first = jnp.minimum(i, last) * MOBA_PAIR
        for c in range(MOBA_PAIR):
            bias = jnp.concatenate([block_bias(c, first + u) for u in range(MOBA_PAIR)], axis=0)
            s_ref[buf, c] = scores(q_cols[c], kj) + bias

    def consume(i, buf, carry):
        vj = vt_ref[0, :, step_keys(i)]
        out = []
        for c in range(MOBA_PAIR):
            m, l, acc = carry[c]
            m_new = jnp.maximum(m, jnp.max(s_ref[buf, c], axis=0, keepdims=True))
            alpha = jnp.exp2(m - m_new)
            again = jnp.concatenate([s_ref[buf, c, u * blk:(u + 1) * blk, :] for u in range(MOBA_PAIR)], axis=0)
            p = jnp.exp2(again - m_new)
            l = alpha * l + jnp.sum(p, axis=0, keepdims=True)
            acc = alpha * acc + jnp.dot(vj, p.astype(BF16), preferred_element_type=F32)
            out.append((m_new, l, acc))
        return tuple(out)

    def body(t, carry):
        i0 = 2 * t
        issue(i0 + 1, 1)
        carry = consume(i0, 0, carry)
        issue(i0 + 2, 0)
        return consume(i0 + 1, 1, carry)

    q_cols = [prepare(c) for c in range(MOBA_PAIR)]
    own_kv = [block_kv(MOBA_PAIR * pair + c) for c in range(MOBA_PAIR)]
    cross = [(c, e) for c in range(1, MOBA_PAIR) for e in range(c)]
    for c in range(MOBA_PAIR):
        e_ref[c] = scores(q_cols[c], own_kv[c][0])
    for n, (c, e) in enumerate(cross):
        e_ref[MOBA_PAIR + n] = scores(q_cols[c], own_kv[e][0])
    issue(0, 0)
    state = [update(None, e_ref[c] + causal_bias, own_kv[c][1]) for c in range(MOBA_PAIR)]
    for n, (c, e) in enumerate(cross):
        state[c] = update(state[c], e_ref[MOBA_PAIR + n] + block_bias(c, MOBA_PAIR * pair + e), own_kv[e][1])
    state = lax.fori_loop(0, pair // 2, body, tuple(state))
    state = lax.cond(lax.rem(pair, 2) == 1, lambda st: consume(pair - 1, 0, st), lambda st: st, state)
    for c in range(MOBA_PAIR):
        _, l, acc = state[c]
        o_t = acc / l
        o_t = jnp.concatenate([o_t[hh * ATT_HEAD_DIM:(hh + 1) * ATT_HEAD_DIM, hh * blk:(hh + 1) * blk]
                               for hh in range(HEADS_PER_VREG)], axis=0)
        rows = slice(c * blk, (c + 1) * blk)
        o_ref[0, rows, :] = (o_t.T * sga_ref[0, rows, :].astype(F32)).astype(BF16)


def _moba_prompt(qs, kb, vt, kmean, sga):
    b, s, _ = qs.shape
    nb = s // MOBA_BLOCK
    assert nb % MOBA_PAIR == 0
    npair = ATT_WIDTH // LANES
    rows = MOBA_PAIR * MOBA_BLOCK
    qmap = lambda bi, hp, j: (bi, j, hp)
    kvmap = lambda bi, hp, j: (bi, 0, hp)
    return pl.pallas_call(
        functools.partial(_moba_prompt_kernel, nb=nb),
        grid=(b, npair, nb // MOBA_PAIR),
        in_specs=[pl.BlockSpec((1, rows, LANES), qmap),
                  pl.BlockSpec((1, s, LANES), kvmap),
                  pl.BlockSpec((1, LANES, s), lambda bi, hp, j: (bi, hp, 0)),
                  pl.BlockSpec((1, nb, LANES), kvmap),
                  pl.BlockSpec((1, rows, LANES), qmap)],
        out_specs=pl.BlockSpec((1, rows, LANES), qmap),
        out_shape=jax.ShapeDtypeStruct((b, s, ATT_WIDTH), BF16),
        scratch_shapes=[pltpu.VMEM((MOBA_PAIR, nb, SUBLANES, HEADS_PER_VREG * MOBA_BLOCK), F32),
                        pltpu.VMEM((2, MOBA_PAIR, rows, HEADS_PER_VREG * MOBA_BLOCK), F32),
                        pltpu.VMEM((MOBA_PAIR * (MOBA_PAIR + 1) // 2, MOBA_BLOCK, HEADS_PER_VREG * MOBA_BLOCK), F32)],
        compiler_params=_cparams(3),
        name="moba_prompt",
    )(qs, kb, vt, kmean, sga)


CONV_HALO = 32
CONV_ROWS = 64


CONV_LEAD = CONV_HALO - (CONV_LEN - 1)
CONV_SPAN = -(-(CONV_LEAD + SUBLANES - 1 + CONV_LEN) // SUBLANES)


def _conv_tap_table(conv_w):
    m = jnp.arange(CONV_SPAN)[:, None, None]
    s = jnp.arange(SUBLANES)[None, :, None]
    r = jnp.arange(SUBLANES)[None, None, :]
    idx = SUBLANES * m + s - r - CONV_LEAD
    ok = (idx >= 0) & (idx < CONV_LEN)
    tab = jnp.where(ok[..., None], conv_w[jnp.clip(idx, 0, CONV_LEN - 1)], 0.0)
    return tab.reshape(CONV_SPAN * SUBLANES, SUBLANES, conv_w.shape[-1])


def _tap_used(ms):
    lo = ms - (SUBLANES - 1) - CONV_LEAD
    return lo + SUBLANES - 1 >= 0 and lo < CONV_LEN


def _conv_rows(xpad_ref, ybuf_ref, tab_ref, cb_ref, n_rows):
    groups = CONV_ROWS // SUBLANES
    for lt in range(CONV_CH // LANES):
        ls = slice(lt * LANES, (lt + 1) * LANES)

        for c in range(n_rows // CONV_ROWS):
            r0 = c * CONV_ROWS
            win_ref = xpad_ref.at[pl.ds(r0, CONV_ROWS + CONV_SPAN * SUBLANES)]
            taps = {ms: tab_ref[ms, :, ls] for ms in range(CONV_SPAN * SUBLANES) if _tap_used(ms)}
            acc = [jnp.broadcast_to(cb_ref[:, ls], (SUBLANES, LANES))] * groups
            for rho in range(CONV_ROWS + CONV_SPAN * SUBLANES):
                users = [(rho // SUBLANES - m, SUBLANES * m + rho % SUBLANES) for m in range(CONV_SPAN)]
                users = [(g, ms) for g, ms in users if 0 <= g < groups and ms in taps]
                if not users:
                    continue
                xb = win_ref[pl.ds(rho, 1), ls]
                for g, ms in users:
                    acc[g] = acc[g] + taps[ms] * xb
            ybuf_ref[pl.ds(r0, CONV_ROWS), ls] = jnp.concatenate(acc, axis=0)


def _ln_silu_gate(y, lg_ref, lb_ref, sgc):
    mu = jnp.mean(y, axis=-1, keepdims=True)
    cen = y - mu
    var = jnp.mean(cen * cen, axis=-1, keepdims=True)
    yn = cen * lax.rsqrt(var + EPS) * lg_ref[...] + lb_ref[...]
    return (_silu(yn) * sgc.astype(F32)).astype(BF16)


def _ab_out_prompt_kernel(x_ref, matt_ref, u_ref, halo_ref, sgc_ref, tab_ref, cb_ref, lg_ref, lb_ref, w_ref,
                          y_ref, xpad_ref, ybuf_ref, m_ref):
    t = pl.program_id(1)
    tm = u_ref.shape[1]
    halo = halo_ref[0]
    xpad_ref[0:CONV_HALO, :] = jnp.where(t == 0, jnp.zeros_like(halo), halo)
    xpad_ref[CONV_HALO:CONV_HALO + tm, :] = u_ref[0]
    y_att = x_ref[0] + jnp.dot(matt_ref[0], w_ref[0:ATT_WIDTH, :], preferred_element_type=F32)
    _conv_rows(xpad_ref, ybuf_ref, tab_ref, cb_ref, tm)
    for c in range(tm // CONV_ROWS):
        rows = slice(c * CONV_ROWS, (c + 1) * CONV_ROWS)
        m_ref[rows, :] = _ln_silu_gate(ybuf_ref[rows, :], lg_ref, lb_ref, sgc_ref[0, rows, :])
    y_ref[0] = y_att + jnp.dot(m_ref[...], w_ref[ATT_WIDTH:, :], preferred_element_type=F32)


def _ab_out_prompt(x, matt, u, sgc, tab, cb, lg, lb, w_bf, tm):
    b, s, d = x.shape
    hpt = tm // CONV_HALO
    tmap = lambda bi, t: (bi, t, 0)
    hmap = lambda bi, t: (bi, jnp.maximum(t * hpt - 1, 0), 0)
    const = lambda bi, t: (0, 0)
    return pl.pallas_call(
        _ab_out_prompt_kernel,
        grid=(b, s // tm),
        in_specs=[pl.BlockSpec((1, tm, d), tmap), pl.BlockSpec((1, tm, ATT_WIDTH), tmap),
                  pl.BlockSpec((1, tm, CONV_CH), tmap), pl.BlockSpec((1, CONV_HALO, CONV_CH), hmap),
                  pl.BlockSpec((1, tm, CONV_CH), tmap),
                  pl.BlockSpec(tab.shape, lambda bi, t: (0, 0, 0)), pl.BlockSpec((1, CONV_CH), const),
                  pl.BlockSpec((1, CONV_CH), const), pl.BlockSpec((1, CONV_CH), const),
                  pl.BlockSpec(w_bf.shape, const)],
        out_specs=pl.BlockSpec((1, tm, d), tmap),
        out_shape=jax.ShapeDtypeStruct((b, s, d), F32),
        scratch_shapes=[pltpu.VMEM((tm + CONV_SPAN * SUBLANES, CONV_CH), F32),
                        pltpu.VMEM((tm, CONV_CH), F32),
                        pltpu.VMEM((tm, CONV_CH), BF16)],
        compiler_params=_cparams(2),
        name="ab_out_prompt",
    )(x, matt, u, u, sgc, tab, cb, lg, lb, w_bf)


C_COLS = 512


def _c_in_kernel(x_ref, ng_ref, w_ref, lbl_ref, qq_ref, kk_ref, lf_ref, vv_ref, sg_ref):
    x = x_ref[...]
    ms = jnp.mean(x * x, axis=-1, keepdims=True)
    h = (x * lax.rsqrt(ms + EPS) * ng_ref[...]).astype(BF16)
    p = jax.nn.softmax(lbl_ref[...], axis=0)
    lb = jnp.sum(p[0:LAYER_C + 1], axis=0, keepdims=True) - p[0:1]
    key_w = qq_ref.shape[1]
    per = key_w // C_COLS
    for c in range(per):
        cs = slice(c * C_COLS, (c + 1) * C_COLS)

        def proj(g, cs=cs):
            return jnp.dot(h, w_ref[:, g * key_w + cs.start:g * key_w + cs.stop], preferred_element_type=F32)

        qq_ref[:, cs] = _silu(proj(0)).astype(BF16)
        lbc = lb[:, cs]
        f = lbc + (1.0 - lbc) * jax.nn.sigmoid(proj(1))
        lf_ref[:, cs] = jnp.log(f)
        kk_ref[:, cs] = (1.0 - f).astype(BF16)
        vv_ref[:, cs] = proj(2).astype(BF16)
        sg_ref[:, cs] = _silu(proj(3)).astype(BF16)


def _c_in(x2d, ng, w_bf, lbl, tm):
    t, d = x2d.shape
    key_w = lbl.shape[1]
    row = lambda i: (i, 0)
    const = lambda i: (0, 0)
    tile = lambda dt: jax.ShapeDtypeStruct((t, key_w), dt)
    tspec = pl.BlockSpec((tm, key_w), row)
    return pl.pallas_call(
        _c_in_kernel,
        grid=(t // tm,),
        in_specs=[pl.BlockSpec((tm, d), row), pl.BlockSpec((1, d), const), pl.BlockSpec(w_bf.shape, const),
                  pl.BlockSpec(lbl.shape, const)],
        out_specs=(tspec,) * 5,
        out_shape=(tile(BF16), tile(BF16), tile(F32), tile(BF16), tile(BF16)),
        compiler_params=_cparams(1),
        name="c_in",
    )(x2d, ng, w_bf, lbl)


HGRN_SUB = SUBLANES


def _split3(x):
    hi = x.astype(BF16)
    r1 = x - hi.astype(F32)
    mid = r1.astype(BF16)
    lo = (r1 - mid.astype(F32)).astype(BF16)
    return hi, mid, lo


def _chunk_masks(c):
    t_i = lax.broadcasted_iota(jnp.int32, (c, c), 0)
    s_i = lax.broadcasted_iota(jnp.int32, (c, c), 1)
    same = (t_i // HGRN_SUB) == (s_i // HGRN_SUB)
    diag = [t_i - s_i == d for d in range(HGRN_SUB)]
    col_group = [(s_i // HGRN_SUB) == a for a in range(c // HGRN_SUB)]
    return diag, col_group, same & (s_i <= t_i)


def _hgrn_pairs(q, k, b, f, o_inter_fn):
    c = q.shape[0]
    n_sub = c // HGRN_SUB
    b_last = b[c - 1:c, :]
    o_inter = o_inter_fn((q * jnp.exp2(b)).astype(BF16))
    kt = (k * jnp.exp2(b_last - b)).astype(BF16)

    k_b = k.astype(BF16)
    f3 = f.reshape(n_sub, HGRN_SUB, f.shape[1])
    stack = [q.astype(BF16)]
    prod = f3
    for d in range(1, HGRN_SUB):
        if d > 1:
            prod = prod * pltpu.roll(f3, d - 1, 1)
        stack.append((q * prod.reshape(f.shape)).astype(BF16))
    zd = lax.dot_general(jnp.concatenate(stack, axis=0), k_b, NT_DIMS, preferred_element_type=F32)

    zo = None
    if n_sub > 1:
        b3 = b.reshape(n_sub, HGRN_SUB, b.shape[1])
        b_end = jnp.broadcast_to(b3[:, HGRN_SUB - 1:HGRN_SUB, :], b3.shape).reshape(b.shape)
        ksc = (k * jnp.exp2(b_end - b)).astype(BF16)
        stack = []
        for g in range(1, n_sub):
            r = g * HGRN_SUB
            stack.append((q[r:, :] * jnp.exp2(b[r:, :] - b[r - 1:r, :])).astype(BF16))
        zo = lax.dot_general(jnp.concatenate(stack, axis=0), ksc, NT_DIMS, preferred_element_type=F32)
    return (o_inter, zd, zo), kt, jnp.exp2(b_last)


def _hgrn_combine(parts, v_b, masks):
    o_inter, zd, zo = parts
    c = o_inter.shape[0]
    n_sub = c // HGRN_SUB
    diag, col_group, same_lower = masks
    a = jnp.zeros((c, c), F32)
    row = 0
    for g in range(1, n_sub):
        r = g * HGRN_SUB
        part = jnp.concatenate([jnp.zeros((r, c), F32), zo[row:row + c - r, :]], axis=0)
        a = jnp.where(col_group[g - 1], part, a)
        row += c - r
    a_in = zd[0:c, :]
    for d in range(1, HGRN_SUB):
        a_in = jnp.where(diag[d], zd[d * c:(d + 1) * c, :], a_in)
    a = jnp.where(same_lower, a_in, a)
    return o_inter + jnp.dot(a.astype(BF16), v_b, preferred_element_type=F32)


def _head_norm_gate(o, og, sg):
    ms = jnp.mean(o * o, axis=-1, keepdims=True)
    return ((o * lax.rsqrt(ms + EPS) * og) * sg.astype(F32)).astype(BF16)


HGRN_CHUNK = 64


def _c_layer_prompt_kernel(xn_ref, y0_ref, ng_ref, w_ref, lbl_ref, og_ref, wo_ref, y_ref, s_ref,
                           st_ref, m_ref, zd_ref, zo_ref, hn_ref, lb_ref, wh_ref, qq_s, kk_s, lf_s, vv_s, sg_s,
                           *, tiles_per_seq):
    g = pl.program_id(0)
    t = lax.rem(g, tiles_per_seq)
    cur = lax.rem(g, 2)
    nxt = 1 - cur
    ts = y0_ref.shape[1]
    c = HGRN_CHUNK
    dk, dv = HGRN_DK, HGRN_DV
    assert ts // c == HGRN_HEADS and dk == dv == LANES

    p = jax.nn.softmax(lbl_ref[...], axis=0)
    lb = jnp.sum(p[0:LAYER_C + 1], axis=0, keepdims=True) - p[0:1]
    for h in range(HGRN_HEADS):
        lb_ref[h] = lb[:, h * dk:(h + 1) * dk]

    def project_qf(hd, buf):
        z = jnp.dot(hn_ref[...], wh_ref[hd, :, 0:2 * dk], preferred_element_type=F32)
        qq_s[buf, hd] = _silu(z[:, 0:dk]).astype(BF16)
        lbh = lb_ref[hd]
        f = lbh + (1.0 - lbh) * jax.nn.sigmoid(z[:, dk:])
        lf_s[buf, hd] = jnp.log(f)
        kk_s[buf, hd] = (1.0 - f).astype(BF16)

    def project_vg(hd, buf):
        z = jnp.dot(hn_ref[...], wh_ref[hd, :, 2 * dk:], preferred_element_type=F32)
        vv_s[buf, hd] = z[:, 0:dv].astype(BF16)
        sg_s[buf, hd] = _silu(z[:, dv:]).astype(BF16)

    def project_head(hd, buf):
        project_qf(hd, buf)
        project_vg(hd, buf)

    def normalise(x):
        ms = jnp.mean(x * x, axis=-1, keepdims=True)
        hn_ref[...] = (x * lax.rsqrt(ms + EPS) * ng_ref[...]).astype(BF16)

    @pl.when(g == 0)
    def _():
        key_w = HGRN_HEADS * dk
        for h in range(HGRN_HEADS):
            for part in range(w_ref.shape[1] // key_w):
                wh_ref[h, :, part * dk:(part + 1) * dk] = w_ref[:, part * key_w + h * dk:part * key_w + (h + 1) * dk]
        normalise(y0_ref[0])
        for h in range(HGRN_HEADS):
            project_head(h, 0)

    @pl.when(t == 0)
    def _():
        st_ref[...] = jnp.zeros_like(st_ref)

    normalise(xn_ref[0])

    def chunk(ci, carry):
        r0 = pl.multiple_of(ci * c, c)
        rows = pl.ds(r0, c)
        heads = range(HGRN_HEADS)
        q_c = [qq_s[cur, h, rows, :] for h in heads]
        k_c = [kk_s[cur, h, rows, :] for h in heads]
        v_c = [vv_s[cur, h, rows, :] for h in heads]
        sg_c = [sg_s[cur, h, rows, :] for h in heads]
        lf2 = jnp.concatenate([lf_s[cur, h, rows, :] for h in heads], axis=1) * LOG2_E
        project_qf(ci, nxt)
        masks = _chunk_masks(c)
        r_i = lax.broadcasted_iota(jnp.int32, (c, c), 0)
        c_i = lax.broadcasted_iota(jnp.int32, (c, c), 1)
        tri = jnp.where(c_i <= r_i, 1.0, 0.0).astype(BF16)
        f_all = jnp.exp2(lf2)
        b_all = sum(jnp.dot(tri, part, preferred_element_type=F32) for part in _split3(lf2))
        parts = []
        for h in range(HGRN_HEADS):
            ks = slice(h * dk, (h + 1) * dk)
            st = st_ref[h]
            st_b = st.astype(BF16)
            (o_inter, zd, zo), kt, decay = _hgrn_pairs(
                q_c[h].astype(F32), k_c[h].astype(F32), b_all[:, ks], f_all[:, ks],
                lambda qt, st_b=st_b: lax.dot_general(qt, st_b, NT_DIMS, preferred_element_type=F32))
            zd_ref[h] = zd
            zo_ref[h] = zo
            parts.append(o_inter)
            st_ref[h] = st * decay + lax.dot_general(v_c[h], kt, TN_DIMS, preferred_element_type=F32)
        project_vg(ci, nxt)
        for h in range(HGRN_HEADS):
            o = _hgrn_combine((parts[h], zd_ref[h], zo_ref[h]), v_c[h], masks)
            m_ref[rows, h * dv:(h + 1) * dv] = _head_norm_gate(o, og_ref[...], sg_c[h])
        return carry

    lax.fori_loop(0, ts // c, chunk, 0)
    y_ref[0] = y0_ref[0] + jnp.dot(m_ref[...], wo_ref[...], preferred_element_type=F32)

    @pl.when(t == tiles_per_seq - 1)
    def _():
        for h in range(HGRN_HEADS):
            s_ref[0, h] = st_ref[h].T


def _c_layer_prompt(y0, ng, w_bf, lbl, og, wo_bf, ts):
    b, s, d = y0.shape
    per_head = w_bf.shape[1] // HGRN_HEADS
    tps = s // ts
    n_tiles = b * tps
    n_sub = HGRN_CHUNK // HGRN_SUB
    cur_map = lambda g: (g // tps, g % tps, 0)
    nxt_map = lambda g: (jnp.minimum(g + 1, n_tiles - 1) // tps, jnp.minimum(g + 1, n_tiles - 1) % tps, 0)
    const = lambda g: (0, 0)
    head_buf = lambda dt: pltpu.VMEM((2, HGRN_HEADS, ts, LANES), dt)
    return pl.pallas_call(
        functools.partial(_c_layer_prompt_kernel, tiles_per_seq=tps),
        grid=(n_tiles,),
        in_specs=[pl.BlockSpec((1, ts, d), nxt_map), pl.BlockSpec((1, ts, d), cur_map),
                  pl.BlockSpec((1, d), const),
                  pl.BlockSpec(w_bf.shape, const, pipeline_mode=pl.Buffered(1)),
                  pl.BlockSpec(lbl.shape, const), pl.BlockSpec((1, HGRN_DV), const),
                  pl.BlockSpec(wo_bf.shape, const)],
        out_specs=(pl.BlockSpec((1, ts, d), cur_map),
                   pl.BlockSpec((1, HGRN_HEADS, HGRN_DK, HGRN_DV), lambda g: (g // tps, 0, 0, 0))),
        out_shape=(jax.ShapeDtypeStruct((b, s, d), F32),
                   jax.ShapeDtypeStruct((b, HGRN_HEADS, HGRN_DK, HGRN_DV), F32)),
        scratch_shapes=[pltpu.VMEM((HGRN_HEADS, HGRN_DV, HGRN_DK), F32),
                        pltpu.VMEM((ts, HGRN_HEADS * HGRN_DV), BF16),
                        pltpu.VMEM((HGRN_HEADS, HGRN_SUB * HGRN_CHUNK, HGRN_CHUNK), F32),
                        pltpu.VMEM((HGRN_HEADS, (n_sub * (n_sub - 1) // 2) * HGRN_SUB, HGRN_CHUNK), F32),
                        pltpu.VMEM((ts, d), BF16),
                        pltpu.VMEM((HGRN_HEADS, 1, LANES), F32),
                        pltpu.VMEM((HGRN_HEADS, d, per_head), BF16),
                        head_buf(BF16), head_buf(BF16), head_buf(F32), head_buf(BF16), head_buf(BF16)],
        compiler_params=_cparams(1),
        name="c_layer_prompt",
    )(y0, y0, ng, w_bf, lbl, og, wo_bf)


def _page_copies(pt_ref, ck_hbm, cv_hbm, kbuf, vbuf, sem, seq, sl, n_pages):
    out = []
    for p in range(n_pages):
        page = pt_ref[seq, p]
        toks = pl.ds(p * PAGE_SIZE, PAGE_SIZE)
        out.append(pltpu.make_async_copy(ck_hbm.at[page], kbuf.at[sl, :, :, toks], sem.at[0, sl]))
        out.append(pltpu.make_async_copy(cv_hbm.at[page], vbuf.at[sl, :, :, toks], sem.at[1, sl]))
    return out


def _ab_in_sample_kernel(pt_ref, x_ref, ng_ref, w_ref, qg_ref, kg_ref, gsum_ref,
                         sq_ref, skn_ref, svn_ref, ssga_ref, ck_hbm, cv_hbm, *rest,
                         kv_transposed, n_pages, t_new, per_step):
    *outs, so_ref, h_ref, kbuf, vbuf, sem = rest
    step = pl.program_id(0)
    n_seq = pl.num_programs(0) * per_step
    copies = functools.partial(_page_copies, pt_ref, ck_hbm, cv_hbm, kbuf, vbuf, sem, n_pages=n_pages)

    _normalise_rows(x_ref, ng_ref, h_ref)
    stages = _ab_in_stages(h_ref, w_ref, qg_ref, kg_ref, gsum_ref, *outs, kv_transposed)
    share = -(-len(stages) // per_step)

    ahead = PAGE_SLOTS - 1

    @pl.when(step == 0)
    def _():
        for s0 in range(ahead):
            for cp in copies(seq=s0, sl=s0):
                cp.start()

    for j in range(per_step):
        seq = step * per_step + j
        slot = lax.rem(seq, PAGE_SLOTS)

        @pl.when(seq + ahead < n_seq)
        def _():
            for cp in copies(seq=seq + ahead, sl=lax.rem(seq + ahead, PAGE_SLOTS)):
                cp.start()

        for cp in copies(seq=seq, sl=slot):
            cp.wait()

        def between(j=j):
            for stage in stages[j * share:(j + 1) * share]:
                stage()

        _sample_attention(j, slot, sq_ref, skn_ref, svn_ref, ssga_ref, so_ref, kbuf, vbuf,
                          t_new, n_pages * PAGE_SIZE // MOBA_BLOCK, between)


def _sample_attention(sq, slot, qs_ref, kn_ref, vn_ref, sga_ref, o_ref, kbuf, vbuf, t_new, nb, between):
    rows = qs_ref.shape[2]
    b_i = lax.broadcasted_iota(jnp.int32, (rows, LANES), 1)
    o_row = lax.broadcasted_iota(jnp.int32, (rows, rows), 0)
    o_col = lax.broadcasted_iota(jnp.int32, (rows, rows), 1)
    raw = []
    for h in range(ATT_HEADS):
        q_h = qs_ref[sq, h]
        raw.append((jnp.dot(q_h, kbuf[slot, h].astype(BF16), preferred_element_type=F32),
                    lax.dot_general(q_h, kn_ref[sq, h], NT_DIMS, preferred_element_type=F32)))
    between()
    for h in range(ATT_HEADS):
        s, s_own = raw[h]
        blocks = [s[:, j * MOBA_BLOCK:(j + 1) * MOBA_BLOCK] for j in range(nb)]
        gates = [jnp.sum(blk, axis=1, keepdims=True) for blk in blocks]
        gate = jnp.full((rows, LANES), -jnp.inf, F32)
        for i in range(nb):
            gate = jnp.where(b_i == i, gates[i], gate)
        rank = jnp.zeros((rows, LANES), F32)
        for i in range(nb):
            gi = gates[i]
            rank = rank + jnp.where(gi > gate, 1.0, jnp.where((gi == gate) & (b_i > i), 1.0, 0.0))
        sel_bias = jnp.where(rank < min(MOBA_TOPK, nb), 0.0, NEG)
        s = jnp.concatenate([blocks[j] + sel_bias[:, j:j + 1] for j in range(nb)], axis=1)

        s_own = jnp.where((o_col <= o_row) & (o_col < t_new), s_own, NEG)

        m = jnp.maximum(jnp.max(s, axis=1, keepdims=True), jnp.max(s_own, axis=1, keepdims=True))
        p = jnp.exp2(s - m)
        p_own = jnp.exp2(s_own - m)
        l = jnp.sum(p, axis=1, keepdims=True) + jnp.sum(p_own, axis=1, keepdims=True)
        o = (lax.dot_general(p.astype(BF16), vbuf[slot, h].astype(BF16), NT_DIMS, preferred_element_type=F32)
             + jnp.dot(p_own.astype(BF16), vn_ref[sq, h], preferred_element_type=F32)) / l
        o_ref[sq, h] = o * sga_ref[sq, h]


def _ab_out_sample_kernel(x_ref, matt_ref, u_ref, buf_ref, sgc_ref, tab_ref, cb_ref, lg_ref, lb_ref, w_ref,
                          y_ref, xpad_ref, m_ref):
    n_seq, t_new, _ = u_ref.shape
    hist = buf_ref.shape[1]
    per = SUBLANES // t_new
    assert per * t_new == SUBLANES and n_seq % per == 0 and hist == CONV_LEN - 1
    span = CONV_SPAN * SUBLANES
    xpad_ref[...] = jnp.zeros_like(xpad_ref)
    row8 = lax.broadcasted_iota(jnp.int32, (SUBLANES, CONV_CH), 0)

    def group(gi, carry):
        y8 = jnp.zeros((SUBLANES, CONV_CH), F32)
        for j in range(per):
            sq = gi * per + j
            off = j * t_new
            xpad_ref[j, CONV_HALO - hist + off:CONV_HALO + off, :] = buf_ref[sq]
            xpad_ref[j, CONV_HALO + off:CONV_HALO + off + t_new, :] = u_ref[sq]
            cols = []
            for lt in range(CONV_CH // LANES):
                ls = slice(lt * LANES, (lt + 1) * LANES)
                acc = jnp.broadcast_to(cb_ref[:, ls], (SUBLANES, LANES))
                for ms in range(span):
                    if _tap_used(ms):
                        acc = acc + tab_ref[ms, :, ls] * xpad_ref[j, pl.ds(ms, 1), ls]
                cols.append(acc)
            yj = jnp.concatenate(cols, axis=1)
            y8 = jnp.where((row8 >= off) & (row8 < off + t_new), yj, y8)
        r0 = pl.multiple_of(gi * SUBLANES, SUBLANES)
        rows = pl.ds(r0, SUBLANES)
        m_ref[rows, 0:ATT_WIDTH] = matt_ref[rows, :]
        m_ref[rows, ATT_WIDTH:] = _ln_silu_gate(y8, lg_ref, lb_ref, sgc_ref[rows, :]).astype(F32)
        return carry

    lax.fori_loop(0, n_seq // per, group, 0)
    y_ref[...] = x_ref[...] + jnp.dot(m_ref[...].astype(BF16), w_ref[...], preferred_element_type=F32)


def _ab_out_sample(x2d, matt2d, u3, buf, sgc2d, tab, cb, lg, lb, w_bf):
    t, d = x2d.shape
    n_seq, t_new, _ = u3.shape
    per = SUBLANES // t_new
    vm = lambda: pl.BlockSpec(memory_space=pltpu.VMEM)
    return pl.pallas_call(
        _ab_out_sample_kernel,
        in_specs=[vm() for _ in range(10)],
        out_specs=vm(),
        out_shape=jax.ShapeDtypeStruct((t, d), F32),
        scratch_shapes=[pltpu.VMEM((per, CONV_HALO + 2 * SUBLANES, CONV_CH), F32),
                        pltpu.VMEM((t, ATT_WIDTH + CONV_CH), F32)],
        compiler_params=pltpu.CompilerParams(vmem_limit_bytes=VMEM_LIMIT),
        name="ab_out_sample",
    )(x2d, matt2d, u3, buf, sgc2d, tab, cb, lg, lb, w_bf)


def _c_scan_sample_kernel(qq_ref, kk_ref, lf_ref, vv_ref, sg_ref, y0_ref, s0_ref, og_ref, w_ref,
                          y_ref, s_ref, *, t_new):
    n_groups = qq_ref.shape[0]
    per = SUBLANES // t_new
    c = SUBLANES
    dk, dv = HGRN_DK, HGRN_DV
    masks = _chunk_masks(c)
    r_i = lax.broadcasted_iota(jnp.int32, (c, c), 0)
    c_i = lax.broadcasted_iota(jnp.int32, (c, c), 1)
    tri = jnp.where(c_i <= r_i, 1.0, 0.0).astype(BF16)
    row = lax.broadcasted_iota(jnp.int32, (c, 1), 0)
    pending = []
    for g in range(n_groups):
        lf = lf_ref[g]
        for j in range(per):
            sq = g * per + j
            mine = (row >= j * t_new) & (row < (j + 1) * t_new)
            keep = jnp.where(mine, 1.0, 0.0)
            lf2 = lf * (keep * LOG2_E)
            f_all = jnp.exp2(lf2)
            b_all = sum(jnp.dot(tri, part, preferred_element_type=F32) for part in _split3(lf2))
            for h in range(HGRN_HEADS):
                ks = slice(h * dk, (h + 1) * dk)
                vs = slice(h * dv, (h + 1) * dv)
                st = s0_ref[sq, h]
                st_b = st.astype(BF16)
                v_b = (vv_ref[g, :, vs].astype(F32) * keep).astype(BF16)
                parts, kt, decay = _hgrn_pairs(
                    qq_ref[g, :, ks].astype(F32) * keep, kk_ref[g, :, ks].astype(F32) * keep,
                    b_all[:, ks], f_all[:, ks],
                    lambda qt, st_b=st_b: jnp.dot(qt, st_b, preferred_element_type=F32))
                decay_col = jnp.broadcast_to(decay, (dv, dk)).T
                s_ref[sq, h] = st * decay_col + lax.dot_general(kt, v_b, TN_DIMS, preferred_element_type=F32)
                pending.append((g, h, parts, v_b))
    m_heads = [[jnp.zeros((c, dv), F32)] * HGRN_HEADS for _ in range(n_groups)]
    for g, h, parts, v_b in pending:
        vs = slice(h * dv, (h + 1) * dv)
        o = _hgrn_combine(parts, v_b, masks)
        m_heads[g][h] = m_heads[g][h] + _head_norm_gate(o, og_ref[...], sg_ref[g, :, vs]).astype(F32)
    m1 = jnp.concatenate([jnp.concatenate(m_heads[g], axis=1) for g in range(n_groups)], axis=0).astype(BF16)
    y = jnp.dot(m1, w_ref[...], preferred_element_type=F32)
    for g in range(n_groups):
        y_ref[g] = y0_ref[g] + y[g * c:(g + 1) * c, :]


def _c_scan_sample(qq, kk, lf, vv, sg, y0, s0, og, w_bf, t_new):
    groups, rows, d = y0.shape
    per = SUBLANES // t_new
    gs = SCAN_GROUPS
    assert groups % gs == 0
    key_w, val_w = qq.shape[2], vv.shape[2]
    gmap = lambda g: (g, 0, 0)
    const = lambda g: (0, 0)
    smap = lambda g: (g, 0, 0, 0)
    sspec = pl.BlockSpec((gs * per, HGRN_HEADS, HGRN_DK, HGRN_DV), smap)
    return pl.pallas_call(
        functools.partial(_c_scan_sample_kernel, t_new=t_new),
        grid=(groups // gs,),
        in_specs=[pl.BlockSpec((gs, rows, key_w), gmap), pl.BlockSpec((gs, rows, key_w), gmap),
                  pl.BlockSpec((gs, rows, key_w), gmap), pl.BlockSpec((gs, rows, val_w), gmap),
                  pl.BlockSpec((gs, rows, val_w), gmap), pl.BlockSpec((gs, rows, d), gmap), sspec,
                  pl.BlockSpec((1, HGRN_DV), const), pl.BlockSpec(w_bf.shape, const)],
        out_specs=(pl.BlockSpec((gs, rows, d), gmap), sspec),
        out_shape=(jax.ShapeDtypeStruct((groups, rows, d), F32),
                   jax.ShapeDtypeStruct(s0.shape, F32)),
        compiler_params=_cparams(1),
        name="c_scan_sample",
    )(qq, kk, lf, vv, sg, y0, s0, og, w_bf)


def kernel(x_prompt, x_sample, cache_k, cache_v, state_conv, state_hgrn, page_table, norm_0, w_in_0, q_norm_0, k_norm_0, conv_w_0, conv_b_0, conv_ln_g_0, conv_ln_b_0, w_out_0, norm_1, w_in_1, lb_logits, o_norm_1, w_out_1):
    b, s, d = x_prompt.shape
    n_seq, t_new, _ = x_sample.shape
    n_tok = n_seq * t_new
    hist = CONV_LEN - 1
    gsum = jnp.kron(jnp.eye(ATT_WIDTH // 2 // ATT_HEAD_DIM, dtype=F32),
                    jnp.full((ATT_HEAD_DIM, ATT_HEAD_DIM), 1.0 / ATT_HEAD_DIM, F32)).astype(BF16)
    qg = jnp.tile(q_norm_0, ATT_HEADS)[None]
    kg = jnp.tile(k_norm_0, ATT_HEADS)[None]
    w_in_0b = w_in_0.astype(BF16)
    w_out_0b = w_out_0.astype(BF16)
    w_in_1b = w_in_1.astype(BF16)
    w_out_1b = w_out_1.astype(BF16)
    tab = _conv_tap_table(conv_w_0)
    conv_args = (tab, conv_b_0[None], conv_ln_g_0[None], conv_ln_b_0[None], w_out_0b)
    heads = lambda a, lead: a.reshape(lead + (ATT_HEADS, ATT_HEAD_DIM))

    qs_s, k_s, v_s, _, _, sga_s, u_s, sgc_s, _ = _ab_in(
        x_sample.reshape(n_tok, d), norm_0[None], w_in_0b, qg, kg, gsum, n_tok)
    q3 = lambda a: a.reshape(n_seq, t_new, a.shape[-1])

    def per_head(a, dt):
        a = a.reshape(n_seq, t_new, ATT_HEADS, ATT_HEAD_DIM).transpose(0, 2, 1, 3).astype(dt)
        return jnp.pad(a, ((0, 0), (0, 0), (0, SUBLANES - t_new), (0, 0)))

    sample_att = (page_table, per_head(qs_s, BF16), per_head(k_s, BF16), per_head(v_s, BF16), per_head(sga_s, F32),
                  cache_k.transpose(0, 2, 3, 1), cache_v.transpose(0, 2, 3, 1), t_new)

    qs, kt_p, vt_p, kb, vbt, sga, u_p, sgc, kmean, matt_s = _ab_in(
        x_prompt.reshape(b * s, d), norm_0[None], w_in_0b, qg, kg, gsum, ROW_TILE, seq_len=s, sample=sample_att)
    heads_t = lambda a: a.reshape(b, ATT_HEADS, ATT_HEAD_DIM, s).transpose(0, 3, 1, 2)
    r3 = lambda a: a.reshape(b, s, a.shape[-1])
    matt = _moba_prompt(r3(qs), r3(kb), vbt, kmean.reshape(b, s // MOBA_BLOCK, ATT_WIDTH), r3(sga))
    y0 = _ab_out_prompt(x_prompt, matt, r3(u_p), r3(sgc), *conv_args, ROW_TILE)
    y_prompt, hgrn_prompt = _c_layer_prompt(y0, norm_1[None], w_in_1b, lb_logits, o_norm_1[None], w_out_1b, SCAN_TILE)

    matt_s = matt_s[:, :, :t_new, :].transpose(0, 2, 1, 3)
    y0_s = _ab_out_sample(x_sample.reshape(n_tok, d), matt_s.reshape(n_tok, ATT_WIDTH), q3(u_s), state_conv,
                          sgc_s.astype(F32), *conv_args)
    qq_s, kk_s, lf_s, vv_s, sg_s = _c_in(y0_s, norm_1[None], w_in_1b, lb_logits, n_tok)
    g8 = lambda a: a.reshape(n_tok // SUBLANES, SUBLANES, a.shape[-1])
    y_s, hgrn_sample = _c_scan_sample(g8(qq_s), g8(kk_s), g8(lf_s), g8(vv_s), g8(sg_s), g8(y0_s), state_hgrn,
                                      o_norm_1[None], w_out_1b, t_new)

    conv_prompt = r3(u_p)[:, s - hist:, :]
    conv_sample = jnp.concatenate([state_conv[:, t_new:, :], q3(u_s)], axis=1)
    return (y_prompt, y_s.reshape(n_seq, t_new, d),
            heads_t(kt_p), heads_t(vt_p), heads(k_s, (n_seq, t_new)), heads(v_s, (n_seq, t_new)),
            conv_prompt, conv_sample, hgrn_prompt, hgrn_sample)
```

```python
import functools

import jax
import jax.numpy as jnp
from jax import lax
from jax.experimental import pallas as pl
from jax.experimental.pallas import tpu as pltpu

EPS = 1e-6
ATT_HEADS = 8
ATT_HEAD_DIM = 64
ATT_WIDTH = ATT_HEADS * ATT_HEAD_DIM
MOBA_BLOCK = 256
MOBA_TOPK = 3
MOBA_PAIR = 2
PAGE_SLOTS = 3
PAGE_SIZE = 128
CONV_CH = 512
CONV_LEN = 31
HGRN_HEADS = 8
HGRN_DK = 128
HGRN_DV = 128
LAYER_C = 1

LANES = 128
SUBLANES = 8
LOG2_E = 1.4426950408889634
HEADS_PER_VREG = LANES // ATT_HEAD_DIM
NEG = -1e30
VMEM_LIMIT = 56 * 1024 * 1024
ROW_TILE = 512
SCAN_TILE = 512
SCAN_GROUPS = 4

F32 = jnp.float32
BF16 = jnp.bfloat16
NT_DIMS = (((1,), (1,)), ((), ()))
TN_DIMS = (((0,), (0,)), ((), ()))


def _silu(x):
    return x * jax.nn.sigmoid(x)


def _cparams(n_axes):
    return pltpu.CompilerParams(dimension_semantics=("arbitrary",) * n_axes,
                                vmem_limit_bytes=VMEM_LIMIT)


def _ab_in_stages(h_ref, w_ref, qg_ref, kg_ref, gsum_ref,
                  qs_ref, k_ref, v_ref, kb_ref, vb_ref, sga_ref, u_ref, sgc_ref, kmean_ref, kv_transposed):
    def proj(c):
        return jnp.dot(h_ref[...], w_ref[:, c * ATT_WIDTH:(c + 1) * ATT_WIDTH], preferred_element_type=F32)

    def head_rms(z, g):
        zz = (z * z).astype(BF16)
        half = ATT_WIDTH // 2
        ms_h = jnp.concatenate(
            [jnp.dot(zz[:, :half], gsum_ref[...], preferred_element_type=F32),
             jnp.dot(zz[:, half:], gsum_ref[...], preferred_element_type=F32)], axis=1)
        return z * lax.rsqrt(ms_h + EPS) * g

    def stage_q():
        q = head_rms(proj(0), qg_ref[...])
        qs_ref[...] = (q * (ATT_HEAD_DIM ** -0.5 * LOG2_E)).astype(BF16)

    def stage_k():
        k = head_rms(proj(1), kg_ref[...])
        if kv_transposed:
            k_ref[0] = k.T
        else:
            k_ref[...] = k
        kb_ref[...] = k.astype(BF16)
        tm = k.shape[0]
        kmean_ref[0] = jnp.mean(k.reshape(tm // MOBA_BLOCK, MOBA_BLOCK, ATT_WIDTH), axis=1)

    def stage_v():
        v = proj(2)
        if kv_transposed:
            v_t = v.T
            v_ref[0] = v_t
            vb_ref[0] = v_t.astype(BF16)
        else:
            v_ref[...] = v
            vb_ref[...] = v.astype(BF16)

    def stage_gate_att():
        sga_ref[...] = _silu(proj(3)).astype(BF16)

    def stage_glu():
        u_ref[...] = proj(4) * jax.nn.sigmoid(proj(5))

    def stage_gate_conv():
        sgc_ref[...] = _silu(proj(6)).astype(BF16)

    return [stage_q, stage_k, stage_v, stage_gate_att, stage_glu, stage_gate_conv]


def _normalise_rows(x_ref, ng_ref, h_ref):
    x = x_ref[...]
    ms = jnp.mean(x * x, axis=-1, keepdims=True)
    h_ref[...] = (x * lax.rsqrt(ms + EPS) * ng_ref[...]).astype(BF16)


def _ab_in_kernel(x_ref, ng_ref, w_ref, qg_ref, kg_ref, gsum_ref, *rest, kv_transposed):
    *outs, h_ref = rest
    _normalise_rows(x_ref, ng_ref, h_ref)
    for stage in _ab_in_stages(h_ref, w_ref, qg_ref, kg_ref, gsum_ref, *outs, kv_transposed):
        stage()


def _ab_in(x2d, ng, w_bf, qg, kg, gsum, tm, seq_len=None, sample=None):
    t, d = x2d.shape
    wn = w_bf.shape[1]
    steps = t // tm
    row = lambda i, *_: (i, 0)
    const = lambda i, *_: (0, 0)
    tile = lambda dt: jax.ShapeDtypeStruct((t, ATT_WIDTH), dt)
    tspec = pl.BlockSpec((tm, ATT_WIDTH), row)
    if seq_len is None:
        kv_shape, kv_spec = tile, tspec
    else:
        per_seq = seq_len // tm
        kv_shape = lambda dt: jax.ShapeDtypeStruct((t // seq_len, ATT_WIDTH, seq_len), dt)
        kv_spec = pl.BlockSpec((1, ATT_WIDTH, tm), lambda i, *_: (i // per_seq, 0, i % per_seq))
    out_shape = (tile(BF16), kv_shape(F32), kv_shape(F32), tile(BF16), kv_shape(BF16), tile(BF16), tile(F32),
                 tile(BF16), jax.ShapeDtypeStruct((steps, tm // MOBA_BLOCK, ATT_WIDTH), F32))
    in_specs = [pl.BlockSpec((tm, d), row), pl.BlockSpec((1, d), const),
                pl.BlockSpec((d, wn), const, pipeline_mode=pl.Buffered(1)),
                pl.BlockSpec((1, ATT_WIDTH), const), pl.BlockSpec((1, ATT_WIDTH), const),
                pl.BlockSpec(gsum.shape, const)]
    out_specs = ((tspec, kv_spec, kv_spec, tspec, kv_spec) + (tspec,) * 3
                 + (pl.BlockSpec((1, tm // MOBA_BLOCK, ATT_WIDTH), lambda i, *_: (i, 0, 0)),))
    scratch = [pltpu.VMEM((tm, d), BF16)]
    kv_t = seq_len is not None
    if sample is None:
        return pl.pallas_call(
            functools.partial(_ab_in_kernel, kv_transposed=kv_t),
            grid=(steps,), in_specs=in_specs, out_specs=out_specs, out_shape=out_shape,
            scratch_shapes=scratch, compiler_params=_cparams(1), name="ab_in",
        )(x2d, ng, w_bf, qg, kg, gsum)

    page_table, sq, skn, svn, ssga, cache_kt, cache_vt, t_new = sample
    n_seq, _, rows, _ = sq.shape
    n_pages = page_table.shape[1]
    past = n_pages * PAGE_SIZE
    per_step = n_seq // steps
    assert past % MOBA_BLOCK == 0, "the sample group's past must end on a MoBA block boundary"
    assert per_step * steps == n_seq and n_seq >= PAGE_SLOTS
    sspec = pl.BlockSpec((per_step, ATT_HEADS, rows, ATT_HEAD_DIM), lambda i, *_: (i, 0, 0, 0))
    page_buf = pltpu.VMEM((PAGE_SLOTS, ATT_HEADS, ATT_HEAD_DIM, past), F32)
    grid_spec = pltpu.PrefetchScalarGridSpec(
        num_scalar_prefetch=1,
        grid=(steps,),
        in_specs=in_specs + [sspec] * 4 + [pl.BlockSpec(memory_space=pl.ANY)] * 2,
        out_specs=out_specs + (sspec,),
        scratch_shapes=scratch + [page_buf, page_buf, pltpu.SemaphoreType.DMA((2, PAGE_SLOTS))],
    )
    return pl.pallas_call(
        functools.partial(_ab_in_sample_kernel, kv_transposed=kv_t, n_pages=n_pages, t_new=t_new,
                          per_step=per_step),
        grid_spec=grid_spec,
        out_shape=out_shape + (jax.ShapeDtypeStruct(sq.shape, F32),),
        compiler_params=_cparams(1),
        name="ab_in_sample",
    )(page_table, x2d, ng, w_bf, qg, kg, gsum, sq, skn, svn, ssga, cache_kt, cache_vt)


def _moba_prompt_kernel(qs_ref, kb_ref, vt_ref, kmean_ref, sga_ref, o_ref, bias_ref, s_ref, e_ref, *, nb):
    pair = pl.program_id(2)
    blk = MOBA_BLOCK
    cols = HEADS_PER_VREG * blk
    km_parts = _split3(kmean_ref[0])
    lane = lax.broadcasted_iota(jnp.int32, (blk, LANES), 1)
    key_i = lax.broadcasted_iota(jnp.int32, (blk, cols), 0)
    qry_i = lax.broadcasted_iota(jnp.int32, (blk, cols), 1) % blk
    causal_bias = jnp.where(key_i <= qry_i, 0.0, NEG).astype(F32)
    b_i = lax.broadcasted_iota(jnp.int32, (nb, blk), 0)

    def block_kv(jb):
        rows = pl.ds(pl.multiple_of(jb * blk, blk), blk)
        return kb_ref[0, rows, :], vt_ref[0, :, rows]

    def prepare(c):
        j = MOBA_PAIR * pair + c
        q2 = qs_ref[0, c * blk:(c + 1) * blk, :]
        valid = b_i < j
        qhs = []
        for hh in range(HEADS_PER_VREG):
            head_mask = jnp.where((lane // ATT_HEAD_DIM) == hh, 1.0, 0.0).astype(BF16)
            qh = q2 * head_mask
            g = sum(lax.dot_general(part, qh, NT_DIMS, preferred_element_type=F32) for part in km_parts)
            g = jnp.where(valid, g, -jnp.inf)
            rank = jnp.zeros((nb, blk), F32)
            for i in range(nb):
                gi = g[i:i + 1, :]
                rank = rank + jnp.where(gi > g, 1.0, jnp.where((gi == g) & (b_i > i), 1.0, 0.0))
            sel_bias = jnp.where(valid & (rank < MOBA_TOPK), 0.0, NEG)
            bias_ref[c, :, :, hh * blk:(hh + 1) * blk] = jnp.broadcast_to(
                sel_bias[:, None, :], (nb, SUBLANES, blk))
            qhs.append(qh)
        return jnp.concatenate(qhs, axis=0)

    def block_bias(c, jb):
        return jnp.broadcast_to(bias_ref[c, jb][None], (blk // SUBLANES, SUBLANES, cols)).reshape(blk, cols)

    def scores(q_cols, k_rows):
        return lax.dot_general(k_rows, q_cols, NT_DIMS, preferred_element_type=F32)

    def update(carry, s, v_cols):
        if carry is None:
            m_new = jnp.max(s, axis=0, keepdims=True)
            p = jnp.exp2(s - m_new)
            return (m_new, jnp.sum(p, axis=0, keepdims=True),
                    jnp.dot(v_cols, p.astype(BF16), preferred_element_type=F32))
        m, l, acc = carry
        m_new = jnp.maximum(m, jnp.max(s, axis=0, keepdims=True))
        alpha = jnp.exp2(m - m_new)
        p = jnp.exp2(s - m_new)
        l = alpha * l + jnp.sum(p, axis=0, keepdims=True)
        acc = alpha * acc + jnp.dot(v_cols, p.astype(BF16), preferred_element_type=F32)
        return m_new, l, acc

    span = MOBA_PAIR * blk
    last = nb // MOBA_PAIR - 1

    def step_keys(i):
        return pl.ds(pl.multiple_of(jnp.minimum(i, last) * span, span), span)

    def issue(i, buf):
        kj = kb_ref[0, step_keys(i), :]
        first = jnp.minimum(i, last) * MOBA_PAIR
        for c in range(MOBA_PAIR):
            bias = jnp.concatenate([block_bias(c, first + u) for u in range(MOBA_PAIR)], axis=0)
            s_ref[buf, c] = scores(q_cols[c], kj) + bias

    def consume(i, buf, carry):
        vj = vt_ref[0, :, step_keys(i)]
        out = []
        for c in range(MOBA_PAIR):
            m, l, acc = carry[c]
            m_new = jnp.maximum(m, jnp.max(s_ref[buf, c], axis=0, keepdims=True))
            alpha = jnp.exp2(m - m_new)
            again = jnp.concatenate([s_ref[buf, c, u * blk:(u + 1) * blk, :] for u in range(MOBA_PAIR)], axis=0)
            p = jnp.exp2(again - m_new)
            l = alpha * l + jnp.sum(p, axis=0, keepdims=True)
            acc = alpha * acc + jnp.dot(vj, p.astype(BF16), preferred_element_type=F32)
            out.append((m_new, l, acc))
        return tuple(out)

    def body(t, carry):
        i0 = 2 * t
        issue(i0 + 1, 1)
        carry = consume(i0, 0, carry)
        issue(i0 + 2, 0)
        return consume(i0 + 1, 1, carry)

    q_cols = [prepare(c) for c in range(MOBA_PAIR)]
    own_kv = [block_kv(MOBA_PAIR * pair + c) for c in range(MOBA_PAIR)]
    cross = [(c, e) for c in range(1, MOBA_PAIR) for e in range(c)]
    for c in range(MOBA_PAIR):
        e_ref[c] = scores(q_cols[c], own_kv[c][0])
    for n, (c, e) in enumerate(cross):
        e_ref[MOBA_PAIR + n] = scores(q_cols[c], own_kv[e][0])
    issue(0, 0)
    state = [update(None, e_ref[c] + causal_bias, own_kv[c][1]) for c in range(MOBA_PAIR)]
    for n, (c, e) in enumerate(cross):
        state[c] = update(state[c], e_ref[MOBA_PAIR + n] + block_bias(c, MOBA_PAIR * pair + e), own_kv[e][1])
    state = lax.fori_loop(0, pair // 2, body, tuple(state))
    state = lax.cond(lax.rem(pair, 2) == 1, lambda st: consume(pair - 1, 0, st), lambda st: st, state)
    for c in range(MOBA_PAIR):
        _, l, acc = state[c]
        o_t = acc / l
        o_t = jnp.concatenate([o_t[hh * ATT_HEAD_DIM:(hh + 1) * ATT_HEAD_DIM, hh * blk:(hh + 1) * blk]
                               for hh in range(HEADS_PER_VREG)], axis=0)
        rows = slice(c * blk, (c + 1) * blk)
        o_ref[0, rows, :] = (o_t.T * sga_ref[0, rows, :].astype(F32)).astype(BF16)


def _moba_prompt(qs, kb, vt, kmean, sga):
    b, s, _ = qs.shape
    nb = s // MOBA_BLOCK
    assert nb % MOBA_PAIR == 0
    npair = ATT_WIDTH // LANES
    rows = MOBA_PAIR * MOBA_BLOCK
    qmap = lambda bi, hp, j: (bi, j, hp)
    kvmap = lambda bi, hp, j: (bi, 0, hp)
    return pl.pallas_call(
        functools.partial(_moba_prompt_kernel, nb=nb),
        grid=(b, npair, nb // MOBA_PAIR),
        in_specs=[pl.BlockSpec((1, rows, LANES), qmap),
                  pl.BlockSpec((1, s, LANES), kvmap),
                  pl.BlockSpec((1, LANES, s), lambda bi, hp, j: (bi, hp, 0)),
                  pl.BlockSpec((1, nb, LANES), kvmap),
                  pl.BlockSpec((1, rows, LANES), qmap)],
        out_specs=pl.BlockSpec((1, rows, LANES), qmap),
        out_shape=jax.ShapeDtypeStruct((b, s, ATT_WIDTH), BF16),
        scratch_shapes=[pltpu.VMEM((MOBA_PAIR, nb, SUBLANES, HEADS_PER_VREG * MOBA_BLOCK), F32),
                        pltpu.VMEM((2, MOBA_PAIR, rows, HEADS_PER_VREG * MOBA_BLOCK), F32),
                        pltpu.VMEM((MOBA_PAIR * (MOBA_PAIR + 1) // 2, MOBA_BLOCK, HEADS_PER_VREG * MOBA_BLOCK), F32)],
        compiler_params=_cparams(3),
        name="moba_prompt",
    )(qs, kb, vt, kmean, sga)


CONV_HALO = 32
CONV_ROWS = 64


CONV_LEAD = CONV_HALO - (CONV_LEN - 1)
CONV_SPAN = -(-(CONV_LEAD + SUBLANES - 1 + CONV_LEN) // SUBLANES)


def _conv_tap_table(conv_w):
    m = jnp.arange(CONV_SPAN)[:, None, None]
    s = jnp.arange(SUBLANES)[None, :, None]
    r = jnp.arange(SUBLANES)[None, None, :]
    idx = SUBLANES * m + s - r - CONV_LEAD
    ok = (idx >= 0) & (idx < CONV_LEN)
    tab = jnp.where(ok[..., None], conv_w[jnp.clip(idx, 0, CONV_LEN - 1)], 0.0)
    return tab.reshape(CONV_SPAN * SUBLANES, SUBLANES, conv_w.shape[-1])


def _tap_used(ms):
    lo = ms - (SUBLANES - 1) - CONV_LEAD
    return lo + SUBLANES - 1 >= 0 and lo < CONV_LEN


def _conv_rows(xpad_ref, ybuf_ref, tab_ref, cb_ref, n_rows):
    groups = CONV_ROWS // SUBLANES
    for lt in range(CONV_CH // LANES):
        ls = slice(lt * LANES, (lt + 1) * LANES)

        for c in range(n_rows // CONV_ROWS):
            r0 = c * CONV_ROWS
            win_ref = xpad_ref.at[pl.ds(r0, CONV_ROWS + CONV_SPAN * SUBLANES)]
            taps = {ms: tab_ref[ms, :, ls] for ms in range(CONV_SPAN * SUBLANES) if _tap_used(ms)}
            acc = [jnp.broadcast_to(cb_ref[:, ls], (SUBLANES, LANES))] * groups
            for rho in range(CONV_ROWS + CONV_SPAN * SUBLANES):
                users = [(rho // SUBLANES - m, SUBLANES * m + rho % SUBLANES) for m in range(CONV_SPAN)]
                users = [(g, ms) for g, ms in users if 0 <= g < groups and ms in taps]
                if not users:
                    continue
                xb = win_ref[pl.ds(rho, 1), ls]
                for g, ms in users:
                    acc[g] = acc[g] + taps[ms] * xb
            ybuf_ref[pl.ds(r0, CONV_ROWS), ls] = jnp.concatenate(acc, axis=0)


def _ln_silu_gate(y, lg_ref, lb_ref, sgc):
    mu = jnp.mean(y, axis=-1, keepdims=True)
    cen = y - mu
    var = jnp.mean(cen * cen, axis=-1, keepdims=True)
    yn = cen * lax.rsqrt(var + EPS) * lg_ref[...] + lb_ref[...]
    return (_silu(yn) * sgc.astype(F32)).astype(BF16)


def _ab_out_prompt_kernel(x_ref, matt_ref, u_ref, halo_ref, sgc_ref, tab_ref, cb_ref, lg_ref, lb_ref, w_ref,
                          y_ref, xpad_ref, ybuf_ref, m_ref):
    t = pl.program_id(1)
    tm = u_ref.shape[1]
    halo = halo_ref[0]
    xpad_ref[0:CONV_HALO, :] = jnp.where(t == 0, jnp.zeros_like(halo), halo)
    xpad_ref[CONV_HALO:CONV_HALO + tm, :] = u_ref[0]
    y_att = x_ref[0] + jnp.dot(matt_ref[0], w_ref[0:ATT_WIDTH, :], preferred_element_type=F32)
    _conv_rows(xpad_ref, ybuf_ref, tab_ref, cb_ref, tm)
    for c in range(tm // CONV_ROWS):
        rows = slice(c * CONV_ROWS, (c + 1) * CONV_ROWS)
        m_ref[rows, :] = _ln_silu_gate(ybuf_ref[rows, :], lg_ref, lb_ref, sgc_ref[0, rows, :])
    y_ref[0] = y_att + jnp.dot(m_ref[...], w_ref[ATT_WIDTH:, :], preferred_element_type=F32)


def _ab_out_prompt(x, matt, u, sgc, tab, cb, lg, lb, w_bf, tm):
    b, s, d = x.shape
    hpt = tm // CONV_HALO
    tmap = lambda bi, t: (bi, t, 0)
    hmap = lambda bi, t: (bi, jnp.maximum(t * hpt - 1, 0), 0)
    const = lambda bi, t: (0, 0)
    return pl.pallas_call(
        _ab_out_prompt_kernel,
        grid=(b, s // tm),
        in_specs=[pl.BlockSpec((1, tm, d), tmap), pl.BlockSpec((1, tm, ATT_WIDTH), tmap),
                  pl.BlockSpec((1, tm, CONV_CH), tmap), pl.BlockSpec((1, CONV_HALO, CONV_CH), hmap),
                  pl.BlockSpec((1, tm, CONV_CH), tmap),
                  pl.BlockSpec(tab.shape, lambda bi, t: (0, 0, 0)), pl.BlockSpec((1, CONV_CH), const),
                  pl.BlockSpec((1, CONV_CH), const), pl.BlockSpec((1, CONV_CH), const),
                  pl.BlockSpec(w_bf.shape, const)],
        out_specs=pl.BlockSpec((1, tm, d), tmap),
        out_shape=jax.ShapeDtypeStruct((b, s, d), F32),
        scratch_shapes=[pltpu.VMEM((tm + CONV_SPAN * SUBLANES, CONV_CH), F32),
                        pltpu.VMEM((tm, CONV_CH), F32),
                        pltpu.VMEM((tm, CONV_CH), BF16)],
        compiler_params=_cparams(2),
        name="ab_out_prompt",
    )(x, matt, u, u, sgc, tab, cb, lg, lb, w_bf)


C_COLS = 512


def _c_in_kernel(x_ref, ng_ref, w_ref, lbl_ref, qq_ref, kk_ref, lf_ref, vv_ref, sg_ref):
    x = x_ref[...]
    ms = jnp.mean(x * x, axis=-1, keepdims=True)
    h = (x * lax.rsqrt(ms + EPS) * ng_ref[...]).astype(BF16)
    p = jax.nn.softmax(lbl_ref[...], axis=0)
    lb = jnp.sum(p[0:LAYER_C + 1], axis=0, keepdims=True) - p[0:1]
    key_w = qq_ref.shape[1]
    per = key_w // C_COLS
    for c in range(per):
        cs = slice(c * C_COLS, (c + 1) * C_COLS)

        def proj(g, cs=cs):
            return jnp.dot(h, w_ref[:, g * key_w + cs.start:g * key_w + cs.stop], preferred_element_type=F32)

        qq_ref[:, cs] = _silu(proj(0)).astype(BF16)
        lbc = lb[:, cs]
        f = lbc + (1.0 - lbc) * jax.nn.sigmoid(proj(1))
        lf_ref[:, cs] = jnp.log(f)
        kk_ref[:, cs] = (1.0 - f).astype(BF16)
        vv_ref[:, cs] = proj(2).astype(BF16)
        sg_ref[:, cs] = _silu(proj(3)).astype(BF16)


def _c_in(x2d, ng, w_bf, lbl, tm):
    t, d = x2d.shape
    key_w = lbl.shape[1]
    row = lambda i: (i, 0)
    const = lambda i: (0, 0)
    tile = lambda dt: jax.ShapeDtypeStruct((t, key_w), dt)
    tspec = pl.BlockSpec((tm, key_w), row)
    return pl.pallas_call(
        _c_in_kernel,
        grid=(t // tm,),
        in_specs=[pl.BlockSpec((tm, d), row), pl.BlockSpec((1, d), const), pl.BlockSpec(w_bf.shape, const),
                  pl.BlockSpec(lbl.shape, const)],
        out_specs=(tspec,) * 5,
        out_shape=(tile(BF16), tile(BF16), tile(F32), tile(BF16), tile(BF16)),
        compiler_params=_cparams(1),
        name="c_in",
    )(x2d, ng, w_bf, lbl)


HGRN_SUB = SUBLANES


def _split3(x):
    hi = x.astype(BF16)
    r1 = x - hi.astype(F32)
    mid = r1.astype(BF16)
    lo = (r1 - mid.astype(F32)).astype(BF16)
    return hi, mid, lo


def _chunk_masks(c):
    t_i = lax.broadcasted_iota(jnp.int32, (c, c), 0)
    s_i = lax.broadcasted_iota(jnp.int32, (c, c), 1)
    same = (t_i // HGRN_SUB) == (s_i // HGRN_SUB)
    diag = [t_i - s_i == d for d in range(HGRN_SUB)]
    col_group = [(s_i // HGRN_SUB) == a for a in range(c // HGRN_SUB)]
    return diag, col_group, same & (s_i <= t_i)


def _hgrn_pairs(q, k, b, f, o_inter_fn):
    c = q.shape[0]
    n_sub = c // HGRN_SUB
    b_last = b[c - 1:c, :]
    o_inter = o_inter_fn((q * jnp.exp2(b)).astype(BF16))
    kt = (k * jnp.exp2(b_last - b)).astype(BF16)

    k_b = k.astype(BF16)
    f3 = f.reshape(n_sub, HGRN_SUB, f.shape[1])
    stack = [q.astype(BF16)]
    prod = f3
    for d in range(1, HGRN_SUB):
        if d > 1:
            prod = prod * pltpu.roll(f3, d - 1, 1)
        stack.append((q * prod.reshape(f.shape)).astype(BF16))
    zd = lax.dot_general(jnp.concatenate(stack, axis=0), k_b, NT_DIMS, preferred_element_type=F32)

    zo = None
    if n_sub > 1:
        b3 = b.reshape(n_sub, HGRN_SUB, b.shape[1])
        b_end = jnp.broadcast_to(b3[:, HGRN_SUB - 1:HGRN_SUB, :], b3.shape).reshape(b.shape)
        ksc = (k * jnp.exp2(b_end - b)).astype(BF16)
        stack = []
        for g in range(1, n_sub):
            r = g * HGRN_SUB
            stack.append((q[r:, :] * jnp.exp2(b[r:, :] - b[r - 1:r, :])).astype(BF16))
        zo = lax.dot_general(jnp.concatenate(stack, axis=0), ksc, NT_DIMS, preferred_element_type=F32)
    return (o_inter, zd, zo), kt, jnp.exp2(b_last)


def _hgrn_combine(parts, v_b, masks):
    o_inter, zd, zo = parts
    c = o_inter.shape[0]
    n_sub = c // HGRN_SUB
    diag, col_group, same_lower = masks
    a = jnp.zeros((c, c), F32)
    row = 0
    for g in range(1, n_sub):
        r = g * HGRN_SUB
        part = jnp.concatenate([jnp.zeros((r, c), F32), zo[row:row + c - r, :]], axis=0)
        a = jnp.where(col_group[g - 1], part, a)
        row += c - r
    a_in = zd[0:c, :]
    for d in range(1, HGRN_SUB):
        a_in = jnp.where(diag[d], zd[d * c:(d + 1) * c, :], a_in)
    a = jnp.where(same_lower, a_in, a)
    return o_inter + jnp.dot(a.astype(BF16), v_b, preferred_element_type=F32)


def _head_norm_gate(o, og, sg):
    ms = jnp.mean(o * o, axis=-1, keepdims=True)
    return ((o * lax.rsqrt(ms + EPS) * og) * sg.astype(F32)).astype(BF16)


HGRN_CHUNK = 64


def _c_layer_prompt_kernel(xn_ref, y0_ref, ng_ref, w_ref, lbl_ref, og_ref, wo_ref, y_ref, s_ref,
                           st_ref, m_ref, zd_ref, zo_ref, hn_ref, lb_ref, wh_ref, qq_s, kk_s, lf_s, vv_s, sg_s,
                           *, tiles_per_seq):
    g = pl.program_id(0)
    t = lax.rem(g, tiles_per_seq)
    cur = lax.rem(g, 2)
    nxt = 1 - cur
    ts = y0_ref.shape[1]
    c = HGRN_CHUNK
    dk, dv = HGRN_DK, HGRN_DV
    assert ts // c == HGRN_HEADS and dk == dv == LANES

    p = jax.nn.softmax(lbl_ref[...], axis=0)
    lb = jnp.sum(p[0:LAYER_C + 1], axis=0, keepdims=True) - p[0:1]
    for h in range(HGRN_HEADS):
        lb_ref[h] = lb[:, h * dk:(h + 1) * dk]

    def project_qf(hd, buf):
        z = jnp.dot(hn_ref[...], wh_ref[hd, :, 0:2 * dk], preferred_element_type=F32)
        qq_s[buf, hd] = _silu(z[:, 0:dk]).astype(BF16)
        lbh = lb_ref[hd]
        f = lbh + (1.0 - lbh) * jax.nn.sigmoid(z[:, dk:])
        lf_s[buf, hd] = jnp.log(f)
        kk_s[buf, hd] = (1.0 - f).astype(BF16)

    def project_vg(hd, buf):
        z = jnp.dot(hn_ref[...], wh_ref[hd, :, 2 * dk:], preferred_element_type=F32)
        vv_s[buf, hd] = z[:, 0:dv].astype(BF16)
        sg_s[buf, hd] = _silu(z[:, dv:]).astype(BF16)

    def project_head(hd, buf):
        project_qf(hd, buf)
        project_vg(hd, buf)

    def normalise(x):
        ms = jnp.mean(x * x, axis=-1, keepdims=True)
        hn_ref[...] = (x * lax.rsqrt(ms + EPS) * ng_ref[...]).astype(BF16)

    @pl.when(g == 0)
    def _():
        key_w = HGRN_HEADS * dk
        for h in range(HGRN_HEADS):
            for part in range(w_ref.shape[1] // key_w):
                wh_ref[h, :, part * dk:(part + 1) * dk] = w_ref[:, part * key_w + h * dk:part * key_w + (h + 1) * dk]
        normalise(y0_ref[0])
        for h in range(HGRN_HEADS):
            project_head(h, 0)

    @pl.when(t == 0)
    def _():
        st_ref[...] = jnp.zeros_like(st_ref)

    normalise(xn_ref[0])

    def chunk(ci, carry):
        r0 = pl.multiple_of(ci * c, c)
        rows = pl.ds(r0, c)
        heads = range(HGRN_HEADS)
        q_c = [qq_s[cur, h, rows, :] for h in heads]
        k_c = [kk_s[cur, h, rows, :] for h in heads]
        v_c = [vv_s[cur, h, rows, :] for h in heads]
        sg_c = [sg_s[cur, h, rows, :] for h in heads]
        lf2 = jnp.concatenate([lf_s[cur, h, rows, :] for h in heads], axis=1) * LOG2_E
        project_qf(ci, nxt)
        masks = _chunk_masks(c)
        r_i = lax.broadcasted_iota(jnp.int32, (c, c), 0)
        c_i = lax.broadcasted_iota(jnp.int32, (c, c), 1)
        tri = jnp.where(c_i <= r_i, 1.0, 0.0).astype(BF16)
        f_all = jnp.exp2(lf2)
        b_all = sum(jnp.dot(tri, part, preferred_element_type=F32) for part in _split3(lf2))
        parts = []
        for h in range(HGRN_HEADS):
            ks = slice(h * dk, (h + 1) * dk)
            st = st_ref[h]
            st_b = st.astype(BF16)
            (o_inter, zd, zo), kt, decay = _hgrn_pairs(
                q_c[h].astype(F32), k_c[h].astype(F32), b_all[:, ks], f_all[:, ks],
                lambda qt, st_b=st_b: lax.dot_general(qt, st_b, NT_DIMS, preferred_element_type=F32))
            zd_ref[h] = zd
            zo_ref[h] = zo
            parts.append(o_inter)
            st_ref[h] = st * decay + lax.dot_general(v_c[h], kt, TN_DIMS, preferred_element_type=F32)
        project_vg(ci, nxt)
        for h in range(HGRN_HEADS):
            o = _hgrn_combine((parts[h], zd_ref[h], zo_ref[h]), v_c[h], masks)
            m_ref[rows, h * dv:(h + 1) * dv] = _head_norm_gate(o, og_ref[...], sg_c[h])
        return carry

    lax.fori_loop(0, ts // c, chunk, 0)
    y_ref[0] = y0_ref[0] + jnp.dot(m_ref[...], wo_ref[...], preferred_element_type=F32)

    @pl.when(t == tiles_per_seq - 1)
    def _():
        for h in range(HGRN_HEADS):
            s_ref[0, h] = st_ref[h].T


def _c_layer_prompt(y0, ng, w_bf, lbl, og, wo_bf, ts):
    b, s, d = y0.shape
    per_head = w_bf.shape[1] // HGRN_HEADS
    tps = s // ts
    n_tiles = b * tps
    n_sub = HGRN_CHUNK // HGRN_SUB
    cur_map = lambda g: (g // tps, g % tps, 0)
    nxt_map = lambda g: (jnp.minimum(g + 1, n_tiles - 1) // tps, jnp.minimum(g + 1, n_tiles - 1) % tps, 0)
    const = lambda g: (0, 0)
    head_buf = lambda dt: pltpu.VMEM((2, HGRN_HEADS, ts, LANES), dt)
    return pl.pallas_call(
        functools.partial(_c_layer_prompt_kernel, tiles_per_seq=tps),
        grid=(n_tiles,),
        in_specs=[pl.BlockSpec((1, ts, d), nxt_map), pl.BlockSpec((1, ts, d), cur_map),
                  pl.BlockSpec((1, d), const),
                  pl.BlockSpec(w_bf.shape, const, pipeline_mode=pl.Buffered(1)),
                  pl.BlockSpec(lbl.shape, const), pl.BlockSpec((1, HGRN_DV), const),
                  pl.BlockSpec(wo_bf.shape, const)],
        out_specs=(pl.BlockSpec((1, ts, d), cur_map),
                   pl.BlockSpec((1, HGRN_HEADS, HGRN_DK, HGRN_DV), lambda g: (g // tps, 0, 0, 0))),
        out_shape=(jax.ShapeDtypeStruct((b, s, d), F32),
                   jax.ShapeDtypeStruct((b, HGRN_HEADS, HGRN_DK, HGRN_DV), F32)),
        scratch_shapes=[pltpu.VMEM((HGRN_HEADS, HGRN_DV, HGRN_DK), F32),
                        pltpu.VMEM((ts, HGRN_HEADS * HGRN_DV), BF16),
                        pltpu.VMEM((HGRN_HEADS, HGRN_SUB * HGRN_CHUNK, HGRN_CHUNK), F32),
                        pltpu.VMEM((HGRN_HEADS, (n_sub * (n_sub - 1) // 2) * HGRN_SUB, HGRN_CHUNK), F32),
                        pltpu.VMEM((ts, d), BF16),
                        pltpu.VMEM((HGRN_HEADS, 1, LANES), F32),
                        pltpu.VMEM((HGRN_HEADS, d, per_head), BF16),
                        head_buf(BF16), head_buf(BF16), head_buf(F32), head_buf(BF16), head_buf(BF16)],
        compiler_params=_cparams(1),
        name="c_layer_prompt",
    )(y0, y0, ng, w_bf, lbl, og, wo_bf)


def _page_copies(pt_ref, ck_hbm, cv_hbm, kbuf, vbuf, sem, seq, sl, n_pages):
    out = []
    for p in range(n_pages):
        page = pt_ref[seq, p]
        toks = pl.ds(p * PAGE_SIZE, PAGE_SIZE)
        out.append(pltpu.make_async_copy(ck_hbm.at[page], kbuf.at[sl, :, :, toks], sem.at[0, sl]))
        out.append(pltpu.make_async_copy(cv_hbm.at[page], vbuf.at[sl, :, :, toks], sem.at[1, sl]))
    return out


def _ab_in_sample_kernel(pt_ref, x_ref, ng_ref, w_ref, qg_ref, kg_ref, gsum_ref,
                         sq_ref, skn_ref, svn_ref, ssga_ref, ck_hbm, cv_hbm, *rest,
                         kv_transposed, n_pages, t_new, per_step):
    *outs, so_ref, h_ref, kbuf, vbuf, sem = rest
    step = pl.program_id(0)
    n_seq = pl.num_programs(0) * per_step
    copies = functools.partial(_page_copies, pt_ref, ck_hbm, cv_hbm, kbuf, vbuf, sem, n_pages=n_pages)

    _normalise_rows(x_ref, ng_ref, h_ref)
    stages = _ab_in_stages(h_ref, w_ref, qg_ref, kg_ref, gsum_ref, *outs, kv_transposed)
    share = -(-len(stages) // per_step)

    ahead = PAGE_SLOTS - 1

    @pl.when(step == 0)
    def _():
        for s0 in range(ahead):
            for n, cp in enumerate(copies(seq=s0, sl=s0)):
                cp.start(priority=n % 2)

    for j in range(per_step):
        seq = step * per_step + j
        slot = lax.rem(seq, PAGE_SLOTS)

        @pl.when(seq + ahead < n_seq)
        def _():
            for n, cp in enumerate(copies(seq=seq + ahead, sl=lax.rem(seq + ahead, PAGE_SLOTS))):
                cp.start(priority=n % 2)

        for cp in copies(seq=seq, sl=slot):
            cp.wait()

        def between(j=j):
            for stage in stages[j * share:(j + 1) * share]:
                stage()

        _sample_attention(j, slot, sq_ref, skn_ref, svn_ref, ssga_ref, so_ref, kbuf, vbuf,
                          t_new, n_pages * PAGE_SIZE // MOBA_BLOCK, between)


def _sample_attention(sq, slot, qs_ref, kn_ref, vn_ref, sga_ref, o_ref, kbuf, vbuf, t_new, nb, between):
    rows = qs_ref.shape[2]
    b_i = lax.broadcasted_iota(jnp.int32, (rows, LANES), 1)
    o_row = lax.broadcasted_iota(jnp.int32, (rows, rows), 0)
    o_col = lax.broadcasted_iota(jnp.int32, (rows, rows), 1)
    raw = []
    for h in range(ATT_HEADS):
        q_h = qs_ref[sq, h]
        raw.append((jnp.dot(q_h, kbuf[slot, h].astype(BF16), preferred_element_type=F32),
                    lax.dot_general(q_h, kn_ref[sq, h], NT_DIMS, preferred_element_type=F32)))
    between()
    for h in range(ATT_HEADS):
        s, s_own = raw[h]
        blocks = [s[:, j * MOBA_BLOCK:(j + 1) * MOBA_BLOCK] for j in range(nb)]
        gates = [jnp.sum(blk, axis=1, keepdims=True) for blk in blocks]
        gate = jnp.full((rows, LANES), -jnp.inf, F32)
        for i in range(nb):
            gate = jnp.where(b_i == i, gates[i], gate)
        rank = jnp.zeros((rows, LANES), F32)
        for i in range(nb):
            gi = gates[i]
            rank = rank + jnp.where(gi > gate, 1.0, jnp.where((gi == gate) & (b_i > i), 1.0, 0.0))
        sel_bias = jnp.where(rank < min(MOBA_TOPK, nb), 0.0, NEG)
        s = jnp.concatenate([blocks[j] + sel_bias[:, j:j + 1] for j in range(nb)], axis=1)

        s_own = jnp.where((o_col <= o_row) & (o_col < t_new), s_own, NEG)

        m = jnp.maximum(jnp.max(s, axis=1, keepdims=True), jnp.max(s_own, axis=1, keepdims=True))
        p = jnp.exp2(s - m)
        p_own = jnp.exp2(s_own - m)
        l = jnp.sum(p, axis=1, keepdims=True) + jnp.sum(p_own, axis=1, keepdims=True)
        o = (lax.dot_general(p.astype(BF16), vbuf[slot, h].astype(BF16), NT_DIMS, preferred_element_type=F32)
             + jnp.dot(p_own.astype(BF16), vn_ref[sq, h], preferred_element_type=F32)) / l
        o_ref[sq, h] = o * sga_ref[sq, h]


def _ab_out_sample_kernel(x_ref, matt_ref, u_ref, buf_ref, sgc_ref, tab_ref, cb_ref, lg_ref, lb_ref, w_ref,
                          y_ref, xpad_ref, m_ref):
    n_seq, t_new, _ = u_ref.shape
    hist = buf_ref.shape[1]
    per = SUBLANES // t_new
    assert per * t_new == SUBLANES and n_seq % per == 0 and hist == CONV_LEN - 1
    span = CONV_SPAN * SUBLANES
    xpad_ref[...] = jnp.zeros_like(xpad_ref)
    row8 = lax.broadcasted_iota(jnp.int32, (SUBLANES, CONV_CH), 0)

    def group(gi, carry):
        y8 = jnp.zeros((SUBLANES, CONV_CH), F32)
        for j in range(per):
            sq = gi * per + j
            off = j * t_new
            xpad_ref[j, CONV_HALO - hist + off:CONV_HALO + off, :] = buf_ref[sq]
            xpad_ref[j, CONV_HALO + off:CONV_HALO + off + t_new, :] = u_ref[sq]
            cols = []
            for lt in range(CONV_CH // LANES):
                ls = slice(lt * LANES, (lt + 1) * LANES)
                acc = jnp.broadcast_to(cb_ref[:, ls], (SUBLANES, LANES))
                for ms in range(span):
                    if _tap_used(ms):
                        acc = acc + tab_ref[ms, :, ls] * xpad_ref[j, pl.ds(ms, 1), ls]
                cols.append(acc)
            yj = jnp.concatenate(cols, axis=1)
            y8 = jnp.where((row8 >= off) & (row8 < off + t_new), yj, y8)
        r0 = pl.multiple_of(gi * SUBLANES, SUBLANES)
        rows = pl.ds(r0, SUBLANES)
        m_ref[rows, 0:ATT_WIDTH] = matt_ref[rows, :]
        m_ref[rows, ATT_WIDTH:] = _ln_silu_gate(y8, lg_ref, lb_ref, sgc_ref[rows, :]).astype(F32)
        return carry

    lax.fori_loop(0, n_seq // per, group, 0)
    y_ref[...] = x_ref[...] + jnp.dot(m_ref[...].astype(BF16), w_ref[...], preferred_element_type=F32)


def _ab_out_sample(x2d, matt2d, u3, buf, sgc2d, tab, cb, lg, lb, w_bf):
    t, d = x2d.shape
    n_seq, t_new, _ = u3.shape
    per = SUBLANES // t_new
    vm = lambda: pl.BlockSpec(memory_space=pltpu.VMEM)
    return pl.pallas_call(
        _ab_out_sample_kernel,
        in_specs=[vm() for _ in range(10)],
        out_specs=vm(),
        out_shape=jax.ShapeDtypeStruct((t, d), F32),
        scratch_shapes=[pltpu.VMEM((per, CONV_HALO + 2 * SUBLANES, CONV_CH), F32),
                        pltpu.VMEM((t, ATT_WIDTH + CONV_CH), F32)],
        compiler_params=pltpu.CompilerParams(vmem_limit_bytes=VMEM_LIMIT),
        name="ab_out_sample",
    )(x2d, matt2d, u3, buf, sgc2d, tab, cb, lg, lb, w_bf)


def _c_scan_sample_kernel(qq_ref, kk_ref, lf_ref, vv_ref, sg_ref, y0_ref, s0_ref, og_ref, w_ref,
                          y_ref, s_ref, *, t_new):
    n_groups = qq_ref.shape[0]
    per = SUBLANES // t_new
    c = SUBLANES
    dk, dv = HGRN_DK, HGRN_DV
    masks = _chunk_masks(c)
    r_i = lax.broadcasted_iota(jnp.int32, (c, c), 0)
    c_i = lax.broadcasted_iota(jnp.int32, (c, c), 1)
    tri = jnp.where(c_i <= r_i, 1.0, 0.0).astype(BF16)
    row = lax.broadcasted_iota(jnp.int32, (c, 1), 0)
    pending = []
    for g in range(n_groups):
        lf = lf_ref[g]
        for j in range(per):
            sq = g * per + j
            mine = (row >= j * t_new) & (row < (j + 1) * t_new)
            keep = jnp.where(mine, 1.0, 0.0)
            lf2 = lf * (keep * LOG2_E)
            f_all = jnp.exp2(lf2)
            b_all = sum(jnp.dot(tri, part, preferred_element_type=F32) for part in _split3(lf2))
            for h in range(HGRN_HEADS):
                ks = slice(h * dk, (h + 1) * dk)
                vs = slice(h * dv, (h + 1) * dv)
                st = s0_ref[sq, h]
                st_b = st.astype(BF16)
                v_b = (vv_ref[g, :, vs].astype(F32) * keep).astype(BF16)
                parts, kt, decay = _hgrn_pairs(
                    qq_ref[g, :, ks].astype(F32) * keep, kk_ref[g, :, ks].astype(F32) * keep,
                    b_all[:, ks], f_all[:, ks],
                    lambda qt, st_b=st_b: jnp.dot(qt, st_b, preferred_element_type=F32))
                decay_col = jnp.broadcast_to(decay, (dv, dk)).T
                s_ref[sq, h] = st * decay_col + lax.dot_general(kt, v_b, TN_DIMS, preferred_element_type=F32)
                pending.append((g, h, parts, v_b))
    m_heads = [[jnp.zeros((c, dv), F32)] * HGRN_HEADS for _ in range(n_groups)]
    for g, h, parts, v_b in pending:
        vs = slice(h * dv, (h + 1) * dv)
        o = _hgrn_combine(parts, v_b, masks)
        m_heads[g][h] = m_heads[g][h] + _head_norm_gate(o, og_ref[...], sg_ref[g, :, vs]).astype(F32)
    m1 = jnp.concatenate([jnp.concatenate(m_heads[g], axis=1) for g in range(n_groups)], axis=0).astype(BF16)
    y = jnp.dot(m1, w_ref[...], preferred_element_type=F32)
    for g in range(n_groups):
        y_ref[g] = y0_ref[g] + y[g * c:(g + 1) * c, :]


def _c_scan_sample(qq, kk, lf, vv, sg, y0, s0, og, w_bf, t_new):
    groups, rows, d = y0.shape
    per = SUBLANES // t_new
    gs = SCAN_GROUPS
    assert groups % gs == 0
    key_w, val_w = qq.shape[2], vv.shape[2]
    gmap = lambda g: (g, 0, 0)
    const = lambda g: (0, 0)
    smap = lambda g: (g, 0, 0, 0)
    sspec = pl.BlockSpec((gs * per, HGRN_HEADS, HGRN_DK, HGRN_DV), smap)
    return pl.pallas_call(
        functools.partial(_c_scan_sample_kernel, t_new=t_new),
        grid=(groups // gs,),
        in_specs=[pl.BlockSpec((gs, rows, key_w), gmap), pl.BlockSpec((gs, rows, key_w), gmap),
                  pl.BlockSpec((gs, rows, key_w), gmap), pl.BlockSpec((gs, rows, val_w), gmap),
                  pl.BlockSpec((gs, rows, val_w), gmap), pl.BlockSpec((gs, rows, d), gmap), sspec,
                  pl.BlockSpec((1, HGRN_DV), const), pl.BlockSpec(w_bf.shape, const)],
        out_specs=(pl.BlockSpec((gs, rows, d), gmap), sspec),
        out_shape=(jax.ShapeDtypeStruct((groups, rows, d), F32),
                   jax.ShapeDtypeStruct(s0.shape, F32)),
        compiler_params=_cparams(1),
        name="c_scan_sample",
    )(qq, kk, lf, vv, sg, y0, s0, og, w_bf)


def kernel(x_prompt, x_sample, cache_k, cache_v, state_conv, state_hgrn, page_table, norm_0, w_in_0, q_norm_0, k_norm_0, conv_w_0, conv_b_0, conv_ln_g_0, conv_ln_b_0, w_out_0, norm_1, w_in_1, lb_logits, o_norm_1, w_out_1):
    b, s, d = x_prompt.shape
    n_seq, t_new, _ = x_sample.shape
    n_tok = n_seq * t_new
    hist = CONV_LEN - 1
    gsum = jnp.kron(jnp.eye(ATT_WIDTH // 2 // ATT_HEAD_DIM, dtype=F32),
                    jnp.full((ATT_HEAD_DIM, ATT_HEAD_DIM), 1.0 / ATT_HEAD_DIM, F32)).astype(BF16)
    qg = jnp.tile(q_norm_0, ATT_HEADS)[None]
    kg = jnp.tile(k_norm_0, ATT_HEADS)[None]
    w_in_0b = w_in_0.astype(BF16)
    w_out_0b = w_out_0.astype(BF16)
    w_in_1b = w_in_1.astype(BF16)
    w_out_1b = w_out_1.astype(BF16)
    tab = _conv_tap_table(conv_w_0)
    conv_args = (tab, conv_b_0[None], conv_ln_g_0[None], conv_ln_b_0[None], w_out_0b)
    heads = lambda a, lead: a.reshape(lead + (ATT_HEADS, ATT_HEAD_DIM))

    qs_s, k_s, v_s, _, _, sga_s, u_s, sgc_s, _ = _ab_in(
        x_sample.reshape(n_tok, d), norm_0[None], w_in_0b, qg, kg, gsum, n_tok)
    q3 = lambda a: a.reshape(n_seq, t_new, a.shape[-1])

    def per_head(a, dt):
        a = a.reshape(n_seq, t_new, ATT_HEADS, ATT_HEAD_DIM).transpose(0, 2, 1, 3).astype(dt)
        return jnp.pad(a, ((0, 0), (0, 0), (0, SUBLANES - t_new), (0, 0)))

    sample_att = (page_table, per_head(qs_s, BF16), per_head(k_s, BF16), per_head(v_s, BF16), per_head(sga_s, F32),
                  cache_k.transpose(0, 2, 3, 1), cache_v.transpose(0, 2, 3, 1), t_new)

    qs, kt_p, vt_p, kb, vbt, sga, u_p, sgc, kmean, matt_s = _ab_in(
        x_prompt.reshape(b * s, d), norm_0[None], w_in_0b, qg, kg, gsum, ROW_TILE, seq_len=s, sample=sample_att)
    heads_t = lambda a: a.reshape(b, ATT_HEADS, ATT_HEAD_DIM, s).transpose(0, 3, 1, 2)
    r3 = lambda a: a.reshape(b, s, a.shape[-1])
    matt = _moba_prompt(r3(qs), r3(kb), vbt, kmean.reshape(b, s // MOBA_BLOCK, ATT_WIDTH), r3(sga))
    y0 = _ab_out_prompt(x_prompt, matt, r3(u_p), r3(sgc), *conv_args, ROW_TILE)
    y_prompt, hgrn_prompt = _c_layer_prompt(y0, norm_1[None], w_in_1b, lb_logits, o_norm_1[None], w_out_1b, SCAN_TILE)

    matt_s = matt_s[:, :, :t_new, :].transpose(0, 2, 1, 3)
    y0_s = _ab_out_sample(x_sample.reshape(n_tok, d), matt_s.reshape(n_tok, ATT_WIDTH), q3(u_s), state_conv,
                          sgc_s.astype(F32), *conv_args)
    qq_s, kk_s, lf_s, vv_s, sg_s = _c_in(y0_s, norm_1[None], w_in_1b, lb_logits, n_tok)
    g8 = lambda a: a.reshape(n_tok // SUBLANES, SUBLANES, a.shape[-1])
    y_s, hgrn_sample = _c_scan_sample(g8(qq_s), g8(kk_s), g8(lf_s), g8(vv_s), g8(sg_s), g8(y0_s), state_hgrn,
                                      o_norm_1[None], w_out_1b, t_new)

    conv_prompt = r3(u_p)[:, s - hist:, :]
    conv_sample = jnp.concatenate([state_conv[:, t_new:, :], q3(u_s)], axis=1)
    return (y_prompt, y_s.reshape(n_seq, t_new, d),
            heads_t(kt_p), heads_t(vt_p), heads(k_s, (n_seq, t_new)), heads(v_s, (n_seq, t_new)),
            conv_prompt, conv_sample, hgrn_prompt, hgrn_sample)
```
